```python
import jax, jax.numpy as jnp
from jax import lax
import numpy as np

D_MODEL = 2048
BATCH = 8
SEQ = 4096
DEPTH = 4

HEAD_DIM = 128
N_Q_HEADS = D_MODEL // HEAD_DIM
N_KV_HEADS = max(N_Q_HEADS // 4, 1)
GQA_GROUP = N_Q_HEADS // N_KV_HEADS
WINDOW = 128
BLOCK = 128
ROPE_DIM = HEAD_DIM // 4
ROPE_THETA = 500000.0
D_CONV = D_MODEL
CONV_WIDTH = 31
D_FF = ((8 * D_MODEL // 3 + 255) // 256) * 256
D_Q = N_Q_HEADS * HEAD_DIM
D_KV = N_KV_HEADS * HEAD_DIM
D_IN = D_Q + 2 * D_KV + 2 * D_CONV + 2 * D_MODEL
SPLITS = (D_Q, D_Q + D_KV, D_Q + 2 * D_KV, D_Q + 2 * D_KV + D_CONV,
          D_Q + 2 * D_KV + 2 * D_CONV, D_Q + 2 * D_KV + 2 * D_CONV + D_MODEL)
N_MOD = 6
DEEPNORM_ALPHA = (2.0 * DEPTH) ** 0.25
DEEPNORM_BETA = (8.0 * DEPTH) ** -0.25
LN_EPS = 1e-5
NEG_INF = -1e30

kernel_name = 'hybrid_swa_conformer_deepnorm_adaln'


def layer_norm(x, g, b):
    xf = x.astype(jnp.float32)
    mu = jnp.mean(xf, axis=-1, keepdims=True)
    xc = xf - mu
    var = jnp.mean(xc * xc, axis=-1, keepdims=True)
    y = xc * lax.rsqrt(var + LN_EPS) * g.astype(jnp.float32) + b.astype(jnp.float32)
    return y.astype(x.dtype)


def partial_rotary(t, cos, sin):
    half = ROPE_DIM // 2
    tf = t[..., :ROPE_DIM].astype(jnp.float32)
    t1, t2 = tf[..., :half], tf[..., half:]
    cs, sn = cos[None, :, None, :], sin[None, :, None, :]
    rot = jnp.concatenate([t1 * cs - t2 * sn, t2 * cs + t1 * sn], axis=-1).astype(t.dtype)
    return jnp.concatenate([rot, t[..., ROPE_DIM:]], axis=-1)


def _band(t, nb):
    B, _, H, D = t.shape
    tp = jnp.pad(t, ((0, 0), (BLOCK, BLOCK), (0, 0), (0, 0))).reshape(B, nb + 2, BLOCK, H, D)
    return jnp.concatenate([tp[:, :-2], tp[:, 1:-1], tp[:, 2:]], axis=2)


def window_gqa_with_sink(q, k, v, sink):
    B, S = q.shape[0], q.shape[1]
    nb = S // BLOCK
    qb = q.reshape(B, nb, BLOCK, N_KV_HEADS, GQA_GROUP, HEAD_DIM)
    kb, vb = _band(k, nb), _band(v, nb)
    s = jnp.einsum('bnqhgd,bnkhd->bnhgqk', qb, kb).astype(jnp.float32) * (HEAD_DIM ** -0.5)
    q_pos = jnp.arange(nb)[:, None] * BLOCK + jnp.arange(BLOCK)[None, :]
    k_pos = jnp.arange(nb)[:, None] * BLOCK - BLOCK + jnp.arange(3 * BLOCK)[None, :]
    valid = ((k_pos >= 0) & (k_pos < S))[:, None, :] & \
        (jnp.abs(k_pos[:, None, :] - q_pos[:, :, None]) <= WINDOW)
    s = jnp.where(valid[None, :, None, None], s, NEG_INF)
    sink_l = sink.astype(jnp.float32).reshape(N_KV_HEADS, GQA_GROUP)[None, None, :, :, None, None]
    m = jnp.maximum(jnp.max(s, axis=-1, keepdims=True), sink_l)
    p = jnp.exp(s - m)
    denom = jnp.sum(p, axis=-1, keepdims=True) + jnp.exp(sink_l - m)
    o = jnp.einsum('bnhgqk,bnkhd->bnqhgd', (p / denom).astype(v.dtype), vb)
    return o.reshape(B, S, N_Q_HEADS * HEAD_DIM)


def conformer_conv(glu_a, glu_b, w_dw, ln_g, ln_b):
    u = glu_a * jax.nn.sigmoid(glu_b)
    u = lax.conv_general_dilated(
        u, w_dw[:, None, :].astype(u.dtype), window_strides=(1,),
        padding=[(CONV_WIDTH // 2, CONV_WIDTH // 2)],
        dimension_numbers=('NWC', 'WIO', 'NWC'), feature_group_count=u.shape[-1])
    return jax.nn.silu(layer_norm(u, ln_g, ln_b))


def _fwd_setup_inputs(seed: int = 0) -> dict:
    key = jax.random.key(seed)
    ks = jax.random.split(key, 18)
    L = DEPTH

    def nrm(k, shape, scale):
        return jax.random.normal(k, shape, jnp.float32) * scale

    return {
        'x': nrm(ks[0], (BATCH, SEQ, D_MODEL), 1.0),
        'c': nrm(ks[1], (BATCH, D_MODEL), 1.0),
        'w_ada': nrm(ks[2], (L, D_MODEL, N_MOD * D_MODEL), 0.3 * D_MODEL ** -0.5),
        'b_ada': nrm(ks[3], (L, N_MOD * D_MODEL), 0.02),
        'w_in': nrm(ks[4], (L, D_MODEL, D_IN), D_MODEL ** -0.5),
        'sink': nrm(ks[5], (L, N_Q_HEADS), 1.0),
        'w_dw': nrm(ks[6], (L, CONV_WIDTH, D_CONV), CONV_WIDTH ** -0.5),
        'conv_ln_g': 1.0 + nrm(ks[7], (L, D_CONV), 0.02),
        'conv_ln_b': nrm(ks[8], (L, D_CONV), 0.02),
        'w_oa': nrm(ks[9], (L, D_Q, D_MODEL), D_Q ** -0.5),
        'w_ob': nrm(ks[10], (L, D_CONV, D_MODEL), D_CONV ** -0.5),
        'w_out': nrm(ks[11], (L, D_MODEL, D_MODEL), DEEPNORM_BETA * D_MODEL ** -0.5),
        'ln1_g': 1.0 + nrm(ks[12], (L, D_MODEL), 0.02),
        'ln1_b': nrm(ks[13], (L, D_MODEL), 0.02),
        'w_gu': nrm(ks[14], (L, D_MODEL, 2 * D_FF), D_MODEL ** -0.5),
        'w_down': nrm(ks[15], (L, D_FF, D_MODEL), DEEPNORM_BETA * D_FF ** -0.5),
        'ln2_g': 1.0 + nrm(ks[16], (L, D_MODEL), 0.02),
        'ln2_b': nrm(ks[17], (L, D_MODEL), 0.02),
    }


def _fwd_reference(x, c, w_ada, b_ada, w_in, sink, w_dw, conv_ln_g, conv_ln_b, w_oa, w_ob, w_out,
              ln1_g, ln1_b, w_gu, w_down, ln2_g, ln2_b):
    B, S, _ = x.shape
    pos = jnp.arange(S, dtype=jnp.float32)
    inv_freq = ROPE_THETA ** (-jnp.arange(0, ROPE_DIM, 2, dtype=jnp.float32) / ROPE_DIM)
    ang = pos[:, None] * inv_freq[None, :]
    cos, sin = jnp.cos(ang), jnp.sin(ang)
    c_act = jax.nn.silu(c)
    for l in range(DEPTH):
        mod = (c_act @ w_ada[l] + b_ada[l])[:, None, :]
        sh_a, sc_a, gt_a, sh_f, sc_f, gt_f = jnp.split(mod, N_MOD, axis=-1)
        h = x * (1 + sc_a) + sh_a
        q, k, v, glu_a, glu_b, g_a, g_b = jnp.split(h @ w_in[l], SPLITS, axis=-1)
        q = partial_rotary(q.reshape(B, S, N_Q_HEADS, HEAD_DIM), cos, sin)
        k = partial_rotary(k.reshape(B, S, N_KV_HEADS, HEAD_DIM), cos, sin)
        v = v.reshape(B, S, N_KV_HEADS, HEAD_DIM)
        y_a = window_gqa_with_sink(q, k, v, sink[l]) @ w_oa[l]
        y_b = conformer_conv(glu_a, glu_b, w_dw[l], conv_ln_g[l], conv_ln_b[l]) @ w_ob[l]
        merged = jax.nn.sigmoid(g_a) * y_a + jax.nn.sigmoid(g_b) * y_b
        x = layer_norm(DEEPNORM_ALPHA * x + (1 + gt_a) * (merged @ w_out[l]), ln1_g[l], ln1_b[l])
        h = x * (1 + sc_f) + sh_f
        gate, up = jnp.split(h @ w_gu[l], 2, axis=-1)
        ffn = (jax.nn.silu(gate) * up) @ w_down[l]
        x = layer_norm(DEEPNORM_ALPHA * x + (1 + gt_f) * ffn, ln2_g[l], ln2_b[l])
    return x


import jax as _jax
import jax.numpy as _jnp

TWIN_FORMAT = 'train_step'
FWD_PARAMS = ['x', 'c', 'w_ada', 'b_ada', 'w_in', 'sink', 'w_dw', 'conv_ln_g', 'conv_ln_b', 'w_oa', 'w_ob', 'w_out', 'ln1_g', 'ln1_b', 'w_gu', 'w_down', 'ln2_g', 'ln2_b']
TWIN_WEIGHTS = ['w_ada', 'b_ada', 'w_in', 'sink', 'w_dw', 'conv_ln_g', 'conv_ln_b', 'w_oa', 'w_ob', 'w_out', 'ln1_g', 'ln1_b', 'w_gu', 'w_down', 'ln2_g', 'ln2_b']
TWIN_DIFF_INPUT = 'x'
TWIN_INPUTS = ['x', 'c', 'w_ada', 'b_ada', 'w_in', 'sink', 'w_dw', 'conv_ln_g', 'conv_ln_b', 'w_oa', 'w_ob', 'w_out', 'ln1_g', 'ln1_b', 'w_gu', 'w_down', 'ln2_g', 'ln2_b', 'loss_target', 'm_w_ada', 'm_b_ada', 'm_w_in', 'm_sink', 'm_w_dw', 'm_conv_ln_g', 'm_conv_ln_b', 'm_w_oa', 'm_w_ob', 'm_w_out', 'm_ln1_g', 'm_ln1_b', 'm_w_gu', 'm_w_down', 'm_ln2_g', 'm_ln2_b', 'v_w_ada', 'v_b_ada', 'v_w_in', 'v_sink', 'v_w_dw', 'v_conv_ln_g', 'v_conv_ln_b', 'v_w_oa', 'v_w_ob', 'v_w_out', 'v_ln1_g', 'v_ln1_b', 'v_w_gu', 'v_w_down', 'v_ln2_g', 'v_ln2_b']
TWIN_OUTPUTS = ['loss', 'grad_x', 'grad_w_ada', 'grad_b_ada', 'grad_w_in', 'grad_sink', 'grad_w_dw', 'grad_conv_ln_g', 'grad_conv_ln_b', 'grad_w_oa', 'grad_w_ob', 'grad_w_out', 'grad_ln1_g', 'grad_ln1_b', 'grad_w_gu', 'grad_w_down', 'grad_ln2_g', 'grad_ln2_b', 'delta_w_ada', 'delta_b_ada', 'delta_w_in', 'delta_sink', 'delta_w_dw', 'delta_conv_ln_g', 'delta_conv_ln_b', 'delta_w_oa', 'delta_w_ob', 'delta_w_out', 'delta_ln1_g', 'delta_ln1_b', 'delta_w_gu', 'delta_w_down', 'delta_ln2_g', 'delta_ln2_b', 'new_m_w_ada', 'new_m_b_ada', 'new_m_w_in', 'new_m_sink', 'new_m_w_dw', 'new_m_conv_ln_g', 'new_m_conv_ln_b', 'new_m_w_oa', 'new_m_w_ob', 'new_m_w_out', 'new_m_ln1_g', 'new_m_ln1_b', 'new_m_w_gu', 'new_m_w_down', 'new_m_ln2_g', 'new_m_ln2_b', 'new_v_w_ada', 'new_v_b_ada', 'new_v_w_in', 'new_v_sink', 'new_v_w_dw', 'new_v_conv_ln_g', 'new_v_conv_ln_b', 'new_v_w_oa', 'new_v_w_ob', 'new_v_w_out', 'new_v_ln1_g', 'new_v_ln1_b', 'new_v_w_gu', 'new_v_w_down', 'new_v_ln2_g', 'new_v_ln2_b']
TWIN_LEAF_KINDS = {'loss': 'loss', 'grad_x': 'grad_x', 'grad_w_ada': 'grad_w', 'grad_b_ada': 'grad_w', 'grad_w_in': 'grad_w', 'grad_sink': 'grad_w', 'grad_w_dw': 'grad_w', 'grad_conv_ln_g': 'grad_w', 'grad_conv_ln_b': 'grad_w', 'grad_w_oa': 'grad_w', 'grad_w_ob': 'grad_w', 'grad_w_out': 'grad_w', 'grad_ln1_g': 'grad_w', 'grad_ln1_b': 'grad_w', 'grad_w_gu': 'grad_w', 'grad_w_down': 'grad_w', 'grad_ln2_g': 'grad_w', 'grad_ln2_b': 'grad_w', 'delta_w_ada': 'delta_w', 'delta_b_ada': 'delta_w', 'delta_w_in': 'delta_w', 'delta_sink': 'delta_w', 'delta_w_dw': 'delta_w', 'delta_conv_ln_g': 'delta_w', 'delta_conv_ln_b': 'delta_w', 'delta_w_oa': 'delta_w', 'delta_w_ob': 'delta_w', 'delta_w_out': 'delta_w', 'delta_ln1_g': 'delta_w', 'delta_ln1_b': 'delta_w', 'delta_w_gu': 'delta_w', 'delta_w_down': 'delta_w', 'delta_ln2_g': 'delta_w', 'delta_ln2_b': 'delta_w', 'new_m_w_ada': 'new_m', 'new_m_b_ada': 'new_m', 'new_m_w_in': 'new_m', 'new_m_sink': 'new_m', 'new_m_w_dw': 'new_m', 'new_m_conv_ln_g': 'new_m', 'new_m_conv_ln_b': 'new_m', 'new_m_w_oa': 'new_m', 'new_m_w_ob': 'new_m', 'new_m_w_out': 'new_m', 'new_m_ln1_g': 'new_m', 'new_m_ln1_b': 'new_m', 'new_m_w_gu': 'new_m', 'new_m_w_down': 'new_m', 'new_m_ln2_g': 'new_m', 'new_m_ln2_b': 'new_m', 'new_v_w_ada': 'new_v', 'new_v_b_ada': 'new_v', 'new_v_w_in': 'new_v', 'new_v_sink': 'new_v', 'new_v_w_dw': 'new_v', 'new_v_conv_ln_g': 'new_v', 'new_v_conv_ln_b': 'new_v', 'new_v_w_oa': 'new_v', 'new_v_w_ob': 'new_v', 'new_v_w_out': 'new_v', 'new_v_ln1_g': 'new_v', 'new_v_ln1_b': 'new_v', 'new_v_w_gu': 'new_v', 'new_v_w_down': 'new_v', 'new_v_ln2_g': 'new_v', 'new_v_ln2_b': 'new_v'}


def _forward(args):
    return _fwd_reference(*[args[k] for k in FWD_PARAMS])


def _output_shape():
    def fwd():
        inp = _fwd_setup_inputs(0)
        return _fwd_reference(*[inp[k] for k in FWD_PARAMS])
    out = _jax.eval_shape(fwd)
    return out.shape, out.dtype

N_MICROBATCH = 1
ADAM_LR = 0.001
ADAM_B1 = 0.9
ADAM_B2 = 0.999
ADAM_EPS = 1e-08
ADAM_WD = 0.01
ADAM_STEP = 10
PER_EXAMPLE_BATCH_AXIS = {'x': 0, 'c': 0, 'loss_target': 0}
SHARED_INPUTS = []
_WEIGHT_DTYPES = {'w_ada': _jnp.float32, 'b_ada': _jnp.float32, 'w_in': _jnp.float32, 'sink': _jnp.float32, 'w_dw': _jnp.float32, 'conv_ln_g': _jnp.float32, 'conv_ln_b': _jnp.float32, 'w_oa': _jnp.float32, 'w_ob': _jnp.float32, 'w_out': _jnp.float32, 'ln1_g': _jnp.float32, 'ln1_b': _jnp.float32, 'w_gu': _jnp.float32, 'w_down': _jnp.float32, 'ln2_g': _jnp.float32, 'ln2_b': _jnp.float32}
MOMENT_SCALE = {'w_ada': 9.981545e-03, 'b_ada': 2.112568e-02, 'w_in': 4.276881e-03, 'sink': 3.124294e-04, 'w_dw': 8.056356e-03, 'conv_ln_g': 1.324066e-02, 'conv_ln_b': 1.672120e-02, 'w_oa': 3.812631e-03, 'w_ob': 9.480389e-03, 'w_out': 2.446825e-02, 'ln1_g': 5.524781e-01, 'ln1_b': 2.749227e-01, 'w_gu': 8.912190e-03, 'w_down': 3.468613e-02, 'ln2_g': 8.049021e+00, 'ln2_b': 6.310430e-01}


def _to_microbatches(a, axis):
    t = _jnp.moveaxis(a, axis, 0)
    t = t.reshape((N_MICROBATCH, t.shape[0] // N_MICROBATCH) + t.shape[1:])
    return _jnp.moveaxis(t, 1, axis + 1)


def setup_inputs(seed: int = 0) -> dict:
    inp = _fwd_setup_inputs(seed)
    key = _jax.random.fold_in(_jax.random.key(seed), 7919)
    shape, _ = _output_shape()
    out = dict(inp)
    out["loss_target"] = _jax.random.normal(_jax.random.fold_in(key, 0), shape, _jnp.float32)
    for i, name in enumerate(TWIN_WEIGHTS):
        w = inp[name].astype(_jnp.float32)
        if MOMENT_SCALE is None:
            s = _jnp.sqrt(_jnp.mean(_jnp.square(w)) + 1e-30)
        else:
            s = MOMENT_SCALE[name]
        km, kv = _jax.random.split(_jax.random.fold_in(key, i + 1))
        out[name] = w
        out["m_" + name] = s * _jax.random.normal(km, w.shape, _jnp.float32)
        out["v_" + name] = (s * s) * _jax.random.uniform(kv, w.shape, _jnp.float32, 0.5, 1.5)
    if N_MICROBATCH > 1:
        for name, axis in PER_EXAMPLE_BATCH_AXIS.items():
            out[name] = _to_microbatches(out[name], axis)
    return {'x': out['x'], 'c': out['c'], 'w_ada': out['w_ada'], 'b_ada': out['b_ada'], 'w_in': out['w_in'], 'sink': out['sink'], 'w_dw': out['w_dw'], 'conv_ln_g': out['conv_ln_g'], 'conv_ln_b': out['conv_ln_b'], 'w_oa': out['w_oa'], 'w_ob': out['w_ob'], 'w_out': out['w_out'], 'ln1_g': out['ln1_g'], 'ln1_b': out['ln1_b'], 'w_gu': out['w_gu'], 'w_down': out['w_down'], 'ln2_g': out['ln2_g'], 'ln2_b': out['ln2_b'], 'loss_target': out['loss_target'], 'm_w_ada': out['m_w_ada'], 'm_b_ada': out['m_b_ada'], 'm_w_in': out['m_w_in'], 'm_sink': out['m_sink'], 'm_w_dw': out['m_w_dw'], 'm_conv_ln_g': out['m_conv_ln_g'], 'm_conv_ln_b': out['m_conv_ln_b'], 'm_w_oa': out['m_w_oa'], 'm_w_ob': out['m_w_ob'], 'm_w_out': out['m_w_out'], 'm_ln1_g': out['m_ln1_g'], 'm_ln1_b': out['m_ln1_b'], 'm_w_gu': out['m_w_gu'], 'm_w_down': out['m_w_down'], 'm_ln2_g': out['m_ln2_g'], 'm_ln2_b': out['m_ln2_b'], 'v_w_ada': out['v_w_ada'], 'v_b_ada': out['v_b_ada'], 'v_w_in': out['v_w_in'], 'v_sink': out['v_sink'], 'v_w_dw': out['v_w_dw'], 'v_conv_ln_g': out['v_conv_ln_g'], 'v_conv_ln_b': out['v_conv_ln_b'], 'v_w_oa': out['v_w_oa'], 'v_w_ob': out['v_w_ob'], 'v_w_out': out['v_w_out'], 'v_ln1_g': out['v_ln1_g'], 'v_ln1_b': out['v_ln1_b'], 'v_w_gu': out['v_w_gu'], 'v_w_down': out['v_w_down'], 'v_ln2_g': out['v_ln2_g'], 'v_ln2_b': out['v_ln2_b']}


def _loss(weights, diff, rest, loss_target):
    with _jax.named_scope("forward"):
        args = {**rest, TWIN_DIFF_INPUT: diff, **{k: w.astype(_WEIGHT_DTYPES[k]) for k, w in weights.items()}}
        y = _forward(args)
    with _jax.named_scope("loss_head"):
        err = _jnp.square(y.astype(_jnp.float32) - loss_target)
        return 0.5 * _jnp.sum(_jnp.mean(err, axis=-1)) if err.ndim else 0.5 * err


def _adamw(w, g, m, v):
    m = ADAM_B1 * m + (1.0 - ADAM_B1) * g
    v = ADAM_B2 * v + (1.0 - ADAM_B2) * _jnp.square(g)
    m_hat = m / (1.0 - ADAM_B1 ** ADAM_STEP)
    v_hat = v / (1.0 - ADAM_B2 ** ADAM_STEP)
    delta = -ADAM_LR * (m_hat / (_jnp.sqrt(v_hat) + ADAM_EPS) + ADAM_WD * w)
    return delta, m, v


def reference(x, c, w_ada, b_ada, w_in, sink, w_dw, conv_ln_g, conv_ln_b, w_oa, w_ob, w_out, ln1_g, ln1_b, w_gu, w_down, ln2_g, ln2_b, loss_target, m_w_ada, m_b_ada, m_w_in, m_sink, m_w_dw, m_conv_ln_g, m_conv_ln_b, m_w_oa, m_w_ob, m_w_out, m_ln1_g, m_ln1_b, m_w_gu, m_w_down, m_ln2_g, m_ln2_b, v_w_ada, v_b_ada, v_w_in, v_sink, v_w_dw, v_conv_ln_g, v_conv_ln_b, v_w_oa, v_w_ob, v_w_out, v_ln1_g, v_ln1_b, v_w_gu, v_w_down, v_ln2_g, v_ln2_b):
    given = dict(x=x, c=c, w_ada=w_ada, b_ada=b_ada, w_in=w_in, sink=sink, w_dw=w_dw, conv_ln_g=conv_ln_g, conv_ln_b=conv_ln_b, w_oa=w_oa, w_ob=w_ob, w_out=w_out, ln1_g=ln1_g, ln1_b=ln1_b, w_gu=w_gu, w_down=w_down, ln2_g=ln2_g, ln2_b=ln2_b, loss_target=loss_target, m_w_ada=m_w_ada, m_b_ada=m_b_ada, m_w_in=m_w_in, m_sink=m_sink, m_w_dw=m_w_dw, m_conv_ln_g=m_conv_ln_g, m_conv_ln_b=m_conv_ln_b, m_w_oa=m_w_oa, m_w_ob=m_w_ob, m_w_out=m_w_out, m_ln1_g=m_ln1_g, m_ln1_b=m_ln1_b, m_w_gu=m_w_gu, m_w_down=m_w_down, m_ln2_g=m_ln2_g, m_ln2_b=m_ln2_b, v_w_ada=v_w_ada, v_b_ada=v_b_ada, v_w_in=v_w_in, v_sink=v_sink, v_w_dw=v_w_dw, v_conv_ln_g=v_conv_ln_g, v_conv_ln_b=v_conv_ln_b, v_w_oa=v_w_oa, v_w_ob=v_w_ob, v_w_out=v_w_out, v_ln1_g=v_ln1_g, v_ln1_b=v_ln1_b, v_w_gu=v_w_gu, v_w_down=v_w_down, v_ln2_g=v_ln2_g, v_ln2_b=v_ln2_b)
    weights = {n: given[n] for n in TWIN_WEIGHTS}
    shared = {n: given[n] for n in SHARED_INPUTS}
    per_example = {n: given[n] for n in ['x', 'c']}
    grad_fn = _jax.value_and_grad(_loss, argnums=(0, 1))

    def one_microbatch(ex, loss_target):
        ex = dict(ex)
        diff = ex.pop(TWIN_DIFF_INPUT)
        return grad_fn(weights, diff, {**shared, **ex}, loss_target)

    if N_MICROBATCH == 1:
        loss, (grad_w, grad_x) = one_microbatch(per_example, given["loss_target"])
    else:
        def body(carry, xs):
            loss_sum, grad_sum = carry
            l_k, (gw_k, gx_k) = one_microbatch(xs[0], xs[1])
            with _jax.named_scope("update"):
                return (loss_sum + l_k, _jax.tree.map(_jnp.add, grad_sum, gw_k)), gx_k

        init = (_jnp.zeros((), _jnp.float32), _jax.tree.map(_jnp.zeros_like, weights))
        (loss, grad_w), grad_x = _jax.lax.scan(body, init, (per_example, given["loss_target"]))
    with _jax.named_scope("update"):
        delta_w, new_m, new_v = {}, {}, {}
        for n in TWIN_WEIGHTS:
            delta_w[n], new_m[n], new_v[n] = _adamw(weights[n], grad_w[n], given["m_" + n], given["v_" + n])
    return (loss, grad_x, *[grad_w[n] for n in TWIN_WEIGHTS], *[delta_w[n] for n in TWIN_WEIGHTS],
            *[new_m[n] for n in TWIN_WEIGHTS], *[new_v[n] for n in TWIN_WEIGHTS])
```

```python
import functools
import math

import jax
import jax.numpy as jnp
from jax import lax
from jax.experimental import pallas as pl
from jax.experimental.pallas import tpu as pltpu

F32 = jnp.float32
BF16 = jnp.bfloat16
MESH = pl.DeviceIdType.MESH

HEAD_DIM = 128
GQA_GROUP = 4
WINDOW = 128
BLOCK = 128
BAND = 3 * BLOCK
ROPE_DIM = HEAD_DIM // 4
ROPE_THETA = 500000.0
CONV_WIDTH = 31
CONV_PAD = CONV_WIDTH // 2
CONV_HALO = 16
N_MOD = 6
LN_EPS = 1e-5
NEG_INF = -1e30
ADAM_LR = 0.001
ADAM_B1 = 0.9
ADAM_B2 = 0.999
ADAM_EPS = 1e-08
ADAM_WD = 0.01
ADAM_STEP = 10

LANE = 128
V7X_VMEM_LIMIT = 56 * 1024 * 1024
ROW_TILE = 256
CONV_ROWS = 32
CONV_LANES = 256

HBM_SPEC = pl.BlockSpec(memory_space=pltpu.HBM)


def _params(*sem):
    return pltpu.CompilerParams(dimension_semantics=sem, vmem_limit_bytes=V7X_VMEM_LIMIT)


def _pick(n, cands, even=False):
    for t in cands:
        if n % t == 0 and (not even or (n // t) % 2 == 0):
            return t
    raise ValueError(f"no tile for {n} in {cands}")


BLOCK_BUDGET = 10 * 1024 * 1024


def _rows_within(n_rows, bytes_per_row, cands=(256, 128, 64, 32, 16, 8)):
    fit = [t for t in cands if n_rows % t == 0]
    for t in fit:
        if t * bytes_per_row <= BLOCK_BUDGET:
            return t
    return fit[-1]


def _sigmoid(v):
    return jax.nn.sigmoid(v)


def _const_map(ndim):
    return lambda *_: (0,) * ndim


def _rowwise(name, fn, n_rows, row_ins, vec_ins, row_outs, vec_outs=()):
    per_row = sum(w * a.dtype.itemsize for a, _, w in row_ins) + sum(w * jnp.dtype(dt).itemsize for w, dt in row_outs)
    tr = _rows_within(n_rows, per_row, (ROW_TILE, 128, 64))
    in_specs, args, pieces = [], [], []
    for arr, off, width in row_ins:
        bw = math.gcd(off, width) if off else width
        assert bw % LANE == 0 and arr.shape[0] == n_rows
        pieces.append(width // bw)
        for p in range(width // bw):
            in_specs.append(pl.BlockSpec((tr, bw), functools.partial(lambda i, blk: (i, blk), blk=off // bw + p)))
            args.append(arr)
    for v in vec_ins:
        in_specs.append(pl.BlockSpec(v.shape, _const_map(v.ndim)))
        args.append(v)
    out_shape = [jax.ShapeDtypeStruct((n_rows, w), dt) for w, dt in row_outs]
    out_specs = [pl.BlockSpec((tr, w), lambda i: (i, 0)) for w, _ in row_outs]
    for shp, dt in vec_outs:
        out_shape.append(jax.ShapeDtypeStruct(shp, dt))
        out_specs.append(pl.BlockSpec(shp, _const_map(len(shp))))
    n_in, n_row_out = len(args), len(row_outs)

    def body(*refs):
        in_refs, out_refs = refs[:n_in], refs[n_in:]
        vals, k = [], 0
        for npc in pieces:
            ps = [in_refs[k + p][...] for p in range(npc)]
            k += npc
            vals.append(ps[0] if npc == 1 else jnp.concatenate(ps, axis=1))
        for _ in vec_ins:
            vals.append(in_refs[k][...])
            k += 1
        outs = fn(*vals)
        if not isinstance(outs, (tuple, list)):
            outs = (outs,)
        assert len(outs) == len(out_refs)
        for j in range(n_row_out):
            out_refs[j][...] = outs[j].astype(out_refs[j].dtype)
        if vec_outs:
            @pl.when(pl.program_id(0) == 0)
            def _():
                for j in range(n_row_out, len(out_refs)):
                    out_refs[j][...] = jnp.zeros(out_refs[j].shape, out_refs[j].dtype)
            for j in range(n_row_out, len(out_refs)):
                out_refs[j][...] += outs[j].astype(out_refs[j].dtype)

    res = pl.pallas_call(
        body, name=name, grid=(n_rows // tr,), in_specs=in_specs, out_specs=out_specs, out_shape=out_shape,
        compiler_params=_params("arbitrary"),
    )(*args)
    return res[0] if len(res) == 1 else res


def _mm(name, a, b, mode, out_dtype, b_layer=None, split=None, c_idx=None):
    bshape = b.shape[1:] if b_layer is not None else b.shape
    if mode == "nn":
        (m, k), (k2, n) = a.shape, bshape
        dims = (((1,), (0,)), ((), ()))
    elif mode == "nt":
        (m, k), (n, k2) = a.shape, bshape
        dims = (((1,), (1,)), ((), ()))
    else:
        (k, m), (k2, n) = a.shape, bshape
        dims = (((0,), (0,)), ((), ()))
    assert k == k2, (name, a.shape, b.shape)
    tm = _pick(m, (1024, 512, 256, 128, 16), even=(split == "rows"))
    tn = _pick(n, (1024, 512, 256, 128), even=(split == "cols"))
    tk = _pick(k, (2048, 1408, 1024, 704, 512, 256, 128))
    ni, nj, nk = m // tm, n // tn, k // tk

    if mode == "nn":
        a_spec = pl.BlockSpec((tm, tk), lambda i, j, kk, *_: (i, kk))
        b_blk, b_map = (tk, tn), (lambda i, j, kk: (kk, j))
    elif mode == "nt":
        a_spec = pl.BlockSpec((tm, tk), lambda i, j, kk, *_: (i, kk))
        b_blk, b_map = (tn, tk), (lambda i, j, kk: (j, kk))
    else:
        a_spec = pl.BlockSpec((tk, tm), lambda i, j, kk, *_: (kk, i))
        b_blk, b_map = (tk, tn), (lambda i, j, kk: (kk, j))
    if b_layer is None:
        b_spec = pl.BlockSpec(b_blk, lambda i, j, kk, *_: b_map(i, j, kk))
    else:
        b_spec = pl.BlockSpec((None,) + b_blk, lambda i, j, kk, *_: (b_layer,) + b_map(i, j, kk))

    if split is None:
        out_shape = jax.ShapeDtypeStruct((m, n), out_dtype)
        o_spec = pl.BlockSpec((tm, tn), lambda i, j, kk, *_: (i, j))
    elif split == "rows":
        out_shape = jax.ShapeDtypeStruct((2, m // 2, n), out_dtype)
        o_spec = pl.BlockSpec(
            (None, tm, tn), lambda i, j, kk, c_ref: (jnp.where(i // (ni // 2) == c_ref[0], 0, 1), i % (ni // 2), j))
    else:
        out_shape = jax.ShapeDtypeStruct((2, m, n // 2), out_dtype)
        o_spec = pl.BlockSpec(
            (None, tm, tn), lambda i, j, kk, c_ref: (jnp.where(j // (nj // 2) == c_ref[0], 0, 1), i, j % (nj // 2)))

    def body(*refs):
        if split is not None:
            refs = refs[1:]
        a_ref, b_ref, o_ref = refs[:3]
        part = lax.dot_general(a_ref[...].astype(BF16), b_ref[...].astype(BF16), dims, preferred_element_type=F32)
        if nk == 1:
            o_ref[...] = part.astype(o_ref.dtype)
        else:
            acc_ref = refs[3]
            kk = pl.program_id(2)

            @pl.when(kk == 0)
            def _():
                acc_ref[...] = part

            @pl.when(kk > 0)
            def _():
                acc_ref[...] += part

            @pl.when(kk == nk - 1)
            def _():
                o_ref[...] = acc_ref[...].astype(o_ref.dtype)

    scratch = [] if nk == 1 else [pltpu.VMEM((tm, tn), F32)]
    params = _params("parallel", "parallel", "arbitrary")
    if split is None:
        return pl.pallas_call(
            body, name=name, grid=(ni, nj, nk), in_specs=[a_spec, b_spec], out_specs=o_spec, out_shape=out_shape,
            scratch_shapes=scratch, compiler_params=params,
        )(a, b)
    grid_spec = pltpu.PrefetchScalarGridSpec(
        num_scalar_prefetch=1, grid=(ni, nj, nk), in_specs=[a_spec, b_spec], out_specs=o_spec, scratch_shapes=scratch)
    return pl.pallas_call(body, name=name, grid_spec=grid_spec, out_shape=out_shape, compiler_params=params)(c_idx, a, b)


def _attn_tile(seq):
    return _pick(seq, (256, 128))


def _attn_scores(q_ref, k_ref, v_ref, sink_ref, kvh, i, b, g, tq, seq):
    q0 = i * tq + b * BLOCK
    k_off = pl.multiple_of(jnp.clip(q0 - BLOCK, 0, seq - BAND), BLOCK)
    kw = k_ref[pl.ds(k_off, BAND), :]
    vw = v_ref[pl.ds(k_off, BAND), :]
    q_pos = q0 + lax.broadcasted_iota(jnp.int32, (BLOCK, BAND), 0)
    k_pos = k_off + lax.broadcasted_iota(jnp.int32, (BLOCK, BAND), 1)
    valid = jnp.abs(k_pos - q_pos) <= WINDOW
    qg = q_ref[b * BLOCK:(b + 1) * BLOCK, g * HEAD_DIM:(g + 1) * HEAD_DIM]
    s = lax.dot_general(qg, kw, (((1,), (1,)), ((), ())), preferred_element_type=F32) * (HEAD_DIM ** -0.5)
    s = jnp.where(valid, s, NEG_INF)
    sink = sink_ref[pl.ds(kvh * GQA_GROUP + g, 1), :][:, :1]
    m = jnp.maximum(jnp.max(s, axis=-1, keepdims=True), sink)
    p = jnp.exp(s - m)
    p_sink = jnp.exp(sink - m)
    denom = jnp.sum(p, axis=-1, keepdims=True) + p_sink
    return k_off, kw, vw, qg, p / denom, p_sink / denom


def _attn_fwd(name, qr, kr, vb, sink_b):
    seq, dq = qr.shape
    nkv = kr.shape[1] // HEAD_DIM
    tq = _attn_tile(seq)
    gw = GQA_GROUP * HEAD_DIM

    def body(q_ref, k_ref, v_ref, sink_ref, o_ref):
        kvh, i = pl.program_id(0), pl.program_id(1)
        for b in range(tq // BLOCK):
            for g in range(GQA_GROUP):
                _, _, vw, _, pn, _ = _attn_scores(q_ref, k_ref, v_ref, sink_ref, kvh, i, b, g, tq, seq)
                o = jnp.dot(pn.astype(BF16), vw, preferred_element_type=F32)
                o_ref[b * BLOCK:(b + 1) * BLOCK, g * HEAD_DIM:(g + 1) * HEAD_DIM] = o.astype(o_ref.dtype)

    return pl.pallas_call(
        body, name=name, grid=(nkv, seq // tq),
        in_specs=[
            pl.BlockSpec((tq, gw), lambda h, i: (i, h)),
            pl.BlockSpec((seq, HEAD_DIM), lambda h, i: (0, h)),
            pl.BlockSpec((seq, HEAD_DIM), lambda h, i: (0, h)),
            pl.BlockSpec(sink_b.shape, lambda h, i: (0, 0)),
        ],
        out_specs=pl.BlockSpec((tq, gw), lambda h, i: (i, h)),
        out_shape=jax.ShapeDtypeStruct((seq, dq), BF16),
        compiler_params=_params("arbitrary", "arbitrary"),
    )(qr, kr, vb, sink_b)


def _attn_bwd(name, qr, kr, vb, sink_b, d_att):
    seq, dq = qr.shape
    dkv = kr.shape[1]
    nkv = dkv // HEAD_DIM
    tq = _attn_tile(seq)
    gw = GQA_GROUP * HEAD_DIM
    tn_dims = (((0,), (0,)), ((), ()))

    def body(q_ref, k_ref, v_ref, sink_ref, do_ref, dq_ref, dk_ref, dv_ref, dsink_ref):
        kvh, i = pl.program_id(0), pl.program_id(1)

        @pl.when(i == 0)
        def _():
            dk_ref[...] = jnp.zeros(dk_ref.shape, F32)
            dv_ref[...] = jnp.zeros(dv_ref.shape, F32)

        @pl.when((i == 0) & (kvh == 0))
        def _():
            dsink_ref[...] = jnp.zeros(dsink_ref.shape, F32)

        for b in range(tq // BLOCK):
            dk_acc = jnp.zeros((BAND, HEAD_DIM), F32)
            dv_acc = jnp.zeros((BAND, HEAD_DIM), F32)
            k_off = None
            for g in range(GQA_GROUP):
                k_off, kw, vw, qg, pn, pn_sink = _attn_scores(q_ref, k_ref, v_ref, sink_ref, kvh, i, b, g, tq, seq)
                dog = do_ref[b * BLOCK:(b + 1) * BLOCK, g * HEAD_DIM:(g + 1) * HEAD_DIM]
                dp = lax.dot_general(dog, vw, (((1,), (1,)), ((), ())), preferred_element_type=F32)
                delta = jnp.sum(pn * dp, axis=-1, keepdims=True)
                ds = (pn * (dp - delta) * (HEAD_DIM ** -0.5)).astype(BF16)
                dq_ref[b * BLOCK:(b + 1) * BLOCK, g * HEAD_DIM:(g + 1) * HEAD_DIM] = jnp.dot(
                    ds, kw, preferred_element_type=F32)
                dk_acc += lax.dot_general(ds, qg, tn_dims, preferred_element_type=F32)
                dv_acc += lax.dot_general(pn.astype(BF16), dog, tn_dims, preferred_element_type=F32)
                d_sink = -jnp.sum(pn_sink * delta, axis=0, keepdims=True)
                row = pl.ds(kvh * GQA_GROUP + g, 1)
                dsink_ref[row, :] += jnp.broadcast_to(d_sink, (1, LANE))
            dk_ref[pl.ds(k_off, BAND), :] += dk_acc
            dv_ref[pl.ds(k_off, BAND), :] += dv_acc

    return pl.pallas_call(
        body, name=name, grid=(nkv, seq // tq),
        in_specs=[
            pl.BlockSpec((tq, gw), lambda h, i: (i, h)),
            pl.BlockSpec((seq, HEAD_DIM), lambda h, i: (0, h)),
            pl.BlockSpec((seq, HEAD_DIM), lambda h, i: (0, h)),
            pl.BlockSpec(sink_b.shape, lambda h, i: (0, 0)),
            pl.BlockSpec((tq, gw), lambda h, i: (i, h)),
        ],
        out_specs=[
            pl.BlockSpec((tq, gw), lambda h, i: (i, h)),
            pl.BlockSpec((seq, HEAD_DIM), lambda h, i: (0, h)),
            pl.BlockSpec((seq, HEAD_DIM), lambda h, i: (0, h)),
            pl.BlockSpec(sink_b.shape, lambda h, i: (0, 0)),
        ],
        out_shape=[
            jax.ShapeDtypeStruct((seq, dq), F32),
            jax.ShapeDtypeStruct((seq, dkv), F32),
            jax.ShapeDtypeStruct((seq, dkv), F32),
            jax.ShapeDtypeStruct(sink_b.shape, F32),
        ],
        compiler_params=_params("arbitrary", "arbitrary"),
    )(qr, kr, vb, sink_b, d_att)


def _halo_specs(tr, width, n_rows):
    per, last = tr // CONV_HALO, n_rows // CONV_HALO - 1
    return [
        pl.BlockSpec((tr, width), lambda i: (i, 0)),
        pl.BlockSpec((CONV_HALO, width), lambda i: (jnp.maximum(i * per - 1, 0), 0)),
        pl.BlockSpec((CONV_HALO, width), lambda i: (jnp.minimum((i + 1) * per, last), 0)),
    ]


def _fill_ext(ext_ref, main_ref, prev_ref, next_ref, n_steps, tr):
    i = pl.program_id(0)
    ext_ref[0:CONV_HALO, :] = jnp.where(i > 0, prev_ref[...], 0.0)
    ext_ref[CONV_HALO:CONV_HALO + tr, :] = main_ref[...]
    ext_ref[CONV_HALO + tr:, :] = jnp.where(i < n_steps - 1, next_ref[...], 0.0)


def _conv_taps(ext_ref, w_ref, out_ref, tr, width, flip):
    cw = min(CONV_LANES, width)
    for cc in range(width // cw):
        cols = slice(cc * cw, (cc + 1) * cw)
        for rc in range(tr // CONV_ROWS):
            acc = jnp.zeros((CONV_ROWS, cw), F32)
            for t in range(CONV_WIDTH):
                wt = CONV_WIDTH - 1 - t if flip else t
                acc += ext_ref[rc * CONV_ROWS + 1 + t:rc * CONV_ROWS + 1 + t + CONV_ROWS, cols] * w_ref[wt:wt + 1, cols]
            out_ref[rc * CONV_ROWS:(rc + 1) * CONV_ROWS, cols] = acc


def _ln(v, g, b):
    mu = jnp.mean(v, axis=-1, keepdims=True)
    vc = v - mu
    var = jnp.mean(vc * vc, axis=-1, keepdims=True)
    return vc * lax.rsqrt(var + LN_EPS) * g + b


def _conv_fwd(name, u, w32, ln_g, ln_b):
    n_rows, width = u.shape
    tr = min(ROW_TILE, n_rows)
    n_steps = n_rows // tr

    def body(main_ref, prev_ref, next_ref, w_ref, g_ref, b_ref, u2_ref, cv_ref, ext_ref):
        _fill_ext(ext_ref, main_ref, prev_ref, next_ref, n_steps, tr)
        _conv_taps(ext_ref, w_ref, u2_ref, tr, width, flip=False)
        u3 = _ln(u2_ref[...], g_ref[...], b_ref[...])
        cv_ref[...] = (u3 * _sigmoid(u3)).astype(cv_ref.dtype)

    vec = lambda a: pl.BlockSpec(a.shape, lambda i: (0, 0))
    return pl.pallas_call(
        body, name=name, grid=(n_steps,),
        in_specs=_halo_specs(tr, width, n_rows) + [vec(w32), vec(ln_g), vec(ln_b)],
        out_specs=[pl.BlockSpec((tr, width), lambda i: (i, 0))] * 2,
        out_shape=[jax.ShapeDtypeStruct((n_rows, width), F32), jax.ShapeDtypeStruct((n_rows, width), BF16)],
        scratch_shapes=[pltpu.VMEM((tr + 2 * CONV_HALO, width), F32)],
        compiler_params=_params("arbitrary"),
    )(u, u, u, w32, ln_g, ln_b)


def _conv_bwd_a(name, u, u2, d_cv, ln_g, ln_b):
    n_rows, width = u.shape
    tr = min(ROW_TILE, n_rows)
    n_steps = n_rows // tr
    cw = min(CONV_LANES, width)

    def body(main_ref, prev_ref, next_ref, u2_ref, dcv_ref, g_ref, b_ref, du2_ref, dw_ref, dg_ref, db_ref, ext_ref):
        @pl.when(pl.program_id(0) == 0)
        def _():
            dw_ref[...] = jnp.zeros(dw_ref.shape, F32)
            dg_ref[...] = jnp.zeros(dg_ref.shape, F32)
            db_ref[...] = jnp.zeros(db_ref.shape, F32)

        _fill_ext(ext_ref, main_ref, prev_ref, next_ref, n_steps, tr)

        def swish_ln(v, g, b):
            u3 = _ln(v, g, b)
            return u3 * _sigmoid(u3)

        _, vjp = jax.vjp(swish_ln, u2_ref[...], g_ref[...], b_ref[...])
        du2, dg, db = vjp(dcv_ref[...])
        du2_ref[...] = du2
        dg_ref[...] += dg
        db_ref[...] += db
        for cc in range(width // cw):
            cols = slice(cc * cw, (cc + 1) * cw)
            for t in range(CONV_WIDTH):
                acc = jnp.zeros((CONV_ROWS, cw), F32)
                for rc in range(tr // CONV_ROWS):
                    r0 = rc * CONV_ROWS
                    acc += du2_ref[r0:r0 + CONV_ROWS, cols] * ext_ref[r0 + 1 + t:r0 + 1 + t + CONV_ROWS, cols]
                dw_ref[t:t + 1, cols] += jnp.sum(acc, axis=0, keepdims=True)

    vec = lambda a: pl.BlockSpec(a.shape, lambda i: (0, 0))
    row = pl.BlockSpec((tr, width), lambda i: (i, 0))
    return pl.pallas_call(
        body, name=name, grid=(n_steps,),
        in_specs=_halo_specs(tr, width, n_rows) + [row, row, vec(ln_g), vec(ln_b)],
        out_specs=[row, pl.BlockSpec((32, width), lambda i: (0, 0)), vec(ln_g), vec(ln_b)],
        out_shape=[jax.ShapeDtypeStruct((n_rows, width), F32), jax.ShapeDtypeStruct((32, width), F32),
                   jax.ShapeDtypeStruct(ln_g.shape, F32), jax.ShapeDtypeStruct(ln_b.shape, F32)],
        scratch_shapes=[pltpu.VMEM((tr + 2 * CONV_HALO, width), F32)],
        compiler_params=_params("arbitrary"),
    )(u, u, u, u2, d_cv, ln_g, ln_b)


def _conv_bwd_b(name, du2, z, off_a, off_b, w32):
    n_rows, width = du2.shape
    tr = min(ROW_TILE, n_rows)
    n_steps = n_rows // tr
    bw = math.gcd(math.gcd(off_a, off_b), width)
    npc = width // bw

    def body(*refs):
        main_ref, prev_ref, next_ref = refs[:3]
        a_refs, b_refs = refs[3:3 + npc], refs[3 + npc:3 + 2 * npc]
        w_ref, out_ref, ext_ref, du_ref = refs[3 + 2 * npc:]
        _fill_ext(ext_ref, main_ref, prev_ref, next_ref, n_steps, tr)
        _conv_taps(ext_ref, w_ref, du_ref, tr, width, flip=True)
        for p in range(npc):
            cols = slice(p * bw, (p + 1) * bw)
            du = du_ref[:, cols]
            sg = _sigmoid(b_refs[p][...])
            out_ref[:, p * bw:(p + 1) * bw] = (du * sg).astype(out_ref.dtype)
            out_ref[:, width + p * bw:width + (p + 1) * bw] = (du * a_refs[p][...] * sg * (1.0 - sg)).astype(out_ref.dtype)

    def piece(off, p):
        return pl.BlockSpec((tr, bw), functools.partial(lambda i, blk: (i, blk), blk=off // bw + p))

    in_specs = _halo_specs(tr, width, n_rows)
    in_specs += [piece(off_a, p) for p in range(npc)] + [piece(off_b, p) for p in range(npc)]
    in_specs.append(pl.BlockSpec(w32.shape, lambda i: (0, 0)))
    return pl.pallas_call(
        body, name=name, grid=(n_steps,), in_specs=in_specs,
        out_specs=pl.BlockSpec((tr, 2 * width), lambda i: (i, 0)),
        out_shape=jax.ShapeDtypeStruct((n_rows, 2 * width), BF16),
        scratch_shapes=[pltpu.VMEM((tr + 2 * CONV_HALO, width), F32), pltpu.VMEM((tr, width), F32)],
        compiler_params=_params("arbitrary"),
    )(du2, du2, du2, *([z] * (2 * npc)), w32)


def _place():
    return lax.axis_index("x"), lax.axis_index("y"), lax.axis_index("c")


def _flip(v, m):
    return 1 - v if m else v


def _gather_small(name, v, masks):
    varies = [any(m[a] for m in masks) for a in range(3)]
    n = len(masks) + 1

    def slot(pos):
        idx = 0
        for a in range(3):
            if varies[a]:
                idx = idx * 2 + pos[a]
        return idx

    def body(v_ref, o_ref, send_sems, recv_sems, local_sem):
        me = _place()
        mine = pltpu.make_async_copy(v_ref, o_ref.at[slot(me)], local_sem)
        mine.start()
        peers = [tuple(_flip(me[a], m[a]) for a in range(3)) for m in masks]
        sends = [pltpu.make_async_remote_copy(v_ref, o_ref.at[slot(me)], send_sems.at[k], recv_sems.at[k],
                                              device_id=peer, device_id_type=MESH) for k, peer in enumerate(peers)]
        for cp in sends:
            cp.start()
        for k, peer in enumerate(peers):
            pltpu.make_async_remote_copy(v_ref, o_ref.at[slot(peer)], send_sems.at[k], recv_sems.at[k],
                                         device_id=peer, device_id_type=MESH).wait_recv()
        for cp in sends:
            cp.wait_send()
        mine.wait()

    return pl.pallas_call(
        body, name=name, in_specs=[HBM_SPEC], out_specs=HBM_SPEC,
        out_shape=jax.ShapeDtypeStruct((n,) + v.shape, v.dtype),
        scratch_shapes=[pltpu.SemaphoreType.DMA((n - 1,)), pltpu.SemaphoreType.DMA((n - 1,)), pltpu.SemaphoreType.DMA(())],
    )(v)


ALL_DEVICES = [(mx, my, mc) for mx in (0, 1) for my in (0, 1) for mc in (0, 1)][1:]
SAME_CORE_CHIPS = [(1, 0, 0), (0, 1, 0), (1, 1, 0)]


def _chips(x, y):
    return [(1 - x, y), (x, 1 - y), (1 - x, 1 - y)]


def _ag_weights(name, shards, kinds):
    n = len(shards)
    shapes = [s.shape for s in shards]
    for r, _ in shapes:
        assert r % 32 == 0

    def window(o_ref, j, s, h):
        r, cc = shapes[j]
        hr = r // 2
        if kinds[j] == "col":
            return o_ref.at[pl.ds(pl.multiple_of(h * hr, 16), hr), pl.ds(pl.multiple_of(s * cc, LANE), cc)]
        return o_ref.at[pl.ds(pl.multiple_of(s * r + h * hr, 16), hr), :]

    def whole(o_ref, j, s):
        r, cc = shapes[j]
        if kinds[j] == "col":
            return o_ref.at[:, pl.ds(pl.multiple_of(s * cc, LANE), cc)]
        return o_ref.at[pl.ds(pl.multiple_of(s * r, 16), r), :]

    def body(*refs):
        ins, outs = refs[:n], refs[n:2 * n]
        send_sems, recv_sems, pass_send_sems, pass_recv_sems, local_sems = refs[2 * n:]
        x, y, c = _place()
        s_me = 2 * x + y
        chips = _chips(x, y)
        sibling = (x, y, 1 - c)
        local = [pltpu.make_async_copy(ins[j], whole(outs[j], j, s_me), local_sems.at[j]) for j in range(n)]
        for cp in local:
            cp.start()
        first, passed = [], []
        for j in range(n):
            hr = shapes[j][0] // 2
            src = ins[j].at[pl.ds(pl.multiple_of(c * hr, 16), hr), :]
            for k, chip in enumerate(chips):
                cp = pltpu.make_async_remote_copy(src, window(outs[j], j, s_me, c), send_sems.at[3 * j + k],
                                                  recv_sems.at[3 * j + k], device_id=(*chip, c), device_id_type=MESH)
                cp.start()
                first.append(cp)
        for j in range(n):
            for k, chip in enumerate(chips):
                win = window(outs[j], j, 2 * chip[0] + chip[1], c)
                pltpu.make_async_remote_copy(win, win, send_sems.at[3 * j + k], recv_sems.at[3 * j + k],
                                             device_id=(*chip, c), device_id_type=MESH).wait_recv()
                cp = pltpu.make_async_remote_copy(win, win, pass_send_sems.at[3 * j + k], pass_recv_sems.at[3 * j + k],
                                                  device_id=sibling, device_id_type=MESH)
                cp.start()
                passed.append(cp)
        for j in range(n):
            for k, chip in enumerate(chips):
                win = window(outs[j], j, 2 * chip[0] + chip[1], 1 - c)
                pltpu.make_async_remote_copy(win, win, pass_send_sems.at[3 * j + k], pass_recv_sems.at[3 * j + k],
                                             device_id=sibling, device_id_type=MESH).wait_recv()
        for cp in first + passed:
            cp.wait_send()
        for cp in local:
            cp.wait()

    out_shape = [jax.ShapeDtypeStruct((r, 4 * cc) if kinds[j] == "col" else (4 * r, cc), shards[j].dtype)
                 for j, (r, cc) in enumerate(shapes)]
    sems = [pltpu.SemaphoreType.DMA((3 * n,)) for _ in range(4)] + [pltpu.SemaphoreType.DMA((n,))]
    return pl.pallas_call(
        body, name=name, in_specs=[HBM_SPEC] * n, out_specs=[HBM_SPEC] * n, out_shape=out_shape, scratch_shapes=sems,
    )(*shards)


def _rs_pair(name, grads):
    n = len(grads)

    def body(*refs):
        ins, outs = refs[:n], refs[n:2 * n]
        send_sems, recv_sems = refs[2 * n:]
        x, y, c = _place()
        cps = [pltpu.make_async_remote_copy(ins[j].at[1], outs[j], send_sems.at[j], recv_sems.at[j],
                                            device_id=(x, y, 1 - c), device_id_type=MESH) for j in range(n)]
        for cp in cps:
            cp.start()
        for cp in cps:
            cp.wait()

    return pl.pallas_call(
        body, name=name, in_specs=[HBM_SPEC] * n, out_specs=[HBM_SPEC] * n,
        out_shape=[jax.ShapeDtypeStruct(g.shape[1:], g.dtype) for g in grads],
        scratch_shapes=[pltpu.SemaphoreType.DMA((n,)), pltpu.SemaphoreType.DMA((n,))],
    )(*grads)


def _rs_chips(name, halves, kinds):
    n = len(halves)
    shapes = [(h.shape[0], h.shape[1] // 4) if kinds[j] == "col" else (h.shape[0] // 4, h.shape[1])
              for j, h in enumerate(halves)]

    def part(ref, j, s):
        r, cc = shapes[j]
        if kinds[j] == "col":
            return ref.at[:, pl.ds(pl.multiple_of(s * cc, LANE), cc)]
        return ref.at[pl.ds(pl.multiple_of(s * r, 16), r), :]

    def body(*refs):
        ins, outs = refs[:n], refs[n:2 * n]
        send_sems, recv_sems, local_sems = refs[2 * n:]
        x, y, c = _place()
        s_me = 2 * x + y
        chips = _chips(x, y)
        local = [pltpu.make_async_copy(part(ins[j], j, s_me), outs[j].at[s_me], local_sems.at[j]) for j in range(n)]
        for cp in local:
            cp.start()
        sends = []
        for j in range(n):
            for k, chip in enumerate(chips):
                cp = pltpu.make_async_remote_copy(part(ins[j], j, 2 * chip[0] + chip[1]), outs[j].at[s_me],
                                                  send_sems.at[3 * j + k], recv_sems.at[3 * j + k],
                                                  device_id=(*chip, c), device_id_type=MESH)
                cp.start()
                sends.append(cp)
        for j in range(n):
            for k, chip in enumerate(chips):
                dst = outs[j].at[2 * chip[0] + chip[1]]
                pltpu.make_async_remote_copy(dst, dst, send_sems.at[3 * j + k], recv_sems.at[3 * j + k],
                                             device_id=(*chip, c), device_id_type=MESH).wait_recv()
        for cp in sends:
            cp.wait_send()
        for cp in local:
            cp.wait()

    return pl.pallas_call(
        body, name=name, in_specs=[HBM_SPEC] * n, out_specs=[HBM_SPEC] * n,
        out_shape=[jax.ShapeDtypeStruct((4,) + shapes[j], halves[j].dtype) for j in range(n)],
        scratch_shapes=[pltpu.SemaphoreType.DMA((3 * n,)), pltpu.SemaphoreType.DMA((3 * n,)), pltpu.SemaphoreType.DMA((n,))],
    )(*halves)


def _rs_join(name, halves, kinds):
    n = len(halves)

    def window(ref, j, h):
        r, cc = halves[j].shape
        if kinds[j] == "col":
            return ref.at[pl.ds(pl.multiple_of(h * r, 8), r), :]
        return ref.at[:, pl.ds(pl.multiple_of(h * cc, LANE), cc)]

    def body(*refs):
        ins, outs = refs[:n], refs[n:2 * n]
        send_sems, recv_sems, local_sems = refs[2 * n:]
        x, y, c = _place()
        local = [pltpu.make_async_copy(ins[j], window(outs[j], j, c), local_sems.at[j]) for j in range(n)]
        sends = [pltpu.make_async_remote_copy(ins[j], window(outs[j], j, c), send_sems.at[j], recv_sems.at[j],
                                              device_id=(x, y, 1 - c), device_id_type=MESH) for j in range(n)]
        for cp in local + sends:
            cp.start()
        for j in range(n):
            dst = window(outs[j], j, 1 - c)
            pltpu.make_async_remote_copy(ins[j], dst, send_sems.at[j], recv_sems.at[j],
                                         device_id=(x, y, 1 - c), device_id_type=MESH).wait_recv()
        for cp in sends:
            cp.wait_send()
        for cp in local:
            cp.wait()

    out_shape = [jax.ShapeDtypeStruct((2 * h.shape[0], h.shape[1]) if kinds[j] == "col" else (h.shape[0], 2 * h.shape[1]),
                                      h.dtype) for j, h in enumerate(halves)]
    return pl.pallas_call(
        body, name=name, in_specs=[HBM_SPEC] * n, out_specs=[HBM_SPEC] * n, out_shape=out_shape,
        scratch_shapes=[pltpu.SemaphoreType.DMA((n,)), pltpu.SemaphoreType.DMA((n,)), pltpu.SemaphoreType.DMA((n,))],
    )(*halves)


def _pair_sum(name, mine_other, got):
    _, r, cc = mine_other.shape
    tr = _rows_within(r, 3 * cc * mine_other.dtype.itemsize, (256, 128, 64, 32, 16))

    def body(a_ref, b_ref, o_ref):
        o_ref[...] = (a_ref[...].astype(F32) + b_ref[...].astype(F32)).astype(o_ref.dtype)

    return pl.pallas_call(
        body, name=name, grid=(r // tr,),
        in_specs=[pl.BlockSpec((None, tr, cc), lambda i: (0, i, 0)), pl.BlockSpec((tr, cc), lambda i: (i, 0))],
        out_specs=pl.BlockSpec((tr, cc), lambda i: (i, 0)),
        out_shape=jax.ShapeDtypeStruct((r, cc), mine_other.dtype), compiler_params=_params("parallel"),
    )(mine_other, got)


def _sum_slots(name, parts):
    n, r, cc = parts.shape
    tr = _rows_within(r, cc * (n * parts.dtype.itemsize + 4), (256, 128, 64, 32, 16, 8))

    def body(p_ref, o_ref):
        acc = p_ref[0].astype(F32)
        for s in range(1, n):
            acc = acc + p_ref[s].astype(F32)
        o_ref[...] = acc

    return pl.pallas_call(
        body, name=name, grid=(r // tr,),
        in_specs=[pl.BlockSpec((n, tr, cc), lambda i: (0, i, 0))],
        out_specs=pl.BlockSpec((tr, cc), lambda i: (i, 0)),
        out_shape=jax.ShapeDtypeStruct((r, cc), F32), compiler_params=_params("parallel"),
    )(parts)


def _adamw_math(w, g, m, v):
    m = ADAM_B1 * m + (1.0 - ADAM_B1) * g
    v = ADAM_B2 * v + (1.0 - ADAM_B2) * jnp.square(g)
    m_hat = m / (1.0 - ADAM_B1 ** ADAM_STEP)
    v_hat = v / (1.0 - ADAM_B2 ** ADAM_STEP)
    delta = -ADAM_LR * (m_hat / (jnp.sqrt(v_hat) + ADAM_EPS) + ADAM_WD * w)
    return delta, m, v


def _adamw_layer(name, layer, g, w, m, v, prev):
    n_layers, r, cc = w.shape
    tr = _rows_within(r, 8 * cc * 4)
    if prev is None:
        prev = tuple(lax.empty(w.shape, F32) for _ in range(4))
    blk = pl.BlockSpec((None, tr, cc), lambda i: (layer, i, 0))

    def body(g_ref, w_ref, m_ref, v_ref, *rest):
        og_ref, od_ref, om_ref, ov_ref = rest[4:]
        gv = g_ref[...]
        delta, m2, v2 = _adamw_math(w_ref[...], gv, m_ref[...], v_ref[...])
        og_ref[...] = gv
        od_ref[...] = delta
        om_ref[...] = m2
        ov_ref[...] = v2

    return pl.pallas_call(
        body, name=name, grid=(r // tr,),
        in_specs=[pl.BlockSpec((tr, cc), lambda i: (i, 0)), blk, blk, blk] + [HBM_SPEC] * 4,
        out_specs=[blk] * 4, out_shape=[jax.ShapeDtypeStruct(w.shape, F32)] * 4,
        input_output_aliases={4: 0, 5: 1, 6: 2, 7: 3}, compiler_params=_params("parallel"),
    )(g, w, m, v, *prev)


def _adamw_small(name, g, w, m, v):
    def body(g_ref, w_ref, m_ref, v_ref, od_ref, om_ref, ov_ref):
        delta, m2, v2 = _adamw_math(w_ref[...], g_ref[...], m_ref[...], v_ref[...])
        od_ref[...] = delta
        om_ref[...] = m2
        ov_ref[...] = v2

    return pl.pallas_call(body, name=name, out_shape=[jax.ShapeDtypeStruct(w.shape, F32)] * 3)(g, w, m, v)


def _adamw_ada(name, c16, dmod16, w, m, v):
    n_layers, d, cols = w.shape
    tr = _rows_within(d, 7 * cols * 4, (256, 128))
    blk = pl.BlockSpec((None, tr, cols), lambda l, i: (l, i, 0))

    def body(c_ref, dm_ref, w_ref, m_ref, v_ref, og_ref, od_ref, om_ref, ov_ref):
        gv = lax.dot_general(c_ref[...], dm_ref[...], (((0,), (0,)), ((), ())), preferred_element_type=F32)
        delta, m2, v2 = _adamw_math(w_ref[...], gv, m_ref[...], v_ref[...])
        og_ref[...] = gv
        od_ref[...] = delta
        om_ref[...] = m2
        ov_ref[...] = v2

    return pl.pallas_call(
        body, name=name, grid=(n_layers, d // tr),
        in_specs=[pl.BlockSpec((16, tr), lambda l, i: (0, i)), pl.BlockSpec((None, 16, cols), lambda l, i: (l, 0, 0)),
                  blk, blk, blk],
        out_specs=[blk] * 4, out_shape=[jax.ShapeDtypeStruct(w.shape, F32)] * 4,
        compiler_params=_params("parallel", "parallel"),
    )(c16, dmod16, w, m, v)


def kernel(x, c, w_ada, b_ada, w_in, sink, w_dw, conv_ln_g, conv_ln_b, w_oa, w_ob, w_out, ln1_g, ln1_b, w_gu, w_down, ln2_g, ln2_b, loss_target, m_w_ada, m_b_ada, m_w_in, m_sink, m_w_dw, m_conv_ln_g, m_conv_ln_b, m_w_oa, m_w_ob, m_w_out, m_ln1_g, m_ln1_b, m_w_gu, m_w_down, m_ln2_g, m_ln2_b, v_w_ada, v_b_ada, v_w_in, v_sink, v_w_dw, v_conv_ln_g, v_conv_ln_b, v_w_oa, v_w_ob, v_w_out, v_ln1_g, v_ln1_b, v_w_gu, v_w_down, v_ln2_g, v_ln2_b):
    seq, d = x.shape[1], x.shape[2]
    n_layers = w_in.shape[0]
    d_in = 4 * w_in.shape[2]
    d_ff = 4 * w_down.shape[1]
    hq = d // HEAD_DIM
    dkv = (hq // GQA_GROUP) * HEAD_DIM
    off_k, off_v, off_ga, off_gb = d, d + dkv, d + 2 * dkv, 2 * d + 2 * dkv
    off_gta, off_gtb = 3 * d + 2 * dkv, 4 * d + 2 * dkv
    assert d_in == 5 * d + 2 * dkv and seq % ROW_TILE == 0 and seq >= BAND
    alpha = (2.0 * n_layers) ** 0.25

    xi, yi, ci = _place()
    chip = 2 * xi + yi
    batch = 4 * xi + 2 * yi + ci
    c_idx = jnp.reshape(ci, (1,)).astype(jnp.int32)
    x2 = x[0]
    target = loss_target[0]

    c_act = jax.nn.silu(c)
    c_all = _gather_small("gather_c", c_act, ALL_DEVICES).reshape(8, d)
    c16 = jnp.concatenate([c_all, jnp.zeros((8, d), F32)], axis=0).astype(BF16)
    mod_cols = [_mm(f"mod_{l}", c16, w_ada, "nn", F32, b_layer=l) for l in range(n_layers)]
    mod_all = _gather_small("gather_mod", jnp.stack(mod_cols), SAME_CORE_CHIPS)
    mod = lax.dynamic_index_in_dim(mod_all, batch, axis=2, keepdims=False)
    mod = jnp.transpose(mod, (1, 0, 2)).reshape(n_layers, N_MOD * d) + b_ada
    mod = mod.reshape(n_layers, N_MOD, 1, d)
    sh_a, sc_a, gt_a, sh_f, sc_f, gt_f = (mod[:, j] for j in range(N_MOD))

    pos = jnp.arange(seq, dtype=F32)
    inv_freq = ROPE_THETA ** (-jnp.arange(0, ROPE_DIM, 2, dtype=F32) / ROPE_DIM)
    ang = pos[:, None] * inv_freq[None, :]
    cos, sin = jnp.cos(ang), jnp.sin(ang)
    half = ROPE_DIM // 2
    rest = HEAD_DIM - ROPE_DIM
    t_cs = jnp.concatenate([cos, cos, jnp.ones((seq, rest), F32)], axis=1)
    t_up = jnp.concatenate([-sin, jnp.zeros((seq, rest + half), F32)], axis=1)
    t_dn = jnp.concatenate([jnp.zeros((seq, half), F32), sin, jnp.zeros((seq, rest), F32)], axis=1)

    def rope(t, cs, up, dn):
        w = t.shape[1]
        reps = (1, w // HEAD_DIM)
        return (t * jnp.tile(cs, reps) + pltpu.roll(t, w - half, 1) * jnp.tile(up, reps)
                + pltpu.roll(t, half, 1) * jnp.tile(dn, reps))

    def rope_t(dt, cs, up, dn):
        w = dt.shape[1]
        reps = (1, w // HEAD_DIM)
        return (dt * jnp.tile(cs, reps) + pltpu.roll(dt * jnp.tile(up, reps), half, 1)
                + pltpu.roll(dt * jnp.tile(dn, reps), w - half, 1))

    tables = [(t_cs, 0, HEAD_DIM), (t_up, 0, HEAD_DIM), (t_dn, 0, HEAD_DIM)]

    kinds = ("col", "col", "row", "row", "row", "row")
    gathered = []
    for l in range(n_layers):
        shards = [w[l].astype(BF16) for w in (w_in, w_gu, w_oa, w_ob, w_out, w_down)]
        gathered.append(_ag_weights(f"gather_w_{l}", shards, kinds))
    w_dw_all = _gather_small("gather_dw", w_dw, SAME_CORE_CHIPS)
    w_dw_full = jnp.transpose(w_dw_all, (1, 2, 0, 3)).reshape(n_layers, CONV_WIDTH, d)
    w_dw32 = jnp.pad(w_dw_full, ((0, 0), (0, 32 - CONV_WIDTH), (0, 0)))
    sink_b = jnp.broadcast_to(sink[:, :, None], (n_layers, hq, LANE))

    def vec(a, l):
        return a[l][None, :]

    def res_ln(xprev, y, gt, g, b, scn, shn):
        xn = _ln(alpha * xprev + (1.0 + gt) * y, g, b)
        return xn, xn * (1.0 + scn) + shn

    def merge(ya, yb, ga, gb):
        return _sigmoid(ga) * ya + _sigmoid(gb) * yb

    def swiglu(gate, up):
        return gate * _sigmoid(gate) * up

    h = _rowwise("modulate_in", lambda xv, sc, sh: xv * (1.0 + sc) + sh, seq, [(x2, 0, d)], [sc_a[0], sh_a[0]],
                 [(d, BF16)])
    xprev = x2
    saved = []
    for l in range(n_layers):
        wi, wg, woa, wob, wout, wdn = gathered[l]
        z = _mm(f"in_proj_{l}", h, wi, "nn", F32)
        qr, kr, vb = _rowwise(
            f"qkv_prep_{l}", lambda q, k, v, cs, up, dn: (rope(q, cs, up, dn), rope(k, cs, up, dn), v), seq,
            [(z, 0, d), (z, off_k, dkv), (z, off_v, dkv)] + tables, [], [(d, BF16), (dkv, BF16), (dkv, BF16)])
        att = _attn_fwd(f"attn_{l}", qr, kr, vb, sink_b[l])
        y_a = _mm(f"attn_out_{l}", att, woa, "nn", F32)
        u = _rowwise(f"glu_{l}", lambda a, b: a * _sigmoid(b), seq, [(z, off_ga, d), (z, off_gb, d)], [], [(d, F32)])
        u2, cv = _conv_fwd(f"conv_{l}", u, w_dw32[l], vec(conv_ln_g, l), vec(conv_ln_b, l))
        y_b = _mm(f"conv_out_{l}", cv, wob, "nn", F32)
        mg = _rowwise(f"merge_{l}", merge, seq, [(y_a, 0, d), (y_b, 0, d), (z, off_gta, d), (z, off_gtb, d)], [],
                      [(d, BF16)])
        o = _mm(f"mix_out_{l}", mg, wout, "nn", F32)
        x1, h2 = _rowwise(f"res_ln1_{l}", res_ln, seq, [(xprev, 0, d), (o, 0, d)],
                          [gt_a[l], vec(ln1_g, l), vec(ln1_b, l), sc_f[l], sh_f[l]], [(d, F32), (d, BF16)])
        gu = _mm(f"ffn_up_{l}", h2, wg, "nn", F32)
        f = _rowwise(f"swiglu_{l}", swiglu, seq, [(gu, 0, d_ff), (gu, d_ff, d_ff)], [], [(d_ff, BF16)])
        ffn = _mm(f"ffn_down_{l}", f, wdn, "nn", F32)
        saved.append(dict(xprev=xprev, h=h, z=z, qr=qr, kr=kr, vb=vb, att=att, u=u, u2=u2, cv=cv, y_a=y_a, y_b=y_b,
                          mg=mg, o=o, x1=x1, h2=h2, gu=gu, f=f, ffn=ffn))
        if l + 1 < n_layers:
            xprev, h = _rowwise(f"res_ln2_{l}", res_ln, seq, [(x1, 0, d), (ffn, 0, d)],
                                [gt_f[l], vec(ln2_g, l), vec(ln2_b, l), sc_a[l + 1], sh_a[l + 1]], [(d, F32), (d, BF16)])

    def res_ln_bwd(xp, y, dxn, dh, gt, g, b, scn, shn):
        _, vjp = jax.vjp(res_ln, xp, y, gt, g, b, scn, shn)
        return vjp((dxn, dh))

    def last_ln_bwd(xp, y, tgt, gt, g, b):
        def head(xp_, y_, gt_, g_, b_):
            return _ln(alpha * xp_ + (1.0 + gt_) * y_, g_, b_)
        out, vjp = jax.vjp(head, xp, y, gt, g, b)
        err = out - tgt
        loss = 0.5 * jnp.sum(jnp.sum(err * err, axis=-1, keepdims=True) / d, axis=0, keepdims=True)
        return vjp(err / d) + (jnp.broadcast_to(loss, (1, LANE)),)

    def merge_bwd(dmg, ya, yb, ga, gb):
        _, vjp = jax.vjp(merge, ya, yb, ga, gb)
        dya, dyb, dga, dgb = vjp(dmg)
        return dya, dyb, jnp.concatenate([dga, dgb], axis=1)

    def swiglu_bwd(df, gate, up):
        _, vjp = jax.vjp(swiglu, gate, up)
        return jnp.concatenate(vjp(df), axis=1)

    vec_d = ((1, d), F32)
    small = [None] * n_layers
    big = None
    loss_part = None
    dxn = dh = None
    for l in reversed(range(n_layers)):
        sv = saved[l]
        wi, wg, woa, wob, wout, wdn = gathered[l]
        ln2 = [gt_f[l], vec(ln2_g, l), vec(ln2_b, l)]
        if l + 1 == n_layers:
            dx1, dffn, d_gtf, d_g2, d_b2, loss_part = _rowwise(
                "last_ln_bwd", last_ln_bwd, seq, [(sv["x1"], 0, d), (sv["ffn"], 0, d), (target, 0, d)], ln2,
                [(d, F32), (d, BF16)], [vec_d, vec_d, vec_d, ((1, LANE), F32)])
            d_sca_next = d_sha_next = None
        else:
            dx1, dffn, d_gtf, d_g2, d_b2, d_sca_next, d_sha_next = _rowwise(
                f"res_ln2_bwd_{l}", res_ln_bwd, seq, [(sv["x1"], 0, d), (sv["ffn"], 0, d), (dxn, 0, d), (dh, 0, d)],
                ln2 + [sc_a[l + 1], sh_a[l + 1]], [(d, F32), (d, BF16)], [vec_d] * 5)
            small[l + 1]["sc_a"], small[l + 1]["sh_a"] = d_sca_next, d_sha_next
        df = _mm(f"ffn_down_dx_{l}", dffn, wdn, "nt", F32)
        g_down = _mm(f"ffn_down_dw_{l}", sv["f"], dffn, "tn", BF16, split="cols", c_idx=c_idx)
        dgu = _rowwise(f"swiglu_bwd_{l}", swiglu_bwd, seq, [(df, 0, d_ff), (sv["gu"], 0, d_ff), (sv["gu"], d_ff, d_ff)],
                       [], [(2 * d_ff, BF16)])
        dh2 = _mm(f"ffn_up_dx_{l}", dgu, wg, "nt", F32)
        g_gu = _mm(f"ffn_up_dw_{l}", sv["h2"], dgu, "tn", BF16, split="rows", c_idx=c_idx)
        dxp, d_o, d_gta, d_g1, d_b1, d_scf, d_shf = _rowwise(
            f"res_ln1_bwd_{l}", res_ln_bwd, seq, [(sv["xprev"], 0, d), (sv["o"], 0, d), (dx1, 0, d), (dh2, 0, d)],
            [gt_a[l], vec(ln1_g, l), vec(ln1_b, l), sc_f[l], sh_f[l]], [(d, F32), (d, BF16)], [vec_d] * 5)
        dmg = _mm(f"mix_out_dx_{l}", d_o, wout, "nt", F32)
        g_out = _mm(f"mix_out_dw_{l}", sv["mg"], d_o, "tn", BF16, split="cols", c_idx=c_idx)
        z = sv["z"]
        dya, dyb, d_gates = _rowwise(
            f"merge_bwd_{l}", merge_bwd, seq,
            [(dmg, 0, d), (sv["y_a"], 0, d), (sv["y_b"], 0, d), (z, off_gta, d), (z, off_gtb, d)], [],
            [(d, BF16), (d, BF16), (2 * d, BF16)])
        d_att = _mm(f"attn_out_dx_{l}", dya, woa, "nt", BF16)
        g_oa = _mm(f"attn_out_dw_{l}", sv["att"], dya, "tn", BF16, split="cols", c_idx=c_idx)
        d_cv = _mm(f"conv_out_dx_{l}", dyb, wob, "nt", F32)
        g_ob = _mm(f"conv_out_dw_{l}", sv["cv"], dyb, "tn", BF16, split="cols", c_idx=c_idx)
        du2, d_wdw, d_cg, d_cb = _conv_bwd_a(f"conv_bwd_a_{l}", sv["u"], sv["u2"], d_cv, vec(conv_ln_g, l),
                                             vec(conv_ln_b, l))
        d_glu = _conv_bwd_b(f"conv_bwd_b_{l}", du2, z, off_ga, off_gb, w_dw32[l])
        dqr, dkr, dvb, d_sink = _attn_bwd(f"attn_bwd_{l}", sv["qr"], sv["kr"], sv["vb"], sink_b[l], d_att)
        d_qkv = _rowwise(
            f"qkv_bwd_{l}",
            lambda dq_, dk_, dv_, cs, up, dn: jnp.concatenate([rope_t(dq_, cs, up, dn), rope_t(dk_, cs, up, dn), dv_], axis=1),
            seq, [(dqr, 0, d), (dkr, 0, dkv), (dvb, 0, dkv)] + tables, [], [(d + 2 * dkv, BF16)])
        dz = jnp.concatenate([d_qkv, d_glu, d_gates], axis=1)
        dh = _mm(f"in_proj_dx_{l}", dz, wi, "nt", F32)
        g_in = _mm(f"in_proj_dw_{l}", sv["h"], dz, "tn", BF16, split="rows", c_idx=c_idx)
        dxn = dxp
        small[l] = dict(gt_a=d_gta, sh_f=d_shf, sc_f=d_scf, gt_f=d_gtf, ln1_g=d_g1, ln1_b=d_b1, ln2_g=d_g2, ln2_b=d_b2,
                        conv_ln_g=d_cg, conv_ln_b=d_cb, sink=d_sink[:, :1].reshape(1, hq), w_dw=d_wdw[:CONV_WIDTH])

        grads = [g_in, g_gu, g_oa, g_ob, g_out, g_down]
        got = _rs_pair(f"rs_pair_{l}", grads)
        halves = [_pair_sum(f"pair_sum_{l}_{j}", grads[j], got[j]) for j in range(6)]
        parts = _rs_chips(f"rs_chips_{l}", halves, kinds)
        reduced = [_sum_slots(f"sum_chips_{l}_{j}", parts[j]) for j in range(6)]
        full = _rs_join(f"rs_join_{l}", reduced, kinds)
        stacks = ((w_in, m_w_in, v_w_in), (w_gu, m_w_gu, v_w_gu), (w_oa, m_w_oa, v_w_oa), (w_ob, m_w_ob, v_w_ob),
                  (w_out, m_w_out, v_w_out), (w_down, m_w_down, v_w_down))
        big = [_adamw_layer(f"adamw_{l}_{j}", l, full[j], *stacks[j], None if big is None else big[j]) for j in range(6)]

    grad_x, d_sca0, d_sha0 = _rowwise(
        "modulate_in_bwd", lambda xv, dhv, dxv, sc: (dxv + dhv * (1.0 + sc), jnp.sum(dhv * xv, axis=0, keepdims=True),
                                                     jnp.sum(dhv, axis=0, keepdims=True)),
        seq, [(x2, 0, d), (dh, 0, d), (dxn, 0, d)], [sc_a[0]], [(d, F32)], [vec_d, vec_d])
    small[0]["sc_a"], small[0]["sh_a"] = d_sca0, d_sha0

    order = ("sh_a", "sc_a", "gt_a", "sh_f", "sc_f", "gt_f", "conv_ln_g", "conv_ln_b", "ln1_g", "ln1_b", "ln2_g", "ln2_b")
    rows = []
    for l in range(n_layers):
        rows += [small[l][k] for k in order]
        rows.append(jnp.pad(small[l]["sink"], ((0, 0), (0, d - hq))))
        rows.append(small[l]["w_dw"])
    rows.append(jnp.pad(loss_part, ((0, 0), (0, d - LANE))))
    n_small = sum(r.shape[0] for r in rows)
    pad_rows = (-n_small) % 8
    packed = jnp.concatenate(rows + [jnp.zeros((pad_rows, d), F32)], axis=0)
    everyone = _gather_small("gather_small_grads", packed, ALL_DEVICES)
    total = _sum_slots("sum_small_grads", everyone)
    per_layer = len(order) + 1 + CONV_WIDTH
    tot = total[:n_layers * per_layer].reshape(n_layers, per_layer, d)
    g_mod = tot[:, :N_MOD].reshape(n_layers, N_MOD * d)
    g_small = {k: tot[:, N_MOD + j] for j, k in enumerate(order[N_MOD:])}
    g_sink = tot[:, len(order), :hq]
    g_dw_full = tot[:, len(order) + 1:]
    cols_dw = w_dw.shape[2]
    g_dw = lax.dynamic_slice_in_dim(g_dw_full, chip * cols_dw, cols_dw, axis=2)
    loss = total[n_layers * per_layer, 0]

    d_mod_all = everyone[:, :n_layers * per_layer].reshape(8, n_layers, per_layer, d)[:, :, :N_MOD]
    d_mod_all = d_mod_all.reshape(8, n_layers, N_MOD * d)
    cols_ada = w_ada.shape[2]
    d_mod_mine = lax.dynamic_slice_in_dim(d_mod_all, chip * cols_ada, cols_ada, axis=2)
    dmod16 = jnp.concatenate([d_mod_mine, jnp.zeros_like(d_mod_mine)], axis=0)
    dmod16 = jnp.transpose(dmod16, (1, 0, 2)).astype(BF16)
    ada = _adamw_ada("adamw_ada", c16, dmod16, w_ada, m_w_ada, v_w_ada)

    def small_step(name, g, w, m, v):
        shp = w.shape
        g2, w2, m2, v2 = (a.reshape(-1, shp[-1]) for a in (g, w, m, v))
        return (g,) + tuple(a.reshape(shp) for a in _adamw_small(name, g2, w2, m2, v2))

    res = {
        "w_ada": ada,
        "b_ada": small_step("adamw_b_ada", g_mod, b_ada, m_b_ada, v_b_ada),
        "sink": small_step("adamw_sink", g_sink, sink, m_sink, v_sink),
        "w_dw": small_step("adamw_w_dw", g_dw, w_dw, m_w_dw, v_w_dw),
        "conv_ln_g": small_step("adamw_conv_ln_g", g_small["conv_ln_g"], conv_ln_g, m_conv_ln_g, v_conv_ln_g),
        "conv_ln_b": small_step("adamw_conv_ln_b", g_small["conv_ln_b"], conv_ln_b, m_conv_ln_b, v_conv_ln_b),
        "ln1_g": small_step("adamw_ln1_g", g_small["ln1_g"], ln1_g, m_ln1_g, v_ln1_g),
        "ln1_b": small_step("adamw_ln1_b", g_small["ln1_b"], ln1_b, m_ln1_b, v_ln1_b),
        "ln2_g": small_step("adamw_ln2_g", g_small["ln2_g"], ln2_g, m_ln2_g, v_ln2_g),
        "ln2_b": small_step("adamw_ln2_b", g_small["ln2_b"], ln2_b, m_ln2_b, v_ln2_b),
        "w_in": big[0], "w_gu": big[1], "w_oa": big[2], "w_ob": big[3], "w_out": big[4], "w_down": big[5],
    }
    names = ("w_ada", "b_ada", "w_in", "sink", "w_dw", "conv_ln_g", "conv_ln_b", "w_oa", "w_ob", "w_out", "ln1_g", "ln1_b",
             "w_gu", "w_down", "ln2_g", "ln2_b")
    outs = [loss, grad_x[None]]
    for field in range(4):
        outs += [res[k][field] for k in names]
    return tuple(outs)
```

```python
import functools
import math

import jax
import jax.numpy as jnp
from jax import lax
from jax.experimental import pallas as pl
from jax.experimental.pallas import tpu as pltpu

F32 = jnp.float32
BF16 = jnp.bfloat16
MESH = pl.DeviceIdType.MESH

HEAD_DIM = 128
GQA_GROUP = 4
WINDOW = 128
BLOCK = 128
BAND = 3 * BLOCK
ROPE_DIM = HEAD_DIM // 4
ROPE_THETA = 500000.0
CONV_WIDTH = 31
CONV_PAD = CONV_WIDTH // 2
CONV_HALO = 16
N_MOD = 6
LN_EPS = 1e-5
NEG_INF = -1e30
ADAM_LR = 0.001
ADAM_B1 = 0.9
ADAM_B2 = 0.999
ADAM_EPS = 1e-08
ADAM_WD = 0.01
ADAM_STEP = 10

LANE = 128
V7X_VMEM_LIMIT = 56 * 1024 * 1024
ROW_TILE = 256
CONV_ROWS = 32
CONV_LANES = 256

HBM_SPEC = pl.BlockSpec(memory_space=pltpu.HBM)


def _params(*sem):
    return pltpu.CompilerParams(dimension_semantics=sem, vmem_limit_bytes=V7X_VMEM_LIMIT)


def _pick(n, cands, even=False):
    for t in cands:
        if n % t == 0 and (not even or (n // t) % 2 == 0):
            return t
    raise ValueError(f"no tile for {n} in {cands}")


BLOCK_BUDGET = 10 * 1024 * 1024


def _rows_within(n_rows, bytes_per_row, cands=(256, 128, 64, 32, 16, 8)):
    fit = [t for t in cands if n_rows % t == 0]
    for t in fit:
        if t * bytes_per_row <= BLOCK_BUDGET:
            return t
    return fit[-1]


def _sigmoid(v):
    return jax.nn.sigmoid(v)


def _const_map(ndim):
    return lambda *_: (0,) * ndim


def _rowwise(name, fn, n_rows, row_ins, vec_ins, row_outs, vec_outs=()):
    per_row = sum(w * a.dtype.itemsize for a, _, w in row_ins) + sum(w * jnp.dtype(dt).itemsize for w, dt in row_outs)
    tr = _rows_within(n_rows, per_row, (ROW_TILE, 128, 64))
    in_specs, args, pieces = [], [], []
    for arr, off, width in row_ins:
        bw = math.gcd(off, width) if off else width
        assert bw % LANE == 0 and arr.shape[0] == n_rows
        pieces.append(width // bw)
        for p in range(width // bw):
            in_specs.append(pl.BlockSpec((tr, bw), functools.partial(lambda i, blk: (i, blk), blk=off // bw + p)))
            args.append(arr)
    for v in vec_ins:
        in_specs.append(pl.BlockSpec(v.shape, _const_map(v.ndim)))
        args.append(v)
    out_shape = [jax.ShapeDtypeStruct((n_rows, w), dt) for w, dt in row_outs]
    out_specs = [pl.BlockSpec((tr, w), lambda i: (i, 0)) for w, _ in row_outs]
    for shp, dt in vec_outs:
        out_shape.append(jax.ShapeDtypeStruct(shp, dt))
        out_specs.append(pl.BlockSpec(shp, _const_map(len(shp))))
    n_in, n_row_out = len(args), len(row_outs)

    def body(*refs):
        in_refs, out_refs = refs[:n_in], refs[n_in:]
        vals, k = [], 0
        for npc in pieces:
            ps = [in_refs[k + p][...] for p in range(npc)]
            k += npc
            vals.append(ps[0] if npc == 1 else jnp.concatenate(ps, axis=1))
        for _ in vec_ins:
            vals.append(in_refs[k][...])
            k += 1
        outs = fn(*vals)
        if not isinstance(outs, (tuple, list)):
            outs = (outs,)
        assert len(outs) == len(out_refs)
        for j in range(n_row_out):
            out_refs[j][...] = outs[j].astype(out_refs[j].dtype)
        if vec_outs:
            @pl.when(pl.program_id(0) == 0)
            def _():
                for j in range(n_row_out, len(out_refs)):
                    out_refs[j][...] = jnp.zeros(out_refs[j].shape, out_refs[j].dtype)
            for j in range(n_row_out, len(out_refs)):
                out_refs[j][...] += outs[j].astype(out_refs[j].dtype)

    res = pl.pallas_call(
        body, name=name, grid=(n_rows // tr,), in_specs=in_specs, out_specs=out_specs, out_shape=out_shape,
        compiler_params=_params("arbitrary"),
    )(*args)
    return res[0] if len(res) == 1 else res


def _mm(name, a, b, mode, out_dtype, b_layer=None, split=None, c_idx=None):
    bshape = b.shape[1:] if b_layer is not None else b.shape
    if mode == "nn":
        (m, k), (k2, n) = a.shape, bshape
        dims = (((1,), (0,)), ((), ()))
    elif mode == "nt":
        (m, k), (n, k2) = a.shape, bshape
        dims = (((1,), (1,)), ((), ()))
    else:
        (k, m), (k2, n) = a.shape, bshape
        dims = (((0,), (0,)), ((), ()))
    assert k == k2, (name, a.shape, b.shape)
    tm = _pick(m, (1024, 512, 256, 128, 16), even=(split == "rows"))
    tn = _pick(n, (1024, 512, 256, 128), even=(split == "cols"))
    tk = _pick(k, (2048, 1408, 1024, 704, 512, 256, 128))
    ni, nj, nk = m // tm, n // tn, k // tk

    if mode == "nn":
        a_spec = pl.BlockSpec((tm, tk), lambda i, j, kk, *_: (i, kk))
        b_blk, b_map = (tk, tn), (lambda i, j, kk: (kk, j))
    elif mode == "nt":
        a_spec = pl.BlockSpec((tm, tk), lambda i, j, kk, *_: (i, kk))
        b_blk, b_map = (tn, tk), (lambda i, j, kk: (j, kk))
    else:
        a_spec = pl.BlockSpec((tk, tm), lambda i, j, kk, *_: (kk, i))
        b_blk, b_map = (tk, tn), (lambda i, j, kk: (kk, j))
    if b_layer is None:
        b_spec = pl.BlockSpec(b_blk, lambda i, j, kk, *_: b_map(i, j, kk))
    else:
        b_spec = pl.BlockSpec((None,) + b_blk, lambda i, j, kk, *_: (b_layer,) + b_map(i, j, kk))

    if split is None:
        out_shape = jax.ShapeDtypeStruct((m, n), out_dtype)
        o_spec = pl.BlockSpec((tm, tn), lambda i, j, kk, *_: (i, j))
    elif split == "rows":
        out_shape = jax.ShapeDtypeStruct((2, m // 2, n), out_dtype)
        o_spec = pl.BlockSpec(
            (None, tm, tn), lambda i, j, kk, c_ref: (jnp.where(i // (ni // 2) == c_ref[0], 0, 1), i % (ni // 2), j))
    else:
        out_shape = jax.ShapeDtypeStruct((2, m, n // 2), out_dtype)
        o_spec = pl.BlockSpec(
            (None, tm, tn), lambda i, j, kk, c_ref: (jnp.where(j // (nj // 2) == c_ref[0], 0, 1), i, j % (nj // 2)))

    def body(*refs):
        if split is not None:
            refs = refs[1:]
        a_ref, b_ref, o_ref = refs[:3]
        part = lax.dot_general(a_ref[...].astype(BF16), b_ref[...].astype(BF16), dims, preferred_element_type=F32)
        if nk == 1:
            o_ref[...] = part.astype(o_ref.dtype)
        else:
            acc_ref = refs[3]
            kk = pl.program_id(2)

            @pl.when(kk == 0)
            def _():
                acc_ref[...] = part

            @pl.when(kk > 0)
            def _():
                acc_ref[...] += part

            @pl.when(kk == nk - 1)
            def _():
                o_ref[...] = acc_ref[...].astype(o_ref.dtype)

    scratch = [] if nk == 1 else [pltpu.VMEM((tm, tn), F32)]
    params = _params("parallel", "parallel", "arbitrary")
    if split is None:
        return pl.pallas_call(
            body, name=name, grid=(ni, nj, nk), in_specs=[a_spec, b_spec], out_specs=o_spec, out_shape=out_shape,
            scratch_shapes=scratch, compiler_params=params,
        )(a, b)
    grid_spec = pltpu.PrefetchScalarGridSpec(
        num_scalar_prefetch=1, grid=(ni, nj, nk), in_specs=[a_spec, b_spec], out_specs=o_spec, scratch_shapes=scratch)
    return pl.pallas_call(body, name=name, grid_spec=grid_spec, out_shape=out_shape, compiler_params=params)(c_idx, a, b)


def _attn_tile(seq):
    return _pick(seq, (256, 128))


def _heads_stacked(ref, b):
    return jnp.concatenate(
        [ref[b * BLOCK:(b + 1) * BLOCK, g * HEAD_DIM:(g + 1) * HEAD_DIM] for g in range(GQA_GROUP)], axis=0)


def _attn_scores(q_ref, k_ref, v_ref, sink_ref, kvh, i, b, tq, seq):
    rows = GQA_GROUP * BLOCK
    q0 = i * tq + b * BLOCK
    k_off = pl.multiple_of(jnp.clip(q0 - BLOCK, 0, seq - BAND), BLOCK)
    kw = k_ref[pl.ds(k_off, BAND), :]
    vw = v_ref[pl.ds(k_off, BAND), :]
    q_pos = q0 + (lax.broadcasted_iota(jnp.int32, (rows, BAND), 0) & (BLOCK - 1))
    k_pos = k_off + lax.broadcasted_iota(jnp.int32, (rows, BAND), 1)
    valid = jnp.abs(k_pos - q_pos) <= WINDOW
    qs = _heads_stacked(q_ref, b)
    s = lax.dot_general(qs, kw, (((1,), (1,)), ((), ())), preferred_element_type=F32) * (HEAD_DIM ** -0.5)
    s = jnp.where(valid, s, NEG_INF)
    sink = jnp.concatenate(
        [jnp.broadcast_to(sink_ref[pl.ds(kvh * GQA_GROUP + g, 1), :][:, :1], (BLOCK, 1)) for g in range(GQA_GROUP)], axis=0)
    m = jnp.maximum(jnp.max(s, axis=-1, keepdims=True), sink)
    p = jnp.exp(s - m)
    p_sink = jnp.exp(sink - m)
    denom = jnp.sum(p, axis=-1, keepdims=True) + p_sink
    return k_off, kw, vw, qs, p / denom, p_sink / denom


def _attn_fwd(name, qr, kr, vb, sink_b):
    seq, dq = qr.shape
    nkv = kr.shape[1] // HEAD_DIM
    tq = _attn_tile(seq)
    gw = GQA_GROUP * HEAD_DIM

    def body(q_ref, k_ref, v_ref, sink_ref, o_ref):
        kvh, i = pl.program_id(0), pl.program_id(1)
        for b in range(tq // BLOCK):
            _, _, vw, _, pn, _ = _attn_scores(q_ref, k_ref, v_ref, sink_ref, kvh, i, b, tq, seq)
            o = jnp.dot(pn.astype(BF16), vw, preferred_element_type=F32).astype(o_ref.dtype)
            for g in range(GQA_GROUP):
                o_ref[b * BLOCK:(b + 1) * BLOCK, g * HEAD_DIM:(g + 1) * HEAD_DIM] = o[g * BLOCK:(g + 1) * BLOCK]

    return pl.pallas_call(
        body, name=name, grid=(nkv, seq // tq),
        in_specs=[
            pl.BlockSpec((tq, gw), lambda h, i: (i, h)),
            pl.BlockSpec((seq, HEAD_DIM), lambda h, i: (0, h)),
            pl.BlockSpec((seq, HEAD_DIM), lambda h, i: (0, h)),
            pl.BlockSpec(sink_b.shape, lambda h, i: (0, 0)),
        ],
        out_specs=pl.BlockSpec((tq, gw), lambda h, i: (i, h)),
        out_shape=jax.ShapeDtypeStruct((seq, dq), BF16),
        compiler_params=_params("arbitrary", "arbitrary"),
    )(qr, kr, vb, sink_b)


def _attn_bwd(name, qr, kr, vb, sink_b, d_att):
    seq, dq = qr.shape
    dkv = kr.shape[1]
    nkv = dkv // HEAD_DIM
    tq = _attn_tile(seq)
    gw = GQA_GROUP * HEAD_DIM
    tn_dims = (((0,), (0,)), ((), ()))

    def body(q_ref, k_ref, v_ref, sink_ref, do_ref, dq_ref, dk_ref, dv_ref, dsink_ref):
        kvh, i = pl.program_id(0), pl.program_id(1)

        @pl.when(i == 0)
        def _():
            dk_ref[...] = jnp.zeros(dk_ref.shape, F32)
            dv_ref[...] = jnp.zeros(dv_ref.shape, F32)

        @pl.when((i == 0) & (kvh == 0))
        def _():
            dsink_ref[...] = jnp.zeros(dsink_ref.shape, F32)

        for b in range(tq // BLOCK):
            k_off, kw, vw, qs, pn, pn_sink = _attn_scores(q_ref, k_ref, v_ref, sink_ref, kvh, i, b, tq, seq)
            dos = _heads_stacked(do_ref, b)
            dp = lax.dot_general(dos, vw, (((1,), (1,)), ((), ())), preferred_element_type=F32)
            delta = jnp.sum(pn * dp, axis=-1, keepdims=True)
            ds = (pn * (dp - delta) * (HEAD_DIM ** -0.5)).astype(BF16)
            dqs = jnp.dot(ds, kw, preferred_element_type=F32)
            sink_term = pn_sink * delta
            for g in range(GQA_GROUP):
                dq_ref[b * BLOCK:(b + 1) * BLOCK, g * HEAD_DIM:(g + 1) * HEAD_DIM] = dqs[g * BLOCK:(g + 1) * BLOCK]
                d_sink = -jnp.sum(sink_term[g * BLOCK:(g + 1) * BLOCK], axis=0, keepdims=True)
                dsink_ref[pl.ds(kvh * GQA_GROUP + g, 1), :] += jnp.broadcast_to(d_sink, (1, LANE))
            dk_ref[pl.ds(k_off, BAND), :] += lax.dot_general(ds, qs, tn_dims, preferred_element_type=F32)
            dv_ref[pl.ds(k_off, BAND), :] += lax.dot_general(pn.astype(BF16), dos, tn_dims, preferred_element_type=F32)

    return pl.pallas_call(
        body, name=name, grid=(nkv, seq // tq),
        in_specs=[
            pl.BlockSpec((tq, gw), lambda h, i: (i, h)),
            pl.BlockSpec((seq, HEAD_DIM), lambda h, i: (0, h)),
            pl.BlockSpec((seq, HEAD_DIM), lambda h, i: (0, h)),
            pl.BlockSpec(sink_b.shape, lambda h, i: (0, 0)),
            pl.BlockSpec((tq, gw), lambda h, i: (i, h)),
        ],
        out_specs=[
            pl.BlockSpec((tq, gw), lambda h, i: (i, h)),
            pl.BlockSpec((seq, HEAD_DIM), lambda h, i: (0, h)),
            pl.BlockSpec((seq, HEAD_DIM), lambda h, i: (0, h)),
            pl.BlockSpec(sink_b.shape, lambda h, i: (0, 0)),
        ],
        out_shape=[
            jax.ShapeDtypeStruct((seq, dq), F32),
            jax.ShapeDtypeStruct((seq, dkv), F32),
            jax.ShapeDtypeStruct((seq, dkv), F32),
            jax.ShapeDtypeStruct(sink_b.shape, F32),
        ],
        compiler_params=_params("arbitrary", "arbitrary"),
    )(qr, kr, vb, sink_b, d_att)


def _halo_specs(tr, width, n_rows):
    per, last = tr // CONV_HALO, n_rows // CONV_HALO - 1
    return [
        pl.BlockSpec((tr, width), lambda i: (i, 0)),
        pl.BlockSpec((CONV_HALO, width), lambda i: (jnp.maximum(i * per - 1, 0), 0)),
        pl.BlockSpec((CONV_HALO, width), lambda i: (jnp.minimum((i + 1) * per, last), 0)),
    ]


def _fill_ext(ext_ref, main_ref, prev_ref, next_ref, n_steps, tr):
    i = pl.program_id(0)
    ext_ref[0:CONV_HALO, :] = jnp.where(i > 0, prev_ref[...], 0.0)
    ext_ref[CONV_HALO:CONV_HALO + tr, :] = main_ref[...]
    ext_ref[CONV_HALO + tr:, :] = jnp.where(i < n_steps - 1, next_ref[...], 0.0)


def _conv_taps(ext_ref, w_ref, out_ref, tr, width, flip):
    cw = min(CONV_LANES, width)
    for cc in range(width // cw):
        cols = slice(cc * cw, (cc + 1) * cw)
        for rc in range(tr // CONV_ROWS):
            acc = jnp.zeros((CONV_ROWS, cw), F32)
            for t in range(CONV_WIDTH):
                wt = CONV_WIDTH - 1 - t if flip else t
                acc += ext_ref[rc * CONV_ROWS + 1 + t:rc * CONV_ROWS + 1 + t + CONV_ROWS, cols] * w_ref[wt:wt + 1, cols]
            out_ref[rc * CONV_ROWS:(rc + 1) * CONV_ROWS, cols] = acc


def _ln(v, g, b):
    mu = jnp.mean(v, axis=-1, keepdims=True)
    vc = v - mu
    var = jnp.mean(vc * vc, axis=-1, keepdims=True)
    return vc * lax.rsqrt(var + LN_EPS) * g + b


def _conv_fwd(name, u, w32, ln_g, ln_b):
    n_rows, width = u.shape
    tr = min(ROW_TILE, n_rows)
    n_steps = n_rows // tr

    def body(main_ref, prev_ref, next_ref, w_ref, g_ref, b_ref, u2_ref, cv_ref, ext_ref):
        _fill_ext(ext_ref, main_ref, prev_ref, next_ref, n_steps, tr)
        _conv_taps(ext_ref, w_ref, u2_ref, tr, width, flip=False)
        u3 = _ln(u2_ref[...], g_ref[...], b_ref[...])
        cv_ref[...] = (u3 * _sigmoid(u3)).astype(cv_ref.dtype)

    vec = lambda a: pl.BlockSpec(a.shape, lambda i: (0, 0))
    return pl.pallas_call(
        body, name=name, grid=(n_steps,),
        in_specs=_halo_specs(tr, width, n_rows) + [vec(w32), vec(ln_g), vec(ln_b)],
        out_specs=[pl.BlockSpec((tr, width), lambda i: (i, 0))] * 2,
        out_shape=[jax.ShapeDtypeStruct((n_rows, width), F32), jax.ShapeDtypeStruct((n_rows, width), BF16)],
        scratch_shapes=[pltpu.VMEM((tr + 2 * CONV_HALO, width), F32)],
        compiler_params=_params("arbitrary"),
    )(u, u, u, w32, ln_g, ln_b)


def _conv_bwd_a(name, u, u2, d_cv, ln_g, ln_b):
    n_rows, width = u.shape
    tr = min(ROW_TILE, n_rows)
    n_steps = n_rows // tr
    cw = min(CONV_LANES, width)

    def body(main_ref, prev_ref, next_ref, u2_ref, dcv_ref, g_ref, b_ref, du2_ref, dw_ref, dg_ref, db_ref, ext_ref):
        @pl.when(pl.program_id(0) == 0)
        def _():
            dw_ref[...] = jnp.zeros(dw_ref.shape, F32)
            dg_ref[...] = jnp.zeros(dg_ref.shape, F32)
            db_ref[...] = jnp.zeros(db_ref.shape, F32)

        _fill_ext(ext_ref, main_ref, prev_ref, next_ref, n_steps, tr)

        def swish_ln(v, g, b):
            u3 = _ln(v, g, b)
            return u3 * _sigmoid(u3)

        _, vjp = jax.vjp(swish_ln, u2_ref[...], g_ref[...], b_ref[...])
        du2, dg, db = vjp(dcv_ref[...])
        du2_ref[...] = du2
        dg_ref[...] += dg
        db_ref[...] += db
        for cc in range(width // cw):
            cols = slice(cc * cw, (cc + 1) * cw)
            for t in range(CONV_WIDTH):
                acc = jnp.zeros((CONV_ROWS, cw), F32)
                for rc in range(tr // CONV_ROWS):
                    r0 = rc * CONV_ROWS
                    acc += du2_ref[r0:r0 + CONV_ROWS, cols] * ext_ref[r0 + 1 + t:r0 + 1 + t + CONV_ROWS, cols]
                dw_ref[t:t + 1, cols] += jnp.sum(acc, axis=0, keepdims=True)

    vec = lambda a: pl.BlockSpec(a.shape, lambda i: (0, 0))
    row = pl.BlockSpec((tr, width), lambda i: (i, 0))
    return pl.pallas_call(
        body, name=name, grid=(n_steps,),
        in_specs=_halo_specs(tr, width, n_rows) + [row, row, vec(ln_g), vec(ln_b)],
        out_specs=[row, pl.BlockSpec((32, width), lambda i: (0, 0)), vec(ln_g), vec(ln_b)],
        out_shape=[jax.ShapeDtypeStruct((n_rows, width), F32), jax.ShapeDtypeStruct((32, width), F32),
                   jax.ShapeDtypeStruct(ln_g.shape, F32), jax.ShapeDtypeStruct(ln_b.shape, F32)],
        scratch_shapes=[pltpu.VMEM((tr + 2 * CONV_HALO, width), F32)],
        compiler_params=_params("arbitrary"),
    )(u, u, u, u2, d_cv, ln_g, ln_b)


def _conv_bwd_b(name, du2, z, off_a, off_b, w32):
    n_rows, width = du2.shape
    tr = min(ROW_TILE, n_rows)
    n_steps = n_rows // tr
    bw = math.gcd(math.gcd(off_a, off_b), width)
    npc = width // bw

    def body(*refs):
        main_ref, prev_ref, next_ref = refs[:3]
        a_refs, b_refs = refs[3:3 + npc], refs[3 + npc:3 + 2 * npc]
        w_ref, out_ref, ext_ref, du_ref = refs[3 + 2 * npc:]
        _fill_ext(ext_ref, main_ref, prev_ref, next_ref, n_steps, tr)
        _conv_taps(ext_ref, w_ref, du_ref, tr, width, flip=True)
        for p in range(npc):
            cols = slice(p * bw, (p + 1) * bw)
            du = du_ref[:, cols]
            sg = _sigmoid(b_refs[p][...])
            out_ref[:, p * bw:(p + 1) * bw] = (du * sg).astype(out_ref.dtype)
            out_ref[:, width + p * bw:width + (p + 1) * bw] = (du * a_refs[p][...] * sg * (1.0 - sg)).astype(out_ref.dtype)

    def piece(off, p):
        return pl.BlockSpec((tr, bw), functools.partial(lambda i, blk: (i, blk), blk=off // bw + p))

    in_specs = _halo_specs(tr, width, n_rows)
    in_specs += [piece(off_a, p) for p in range(npc)] + [piece(off_b, p) for p in range(npc)]
    in_specs.append(pl.BlockSpec(w32.shape, lambda i: (0, 0)))
    return pl.pallas_call(
        body, name=name, grid=(n_steps,), in_specs=in_specs,
        out_specs=pl.BlockSpec((tr, 2 * width), lambda i: (i, 0)),
        out_shape=jax.ShapeDtypeStruct((n_rows, 2 * width), BF16),
        scratch_shapes=[pltpu.VMEM((tr + 2 * CONV_HALO, width), F32), pltpu.VMEM((tr, width), F32)],
        compiler_params=_params("arbitrary"),
    )(du2, du2, du2, *([z] * (2 * npc)), w32)


def _place():
    return lax.axis_index("x"), lax.axis_index("y"), lax.axis_index("c")


def _flip(v, m):
    return 1 - v if m else v


def _gather_small(name, v, masks):
    varies = [any(m[a] for m in masks) for a in range(3)]
    n = len(masks) + 1

    def slot(pos):
        idx = 0
        for a in range(3):
            if varies[a]:
                idx = idx * 2 + pos[a]
        return idx

    def body(v_ref, o_ref, send_sems, recv_sems, local_sem):
        me = _place()
        mine = pltpu.make_async_copy(v_ref, o_ref.at[slot(me)], local_sem)
        mine.start()
        peers = [tuple(_flip(me[a], m[a]) for a in range(3)) for m in masks]
        sends = [pltpu.make_async_remote_copy(v_ref, o_ref.at[slot(me)], send_sems.at[k], recv_sems.at[k],
                                              device_id=peer, device_id_type=MESH) for k, peer in enumerate(peers)]
        for cp in sends:
            cp.start()
        for k, peer in enumerate(peers):
            pltpu.make_async_remote_copy(v_ref, o_ref.at[slot(peer)], send_sems.at[k], recv_sems.at[k],
                                         device_id=peer, device_id_type=MESH).wait_recv()
        for cp in sends:
            cp.wait_send()
        mine.wait()

    return pl.pallas_call(
        body, name=name, in_specs=[HBM_SPEC], out_specs=HBM_SPEC,
        out_shape=jax.ShapeDtypeStruct((n,) + v.shape, v.dtype),
        scratch_shapes=[pltpu.SemaphoreType.DMA((n - 1,)), pltpu.SemaphoreType.DMA((n - 1,)), pltpu.SemaphoreType.DMA(())],
    )(v)


ALL_DEVICES = [(mx, my, mc) for mx in (0, 1) for my in (0, 1) for mc in (0, 1)][1:]
SAME_CORE_CHIPS = [(1, 0, 0), (0, 1, 0), (1, 1, 0)]


def _chips(x, y):
    return [(1 - x, y), (x, 1 - y), (1 - x, 1 - y)]


def _cast_into(name, w, layer, kind, chip_idx):
    _, r, cc = w.shape
    tr = _rows_within(r, cc * 6)
    steps = r // tr
    if kind == "col":
        shape, o_spec = (r, 4 * cc), pl.BlockSpec((tr, cc), lambda i, s_ref: (i, s_ref[0]))
    else:
        shape, o_spec = (4 * r, cc), pl.BlockSpec((tr, cc), lambda i, s_ref: (s_ref[0] * steps + i, 0))

    def body(s_ref, w_ref, o_ref):
        o_ref[...] = w_ref[...].astype(o_ref.dtype)

    grid_spec = pltpu.PrefetchScalarGridSpec(
        num_scalar_prefetch=1, grid=(steps,),
        in_specs=[pl.BlockSpec((None, tr, cc), lambda i, s_ref: (layer, i, 0))], out_specs=o_spec)
    return pl.pallas_call(body, name=name, grid_spec=grid_spec, out_shape=jax.ShapeDtypeStruct(shape, BF16),
                          compiler_params=_params("arbitrary"))(chip_idx, w)


def _ag_weights(name, fulls, shapes, kinds):
    n = len(fulls)
    for r, _ in shapes:
        assert r % 32 == 0

    def window(o_ref, j, s, h):
        r, cc = shapes[j]
        hr = r // 2
        if kinds[j] == "col":
            return o_ref.at[pl.ds(pl.multiple_of(h * hr, 16), hr), pl.ds(pl.multiple_of(s * cc, LANE), cc)]
        return o_ref.at[pl.ds(pl.multiple_of(s * r + h * hr, 16), hr), :]

    def body(*refs):
        outs = refs[n:2 * n]
        send_sems, recv_sems, pass_send_sems, pass_recv_sems = refs[2 * n:]
        x, y, c = _place()
        s_me = 2 * x + y
        chips = _chips(x, y)
        sibling = (x, y, 1 - c)
        first, passed = [], []
        for j in range(n):
            mine = window(outs[j], j, s_me, c)
            for k, chip in enumerate(chips):
                cp = pltpu.make_async_remote_copy(mine, mine, send_sems.at[3 * j + k], recv_sems.at[3 * j + k],
                                                  device_id=(*chip, c), device_id_type=MESH)
                cp.start()
                first.append(cp)
        for j in range(n):
            for k, chip in enumerate(chips):
                win = window(outs[j], j, 2 * chip[0] + chip[1], c)
                pltpu.make_async_remote_copy(win, win, send_sems.at[3 * j + k], recv_sems.at[3 * j + k],
                                             device_id=(*chip, c), device_id_type=MESH).wait_recv()
                cp = pltpu.make_async_remote_copy(win, win, pass_send_sems.at[3 * j + k], pass_recv_sems.at[3 * j + k],
                                                  device_id=sibling, device_id_type=MESH)
                cp.start()
                passed.append(cp)
        for j in range(n):
            for k, chip in enumerate(chips):
                win = window(outs[j], j, 2 * chip[0] + chip[1], 1 - c)
                pltpu.make_async_remote_copy(win, win, pass_send_sems.at[3 * j + k], pass_recv_sems.at[3 * j + k],
                                             device_id=sibling, device_id_type=MESH).wait_recv()
        for cp in first + passed:
            cp.wait_send()

    return pl.pallas_call(
        body, name=name, in_specs=[HBM_SPEC] * n, out_specs=[HBM_SPEC] * n,
        out_shape=[jax.ShapeDtypeStruct(f.shape, f.dtype) for f in fulls],
        input_output_aliases={j: j for j in range(n)},
        scratch_shapes=[pltpu.SemaphoreType.DMA((3 * n,)) for _ in range(4)],
    )(*fulls)


def _rs_pair(name, grads):
    n = len(grads)

    def body(*refs):
        ins, outs = refs[:n], refs[n:2 * n]
        send_sems, recv_sems = refs[2 * n:]
        x, y, c = _place()
        cps = [pltpu.make_async_remote_copy(ins[j].at[1], outs[j], send_sems.at[j], recv_sems.at[j],
                                            device_id=(x, y, 1 - c), device_id_type=MESH) for j in range(n)]
        for cp in cps:
            cp.start()
        for cp in cps:
            cp.wait()

    return pl.pallas_call(
        body, name=name, in_specs=[HBM_SPEC] * n, out_specs=[HBM_SPEC] * n,
        out_shape=[jax.ShapeDtypeStruct(g.shape[1:], g.dtype) for g in grads],
        scratch_shapes=[pltpu.SemaphoreType.DMA((n,)), pltpu.SemaphoreType.DMA((n,))],
    )(*grads)


def _rs_chips(name, halves, kinds):
    n = len(halves)
    shapes = [(h.shape[0], h.shape[1] // 4) if kinds[j] == "col" else (h.shape[0] // 4, h.shape[1])
              for j, h in enumerate(halves)]

    def part(ref, j, s):
        r, cc = shapes[j]
        if kinds[j] == "col":
            return ref.at[:, pl.ds(pl.multiple_of(s * cc, LANE), cc)]
        return ref.at[pl.ds(pl.multiple_of(s * r, 16), r), :]

    def body(*refs):
        ins, outs = refs[:n], refs[n:2 * n]
        send_sems, recv_sems, local_sems = refs[2 * n:]
        x, y, c = _place()
        s_me = 2 * x + y
        chips = _chips(x, y)
        local = [pltpu.make_async_copy(part(ins[j], j, s_me), outs[j].at[s_me], local_sems.at[j]) for j in range(n)]
        for cp in local:
            cp.start()
        sends = []
        for j in range(n):
            for k, chip in enumerate(chips):
                cp = pltpu.make_async_remote_copy(part(ins[j], j, 2 * chip[0] + chip[1]), outs[j].at[s_me],
                                                  send_sems.at[3 * j + k], recv_sems.at[3 * j + k],
                                                  device_id=(*chip, c), device_id_type=MESH)
                cp.start()
                sends.append(cp)
        for j in range(n):
            for k, chip in enumerate(chips):
                dst = outs[j].at[2 * chip[0] + chip[1]]
                pltpu.make_async_remote_copy(dst, dst, send_sems.at[3 * j + k], recv_sems.at[3 * j + k],
                                             device_id=(*chip, c), device_id_type=MESH).wait_recv()
        for cp in sends:
            cp.wait_send()
        for cp in local:
            cp.wait()

    return pl.pallas_call(
        body, name=name, in_specs=[HBM_SPEC] * n, out_specs=[HBM_SPEC] * n,
        out_shape=[jax.ShapeDtypeStruct((4,) + shapes[j], halves[j].dtype) for j in range(n)],
        scratch_shapes=[pltpu.SemaphoreType.DMA((3 * n,)), pltpu.SemaphoreType.DMA((3 * n,)), pltpu.SemaphoreType.DMA((n,))],
    )(*halves)


def _rs_join(name, pairs):
    n = len(pairs)

    def body(*refs):
        outs = refs[n:2 * n]
        send_sems, recv_sems = refs[2 * n:]
        x, y, c = _place()
        sends = [pltpu.make_async_remote_copy(outs[j].at[c], outs[j].at[c], send_sems.at[j], recv_sems.at[j],
                                              device_id=(x, y, 1 - c), device_id_type=MESH) for j in range(n)]
        for cp in sends:
            cp.start()
        for j in range(n):
            theirs = outs[j].at[1 - c]
            pltpu.make_async_remote_copy(theirs, theirs, send_sems.at[j], recv_sems.at[j],
                                         device_id=(x, y, 1 - c), device_id_type=MESH).wait_recv()
        for cp in sends:
            cp.wait_send()

    return pl.pallas_call(
        body, name=name, in_specs=[HBM_SPEC] * n, out_specs=[HBM_SPEC] * n,
        out_shape=[jax.ShapeDtypeStruct(p.shape, p.dtype) for p in pairs],
        input_output_aliases={j: j for j in range(n)},
        scratch_shapes=[pltpu.SemaphoreType.DMA((n,)), pltpu.SemaphoreType.DMA((n,))],
    )(*pairs)


def _pair_sum(name, mine_other, got):
    _, r, cc = mine_other.shape
    tr = _rows_within(r, 3 * cc * mine_other.dtype.itemsize, (256, 128, 64, 32, 16))

    def body(a_ref, b_ref, o_ref):
        o_ref[...] = (a_ref[...].astype(F32) + b_ref[...].astype(F32)).astype(o_ref.dtype)

    return pl.pallas_call(
        body, name=name, grid=(r // tr,),
        in_specs=[pl.BlockSpec((None, tr, cc), lambda i: (0, i, 0)), pl.BlockSpec((tr, cc), lambda i: (i, 0))],
        out_specs=pl.BlockSpec((tr, cc), lambda i: (i, 0)),
        out_shape=jax.ShapeDtypeStruct((r, cc), mine_other.dtype), compiler_params=_params("parallel"),
    )(mine_other, got)


def _sum_slots(name, parts, into_slot=None):
    n, r, cc = parts.shape
    tr = _rows_within(r, cc * (n * parts.dtype.itemsize + 4), (256, 128, 64, 32, 16, 8))

    def body(*refs):
        p_ref, o_ref = refs[-2:]
        acc = p_ref[0].astype(F32)
        for s in range(1, n):
            acc = acc + p_ref[s].astype(F32)
        o_ref[...] = acc

    in_spec = pl.BlockSpec((n, tr, cc), lambda i, *_: (0, i, 0))
    if into_slot is None:
        return pl.pallas_call(
            body, name=name, grid=(r // tr,), in_specs=[in_spec], out_specs=pl.BlockSpec((tr, cc), lambda i: (i, 0)),
            out_shape=jax.ShapeDtypeStruct((r, cc), F32), compiler_params=_params("parallel"),
        )(parts)
    grid_spec = pltpu.PrefetchScalarGridSpec(
        num_scalar_prefetch=1, grid=(r // tr,), in_specs=[in_spec],
        out_specs=pl.BlockSpec((None, tr, cc), lambda i, c_ref: (c_ref[0], i, 0)))
    return pl.pallas_call(body, name=name, grid_spec=grid_spec, out_shape=jax.ShapeDtypeStruct((2, r, cc), F32),
                          compiler_params=_params("arbitrary"))(into_slot, parts)


def _adamw_math(w, g, m, v):
    m = ADAM_B1 * m + (1.0 - ADAM_B1) * g
    v = ADAM_B2 * v + (1.0 - ADAM_B2) * jnp.square(g)
    m_hat = m / (1.0 - ADAM_B1 ** ADAM_STEP)
    v_hat = v / (1.0 - ADAM_B2 ** ADAM_STEP)
    delta = -ADAM_LR * (m_hat / (jnp.sqrt(v_hat) + ADAM_EPS) + ADAM_WD * w)
    return delta, m, v


def _adamw_layer(name, layer, g_pair, kind, w, m, v, prev):
    n_layers, r, cc = w.shape
    if prev is None:
        prev = tuple(lax.empty(w.shape, F32) for _ in range(4))
    if kind == "col":
        g = g_pair.reshape(r, cc)
        tr = _rows_within(r, 8 * cc * 4)
        grid = (r // tr,)
        g_spec = pl.BlockSpec((tr, cc), lambda i: (i, 0))
        blk = pl.BlockSpec((None, tr, cc), lambda i: (layer, i, 0))
    else:
        g = g_pair
        tr = _rows_within(r, 4 * cc * 4)
        grid = (r // tr, 2)
        g_spec = pl.BlockSpec((None, tr, cc // 2), lambda i, h: (h, i, 0))
        blk = pl.BlockSpec((None, tr, cc // 2), lambda i, h: (layer, i, h))

    def body(g_ref, w_ref, m_ref, v_ref, *rest):
        og_ref, od_ref, om_ref, ov_ref = rest[4:]
        gv = g_ref[...]
        delta, m2, v2 = _adamw_math(w_ref[...], gv, m_ref[...], v_ref[...])
        og_ref[...] = gv
        od_ref[...] = delta
        om_ref[...] = m2
        ov_ref[...] = v2

    return pl.pallas_call(
        body, name=name, grid=grid,
        in_specs=[g_spec, blk, blk, blk] + [HBM_SPEC] * 4,
        out_specs=[blk] * 4, out_shape=[jax.ShapeDtypeStruct(w.shape, F32)] * 4,
        input_output_aliases={4: 0, 5: 1, 6: 2, 7: 3}, compiler_params=_params(*(["parallel"] * len(grid))),
    )(g, w, m, v, *prev)


def _adamw_small(name, g, w, m, v):
    def body(g_ref, w_ref, m_ref, v_ref, od_ref, om_ref, ov_ref):
        delta, m2, v2 = _adamw_math(w_ref[...], g_ref[...], m_ref[...], v_ref[...])
        od_ref[...] = delta
        om_ref[...] = m2
        ov_ref[...] = v2

    return pl.pallas_call(body, name=name, out_shape=[jax.ShapeDtypeStruct(w.shape, F32)] * 3)(g, w, m, v)


def _adamw_ada(name, c16, dmod16, w, m, v):
    n_layers, d, cols = w.shape
    tr = _rows_within(d, 7 * cols * 4, (256, 128))
    blk = pl.BlockSpec((None, tr, cols), lambda l, i: (l, i, 0))

    def body(c_ref, dm_ref, w_ref, m_ref, v_ref, og_ref, od_ref, om_ref, ov_ref):
        gv = lax.dot_general(c_ref[...], dm_ref[...], (((0,), (0,)), ((), ())), preferred_element_type=F32)
        delta, m2, v2 = _adamw_math(w_ref[...], gv, m_ref[...], v_ref[...])
        og_ref[...] = gv
        od_ref[...] = delta
        om_ref[...] = m2
        ov_ref[...] = v2

    return pl.pallas_call(
        body, name=name, grid=(n_layers, d // tr),
        in_specs=[pl.BlockSpec((16, tr), lambda l, i: (0, i)), pl.BlockSpec((None, 16, cols), lambda l, i: (l, 0, 0)),
                  blk, blk, blk],
        out_specs=[blk] * 4, out_shape=[jax.ShapeDtypeStruct(w.shape, F32)] * 4,
        compiler_params=_params("parallel", "parallel"),
    )(c16, dmod16, w, m, v)


def kernel(x, c, w_ada, b_ada, w_in, sink, w_dw, conv_ln_g, conv_ln_b, w_oa, w_ob, w_out, ln1_g, ln1_b, w_gu, w_down, ln2_g, ln2_b, loss_target, m_w_ada, m_b_ada, m_w_in, m_sink, m_w_dw, m_conv_ln_g, m_conv_ln_b, m_w_oa, m_w_ob, m_w_out, m_ln1_g, m_ln1_b, m_w_gu, m_w_down, m_ln2_g, m_ln2_b, v_w_ada, v_b_ada, v_w_in, v_sink, v_w_dw, v_conv_ln_g, v_conv_ln_b, v_w_oa, v_w_ob, v_w_out, v_ln1_g, v_ln1_b, v_w_gu, v_w_down, v_ln2_g, v_ln2_b):
    seq, d = x.shape[1], x.shape[2]
    n_layers = w_in.shape[0]
    d_in = 4 * w_in.shape[2]
    d_ff = 4 * w_down.shape[1]
    hq = d // HEAD_DIM
    dkv = (hq // GQA_GROUP) * HEAD_DIM
    off_k, off_v, off_ga, off_gb = d, d + dkv, d + 2 * dkv, 2 * d + 2 * dkv
    off_gta, off_gtb = 3 * d + 2 * dkv, 4 * d + 2 * dkv
    assert d_in == 5 * d + 2 * dkv and seq % ROW_TILE == 0 and seq >= BAND
    alpha = (2.0 * n_layers) ** 0.25

    xi, yi, ci = _place()
    chip = 2 * xi + yi
    batch = 4 * xi + 2 * yi + ci
    c_idx = jnp.reshape(ci, (1,)).astype(jnp.int32)
    x2 = x[0]
    target = loss_target[0]

    c_act = jax.nn.silu(c)
    c_all = _gather_small("gather_c", c_act, ALL_DEVICES).reshape(8, d)
    c16 = jnp.concatenate([c_all, jnp.zeros((8, d), F32)], axis=0).astype(BF16)
    mod_cols = [_mm(f"mod_{l}", c16, w_ada, "nn", F32, b_layer=l) for l in range(n_layers)]
    mod_all = _gather_small("gather_mod", jnp.stack(mod_cols), SAME_CORE_CHIPS)
    mod = lax.dynamic_index_in_dim(mod_all, batch, axis=2, keepdims=False)
    mod = jnp.transpose(mod, (1, 0, 2)).reshape(n_layers, N_MOD * d) + b_ada
    mod = mod.reshape(n_layers, N_MOD, 1, d)
    sh_a, sc_a, gt_a, sh_f, sc_f, gt_f = (mod[:, j] for j in range(N_MOD))

    pos = jnp.arange(seq, dtype=F32)
    inv_freq = ROPE_THETA ** (-jnp.arange(0, ROPE_DIM, 2, dtype=F32) / ROPE_DIM)
    ang = pos[:, None] * inv_freq[None, :]
    cos, sin = jnp.cos(ang), jnp.sin(ang)
    half = ROPE_DIM // 2
    rest = HEAD_DIM - ROPE_DIM
    t_cs = jnp.concatenate([cos, cos, jnp.ones((seq, rest), F32)], axis=1)
    t_up = jnp.concatenate([-sin, jnp.zeros((seq, rest + half), F32)], axis=1)
    t_dn = jnp.concatenate([jnp.zeros((seq, half), F32), sin, jnp.zeros((seq, rest), F32)], axis=1)

    def rope(t, cs, up, dn):
        w = t.shape[1]
        reps = (1, w // HEAD_DIM)
        return (t * jnp.tile(cs, reps) + pltpu.roll(t, w - half, 1) * jnp.tile(up, reps)
                + pltpu.roll(t, half, 1) * jnp.tile(dn, reps))

    def rope_t(dt, cs, up, dn):
        w = dt.shape[1]
        reps = (1, w // HEAD_DIM)
        return (dt * jnp.tile(cs, reps) + pltpu.roll(dt * jnp.tile(up, reps), half, 1)
                + pltpu.roll(dt * jnp.tile(dn, reps), w - half, 1))

    tables = [(t_cs, 0, HEAD_DIM), (t_up, 0, HEAD_DIM), (t_dn, 0, HEAD_DIM)]

    kinds = ("col", "col", "row", "row", "row", "row")
    big_weights = (w_in, w_gu, w_oa, w_ob, w_out, w_down)
    chip_idx = jnp.reshape(chip, (1,)).astype(jnp.int32)
    gathered = []
    for l in range(n_layers):
        fulls = [_cast_into(f"cast_w_{l}_{j}", w, l, kinds[j], chip_idx) for j, w in enumerate(big_weights)]
        gathered.append(_ag_weights(f"gather_w_{l}", fulls, [w.shape[1:] for w in big_weights], kinds))
    w_dw_all = _gather_small("gather_dw", w_dw, SAME_CORE_CHIPS)
    w_dw_full = jnp.transpose(w_dw_all, (1, 2, 0, 3)).reshape(n_layers, CONV_WIDTH, d)
    w_dw32 = jnp.pad(w_dw_full, ((0, 0), (0, 32 - CONV_WIDTH), (0, 0)))
    sink_b = jnp.broadcast_to(sink[:, :, None], (n_layers, hq, LANE))

    def vec(a, l):
        return a[l][None, :]

    def res_ln(xprev, y, gt, g, b, scn, shn):
        xn = _ln(alpha * xprev + (1.0 + gt) * y, g, b)
        return xn, xn * (1.0 + scn) + shn

    def merge(ya, yb, ga, gb):
        return _sigmoid(ga) * ya + _sigmoid(gb) * yb

    def swiglu(gate, up):
        return gate * _sigmoid(gate) * up

    h = _rowwise("modulate_in", lambda xv, sc, sh: xv * (1.0 + sc) + sh, seq, [(x2, 0, d)], [sc_a[0], sh_a[0]],
                 [(d, BF16)])
    xprev = x2
    saved = []
    for l in range(n_layers):
        wi, wg, woa, wob, wout, wdn = gathered[l]
        z = _mm(f"in_proj_{l}", h, wi, "nn", F32)
        qr, kr, vb = _rowwise(
            f"qkv_prep_{l}", lambda q, k, v, cs, up, dn: (rope(q, cs, up, dn), rope(k, cs, up, dn), v), seq,
            [(z, 0, d), (z, off_k, dkv), (z, off_v, dkv)] + tables, [], [(d, BF16), (dkv, BF16), (dkv, BF16)])
        att = _attn_fwd(f"attn_{l}", qr, kr, vb, sink_b[l])
        y_a = _mm(f"attn_out_{l}", att, woa, "nn", F32)
        u = _rowwise(f"glu_{l}", lambda a, b: a * _sigmoid(b), seq, [(z, off_ga, d), (z, off_gb, d)], [], [(d, F32)])
        u2, cv = _conv_fwd(f"conv_{l}", u, w_dw32[l], vec(conv_ln_g, l), vec(conv_ln_b, l))
        y_b = _mm(f"conv_out_{l}", cv, wob, "nn", F32)
        mg = _rowwise(f"merge_{l}", merge, seq, [(y_a, 0, d), (y_b, 0, d), (z, off_gta, d), (z, off_gtb, d)], [],
                      [(d, BF16)])
        o = _mm(f"mix_out_{l}", mg, wout, "nn", F32)
        x1, h2 = _rowwise(f"res_ln1_{l}", res_ln, seq, [(xprev, 0, d), (o, 0, d)],
                          [gt_a[l], vec(ln1_g, l), vec(ln1_b, l), sc_f[l], sh_f[l]], [(d, F32), (d, BF16)])
        gu = _mm(f"ffn_up_{l}", h2, wg, "nn", F32)
        f = _rowwise(f"swiglu_{l}", swiglu, seq, [(gu, 0, d_ff), (gu, d_ff, d_ff)], [], [(d_ff, BF16)])
        ffn = _mm(f"ffn_down_{l}", f, wdn, "nn", F32)
        saved.append(dict(xprev=xprev, h=h, z=z, qr=qr, kr=kr, vb=vb, att=att, u=u, u2=u2, cv=cv, y_a=y_a, y_b=y_b,
                          mg=mg, o=o, x1=x1, h2=h2, gu=gu, f=f, ffn=ffn))
        if l + 1 < n_layers:
            xprev, h = _rowwise(f"res_ln2_{l}", res_ln, seq, [(x1, 0, d), (ffn, 0, d)],
                                [gt_f[l], vec(ln2_g, l), vec(ln2_b, l), sc_a[l + 1], sh_a[l + 1]], [(d, F32), (d, BF16)])

    def res_ln_bwd(xp, y, dxn, dh, gt, g, b, scn, shn):
        _, vjp = jax.vjp(res_ln, xp, y, gt, g, b, scn, shn)
        return vjp((dxn, dh))

    def last_ln_bwd(xp, y, tgt, gt, g, b):
        def head(xp_, y_, gt_, g_, b_):
            return _ln(alpha * xp_ + (1.0 + gt_) * y_, g_, b_)
        out, vjp = jax.vjp(head, xp, y, gt, g, b)
        err = out - tgt
        loss = 0.5 * jnp.sum(jnp.sum(err * err, axis=-1, keepdims=True) / d, axis=0, keepdims=True)
        return vjp(err / d) + (jnp.broadcast_to(loss, (1, LANE)),)

    def merge_bwd(dmg, ya, yb, ga, gb):
        _, vjp = jax.vjp(merge, ya, yb, ga, gb)
        dya, dyb, dga, dgb = vjp(dmg)
        return dya, dyb, jnp.concatenate([dga, dgb], axis=1)

    def swiglu_bwd(df, gate, up):
        _, vjp = jax.vjp(swiglu, gate, up)
        return jnp.concatenate(vjp(df), axis=1)

    vec_d = ((1, d), F32)
    small = [None] * n_layers
    big = None
    loss_part = None
    dxn = dh = None
    for l in reversed(range(n_layers)):
        sv = saved[l]
        wi, wg, woa, wob, wout, wdn = gathered[l]
        ln2 = [gt_f[l], vec(ln2_g, l), vec(ln2_b, l)]
        if l + 1 == n_layers:
            dx1, dffn, d_gtf, d_g2, d_b2, loss_part = _rowwise(
                "last_ln_bwd", last_ln_bwd, seq, [(sv["x1"], 0, d), (sv["ffn"], 0, d), (target, 0, d)], ln2,
                [(d, F32), (d, BF16)], [vec_d, vec_d, vec_d, ((1, LANE), F32)])
            d_sca_next = d_sha_next = None
        else:
            dx1, dffn, d_gtf, d_g2, d_b2, d_sca_next, d_sha_next = _rowwise(
                f"res_ln2_bwd_{l}", res_ln_bwd, seq, [(sv["x1"], 0, d), (sv["ffn"], 0, d), (dxn, 0, d), (dh, 0, d)],
                ln2 + [sc_a[l + 1], sh_a[l + 1]], [(d, F32), (d, BF16)], [vec_d] * 5)
            small[l + 1]["sc_a"], small[l + 1]["sh_a"] = d_sca_next, d_sha_next
        df = _mm(f"ffn_down_dx_{l}", dffn, wdn, "nt", F32)
        g_down = _mm(f"ffn_down_dw_{l}", sv["f"], dffn, "tn", BF16, split="cols", c_idx=c_idx)
        dgu = _rowwise(f"swiglu_bwd_{l}", swiglu_bwd, seq, [(df, 0, d_ff), (sv["gu"], 0, d_ff), (sv["gu"], d_ff, d_ff)],
                       [], [(2 * d_ff, BF16)])
        dh2 = _mm(f"ffn_up_dx_{l}", dgu, wg, "nt", F32)
        g_gu = _mm(f"ffn_up_dw_{l}", sv["h2"], dgu, "tn", BF16, split="rows", c_idx=c_idx)
        dxp, d_o, d_gta, d_g1, d_b1, d_scf, d_shf = _rowwise(
            f"res_ln1_bwd_{l}", res_ln_bwd, seq, [(sv["xprev"], 0, d), (sv["o"], 0, d), (dx1, 0, d), (dh2, 0, d)],
            [gt_a[l], vec(ln1_g, l), vec(ln1_b, l), sc_f[l], sh_f[l]], [(d, F32), (d, BF16)], [vec_d] * 5)
        dmg = _mm(f"mix_out_dx_{l}", d_o, wout, "nt", F32)
        g_out = _mm(f"mix_out_dw_{l}", sv["mg"], d_o, "tn", BF16, split="cols", c_idx=c_idx)
        z = sv["z"]
        dya, dyb, d_gates = _rowwise(
            f"merge_bwd_{l}", merge_bwd, seq,
            [(dmg, 0, d), (sv["y_a"], 0, d), (sv["y_b"], 0, d), (z, off_gta, d), (z, off_gtb, d)], [],
            [(d, BF16), (d, BF16), (2 * d, BF16)])
        d_att = _mm(f"attn_out_dx_{l}", dya, woa, "nt", BF16)
        g_oa = _mm(f"attn_out_dw_{l}", sv["att"], dya, "tn", BF16, split="cols", c_idx=c_idx)
        d_cv = _mm(f"conv_out_dx_{l}", dyb, wob, "nt", F32)
        g_ob = _mm(f"conv_out_dw_{l}", sv["cv"], dyb, "tn", BF16, split="cols", c_idx=c_idx)
        du2, d_wdw, d_cg, d_cb = _conv_bwd_a(f"conv_bwd_a_{l}", sv["u"], sv["u2"], d_cv, vec(conv_ln_g, l),
                                             vec(conv_ln_b, l))
        d_glu = _conv_bwd_b(f"conv_bwd_b_{l}", du2, z, off_ga, off_gb, w_dw32[l])
        dqr, dkr, dvb, d_sink = _attn_bwd(f"attn_bwd_{l}", sv["qr"], sv["kr"], sv["vb"], sink_b[l], d_att)
        d_qkv = _rowwise(
            f"qkv_bwd_{l}",
            lambda dq_, dk_, dv_, cs, up, dn: jnp.concatenate([rope_t(dq_, cs, up, dn), rope_t(dk_, cs, up, dn), dv_], axis=1),
            seq, [(dqr, 0, d), (dkr, 0, dkv), (dvb, 0, dkv)] + tables, [], [(d + 2 * dkv, BF16)])
        dz = jnp.concatenate([d_qkv, d_glu, d_gates], axis=1)
        dh = _mm(f"in_proj_dx_{l}", dz, wi, "nt", F32)
        g_in = _mm(f"in_proj_dw_{l}", sv["h"], dz, "tn", BF16, split="rows", c_idx=c_idx)
        dxn = dxp
        small[l] = dict(gt_a=d_gta, sh_f=d_shf, sc_f=d_scf, gt_f=d_gtf, ln1_g=d_g1, ln1_b=d_b1, ln2_g=d_g2, ln2_b=d_b2,
                        conv_ln_g=d_cg, conv_ln_b=d_cb, sink=d_sink[:, :1].reshape(1, hq), w_dw=d_wdw[:CONV_WIDTH])

        grads = [g_in, g_gu, g_oa, g_ob, g_out, g_down]
        got = _rs_pair(f"rs_pair_{l}", grads)
        halves = [_pair_sum(f"pair_sum_{l}_{j}", grads[j], got[j]) for j in range(6)]
        parts = _rs_chips(f"rs_chips_{l}", halves, kinds)
        reduced = [_sum_slots(f"sum_chips_{l}_{j}", parts[j], into_slot=c_idx) for j in range(6)]
        full = _rs_join(f"rs_join_{l}", reduced)
        stacks = ((w_in, m_w_in, v_w_in), (w_gu, m_w_gu, v_w_gu), (w_oa, m_w_oa, v_w_oa), (w_ob, m_w_ob, v_w_ob),
                  (w_out, m_w_out, v_w_out), (w_down, m_w_down, v_w_down))
        big = [_adamw_layer(f"adamw_{l}_{j}", l, full[j], kinds[j], *stacks[j], None if big is None else big[j])
               for j in range(6)]

    grad_x, d_sca0, d_sha0 = _rowwise(
        "modulate_in_bwd", lambda xv, dhv, dxv, sc: (dxv + dhv * (1.0 + sc), jnp.sum(dhv * xv, axis=0, keepdims=True),
                                                     jnp.sum(dhv, axis=0, keepdims=True)),
        seq, [(x2, 0, d), (dh, 0, d), (dxn, 0, d)], [sc_a[0]], [(d, F32)], [vec_d, vec_d])
    small[0]["sc_a"], small[0]["sh_a"] = d_sca0, d_sha0

    order = ("sh_a", "sc_a", "gt_a", "sh_f", "sc_f", "gt_f", "conv_ln_g", "conv_ln_b", "ln1_g", "ln1_b", "ln2_g", "ln2_b")
    rows = []
    for l in range(n_layers):
        rows += [small[l][k] for k in order]
        rows.append(jnp.pad(small[l]["sink"], ((0, 0), (0, d - hq))))
        rows.append(small[l]["w_dw"])
    rows.append(jnp.pad(loss_part, ((0, 0), (0, d - LANE))))
    n_small = sum(r.shape[0] for r in rows)
    pad_rows = (-n_small) % 8
    packed = jnp.concatenate(rows + [jnp.zeros((pad_rows, d), F32)], axis=0)
    everyone = _gather_small("gather_small_grads", packed, ALL_DEVICES)
    total = _sum_slots("sum_small_grads", everyone)
    per_layer = len(order) + 1 + CONV_WIDTH
    tot = total[:n_layers * per_layer].reshape(n_layers, per_layer, d)
    g_mod = tot[:, :N_MOD].reshape(n_layers, N_MOD * d)
    g_small = {k: tot[:, N_MOD + j] for j, k in enumerate(order[N_MOD:])}
    g_sink = tot[:, len(order), :hq]
    g_dw_full = tot[:, len(order) + 1:]
    cols_dw = w_dw.shape[2]
    g_dw = lax.dynamic_slice_in_dim(g_dw_full, chip * cols_dw, cols_dw, axis=2)
    loss = total[n_layers * per_layer, 0]

    d_mod_all = everyone[:, :n_layers * per_layer].reshape(8, n_layers, per_layer, d)[:, :, :N_MOD]
    d_mod_all = d_mod_all.reshape(8, n_layers, N_MOD * d)
    cols_ada = w_ada.shape[2]
    d_mod_mine = lax.dynamic_slice_in_dim(d_mod_all, chip * cols_ada, cols_ada, axis=2)
    dmod16 = jnp.concatenate([d_mod_mine, jnp.zeros_like(d_mod_mine)], axis=0)
    dmod16 = jnp.transpose(dmod16, (1, 0, 2)).astype(BF16)
    ada = _adamw_ada("adamw_ada", c16, dmod16, w_ada, m_w_ada, v_w_ada)

    def small_step(name, g, w, m, v):
        shp = w.shape
        g2, w2, m2, v2 = (a.reshape(-1, shp[-1]) for a in (g, w, m, v))
        return (g,) + tuple(a.reshape(shp) for a in _adamw_small(name, g2, w2, m2, v2))

    res = {
        "w_ada": ada,
        "b_ada": small_step("adamw_b_ada", g_mod, b_ada, m_b_ada, v_b_ada),
        "sink": small_step("adamw_sink", g_sink, sink, m_sink, v_sink),
        "w_dw": small_step("adamw_w_dw", g_dw, w_dw, m_w_dw, v_w_dw),
        "conv_ln_g": small_step("adamw_conv_ln_g", g_small["conv_ln_g"], conv_ln_g, m_conv_ln_g, v_conv_ln_g),
        "conv_ln_b": small_step("adamw_conv_ln_b", g_small["conv_ln_b"], conv_ln_b, m_conv_ln_b, v_conv_ln_b),
        "ln1_g": small_step("adamw_ln1_g", g_small["ln1_g"], ln1_g, m_ln1_g, v_ln1_g),
        "ln1_b": small_step("adamw_ln1_b", g_small["ln1_b"], ln1_b, m_ln1_b, v_ln1_b),
        "ln2_g": small_step("adamw_ln2_g", g_small["ln2_g"], ln2_g, m_ln2_g, v_ln2_g),
        "ln2_b": small_step("adamw_ln2_b", g_small["ln2_b"], ln2_b, m_ln2_b, v_ln2_b),
        "w_in": big[0], "w_gu": big[1], "w_oa": big[2], "w_ob": big[3], "w_out": big[4], "w_down": big[5],
    }
    names = ("w_ada", "b_ada", "w_in", "sink", "w_dw", "conv_ln_g", "conv_ln_b", "w_oa", "w_ob", "w_out", "ln1_g", "ln1_b",
             "w_gu", "w_down", "ln2_g", "ln2_b")
    outs = [loss, grad_x[None]]
    for field in range(4):
        outs += [res[k][field] for k in names]
    return tuple(outs)
```

```python
import functools
import math

import jax
import jax.numpy as jnp
from jax import lax
from jax.experimental import pallas as pl
from jax.experimental.pallas import tpu as pltpu

F32 = jnp.float32
BF16 = jnp.bfloat16
MESH = pl.DeviceIdType.MESH

HEAD_DIM = 128
GQA_GROUP = 4
WINDOW = 128
BLOCK = 128
BAND = 3 * BLOCK
ROPE_DIM = HEAD_DIM // 4
ROPE_THETA = 500000.0
CONV_WIDTH = 31
CONV_PAD = CONV_WIDTH // 2
CONV_HALO = 16
N_MOD = 6
LN_EPS = 1e-5
NEG_INF = -1e30
ADAM_LR = 0.001
ADAM_B1 = 0.9
ADAM_B2 = 0.999
ADAM_EPS = 1e-08
ADAM_WD = 0.01
ADAM_STEP = 10

LANE = 128
V7X_VMEM_LIMIT = 56 * 1024 * 1024
ROW_TILE = 256
CONV_ROWS = 32
CONV_LANES = 256

HBM_SPEC = pl.BlockSpec(memory_space=pltpu.HBM)


def _params(*sem):
    return pltpu.CompilerParams(dimension_semantics=sem, vmem_limit_bytes=V7X_VMEM_LIMIT)


def _pick(n, cands, even=False):
    for t in cands:
        if n % t == 0 and (not even or (n // t) % 2 == 0):
            return t
    raise ValueError(f"no tile for {n} in {cands}")


BLOCK_BUDGET = 10 * 1024 * 1024


def _rows_within(n_rows, bytes_per_row, cands=(256, 128, 64, 32, 16, 8)):
    fit = [t for t in cands if n_rows % t == 0]
    for t in fit:
        if t * bytes_per_row <= BLOCK_BUDGET:
            return t
    return fit[-1]


def _sigmoid(v):
    return jax.nn.sigmoid(v)


def _const_map(ndim):
    return lambda *_: (0,) * ndim


def _rowwise(name, fn, n_rows, row_ins, vec_ins, row_outs, vec_outs=()):
    per_row = sum(w * a.dtype.itemsize for a, _, w in row_ins) + sum(w * jnp.dtype(dt).itemsize for w, dt in row_outs)
    tr = _rows_within(n_rows, per_row, (ROW_TILE, 128, 64))
    in_specs, args, pieces = [], [], []
    for arr, off, width in row_ins:
        bw = math.gcd(off, width) if off else width
        assert bw % LANE == 0 and arr.shape[0] == n_rows
        pieces.append(width // bw)
        for p in range(width // bw):
            in_specs.append(pl.BlockSpec((tr, bw), functools.partial(lambda i, blk: (i, blk), blk=off // bw + p)))
            args.append(arr)
    for v in vec_ins:
        in_specs.append(pl.BlockSpec(v.shape, _const_map(v.ndim)))
        args.append(v)
    out_shape = [jax.ShapeDtypeStruct((n_rows, w), dt) for w, dt in row_outs]
    out_specs = [pl.BlockSpec((tr, w), lambda i: (i, 0)) for w, _ in row_outs]
    for shp, dt in vec_outs:
        out_shape.append(jax.ShapeDtypeStruct(shp, dt))
        out_specs.append(pl.BlockSpec(shp, _const_map(len(shp))))
    n_in, n_row_out = len(args), len(row_outs)

    def body(*refs):
        in_refs, out_refs = refs[:n_in], refs[n_in:]
        vals, k = [], 0
        for npc in pieces:
            ps = [in_refs[k + p][...] for p in range(npc)]
            k += npc
            vals.append(ps[0] if npc == 1 else jnp.concatenate(ps, axis=1))
        for _ in vec_ins:
            vals.append(in_refs[k][...])
            k += 1
        outs = fn(*vals)
        if not isinstance(outs, (tuple, list)):
            outs = (outs,)
        assert len(outs) == len(out_refs)
        for j in range(n_row_out):
            out_refs[j][...] = outs[j].astype(out_refs[j].dtype)
        if vec_outs:
            @pl.when(pl.program_id(0) == 0)
            def _():
                for j in range(n_row_out, len(out_refs)):
                    out_refs[j][...] = jnp.zeros(out_refs[j].shape, out_refs[j].dtype)
            for j in range(n_row_out, len(out_refs)):
                out_refs[j][...] += outs[j].astype(out_refs[j].dtype)

    res = pl.pallas_call(
        body, name=name, grid=(n_rows // tr,), in_specs=in_specs, out_specs=out_specs, out_shape=out_shape,
        compiler_params=_params("arbitrary"),
    )(*args)
    return res[0] if len(res) == 1 else res


def _mm(name, a, b, mode, out_dtype, b_layer=None, split=None, c_idx=None, jobs=()):
    bshape = b.shape[1:] if b_layer is not None else b.shape
    if mode == "nn":
        (m, k), (k2, n) = a.shape, bshape
        dims = (((1,), (0,)), ((), ()))
    elif mode == "nt":
        (m, k), (n, k2) = a.shape, bshape
        dims = (((1,), (1,)), ((), ()))
    else:
        (k, m), (k2, n) = a.shape, bshape
        dims = (((0,), (0,)), ((), ()))
    assert k == k2, (name, a.shape, b.shape)
    tm = _pick(m, (1024, 512, 256, 128, 16), even=(split == "rows"))
    tn = _pick(n, (1024, 512, 256, 128), even=(split == "cols"))
    tk = _pick(k, (2048, 1408, 1024, 704, 512, 256, 128))
    ni, nj, nk = m // tm, n // tn, k // tk

    if mode == "nn":
        a_spec = pl.BlockSpec((tm, tk), lambda i, j, kk, *_: (i, kk))
        b_blk, b_map = (tk, tn), (lambda i, j, kk: (kk, j))
    elif mode == "nt":
        a_spec = pl.BlockSpec((tm, tk), lambda i, j, kk, *_: (i, kk))
        b_blk, b_map = (tn, tk), (lambda i, j, kk: (j, kk))
    else:
        a_spec = pl.BlockSpec((tk, tm), lambda i, j, kk, *_: (kk, i))
        b_blk, b_map = (tk, tn), (lambda i, j, kk: (kk, j))
    if b_layer is None:
        b_spec = pl.BlockSpec(b_blk, lambda i, j, kk, *_: b_map(i, j, kk))
    else:
        b_spec = pl.BlockSpec((None,) + b_blk, lambda i, j, kk, *_: (b_layer,) + b_map(i, j, kk))

    if split is None:
        out_shape = jax.ShapeDtypeStruct((m, n), out_dtype)
        o_spec = pl.BlockSpec((tm, tn), lambda i, j, kk, *_: (i, j))
    elif split == "rows":
        out_shape = jax.ShapeDtypeStruct((2, m // 2, n), out_dtype)
        o_spec = pl.BlockSpec(
            (None, tm, tn), lambda i, j, kk, c_ref: (jnp.where(i // (ni // 2) == c_ref[0], 0, 1), i % (ni // 2), j))
    else:
        out_shape = jax.ShapeDtypeStruct((2, m, n // 2), out_dtype)
        o_spec = pl.BlockSpec(
            (None, tm, tn), lambda i, j, kk, c_ref: (jnp.where(j // (nj // 2) == c_ref[0], 0, 1), i, j % (nj // 2)))

    n_job_in = sum(len(jb.ins) for jb in jobs)
    n_job_out = sum(len(jb.out_shapes) for jb in jobs)
    n_acc = 0 if nk == 1 else 1

    def body(*refs):
        if split is not None:
            refs = refs[1:]
        a_ref, b_ref = refs[:2]
        job_ins = refs[2:2 + n_job_in]
        o_ref = refs[2 + n_job_in]
        job_outs = refs[3 + n_job_in:3 + n_job_in + n_job_out]
        scratch_refs = refs[3 + n_job_in + n_job_out:]
        cut = _job_refs(jobs, job_ins, job_outs, scratch_refs[n_acc:])
        i, j, kk = pl.program_id(0), pl.program_id(1), pl.program_id(2)

        if jobs:
            @pl.when((i == 0) & (j == 0) & (kk == 0))
            def _():
                for jb, parts in zip(jobs, cut):
                    jb.start(*parts)

        part = lax.dot_general(a_ref[...].astype(BF16), b_ref[...].astype(BF16), dims, preferred_element_type=F32)
        if nk == 1:
            o_ref[...] = part.astype(o_ref.dtype)
        else:
            acc_ref = scratch_refs[0]

            @pl.when(kk == 0)
            def _():
                acc_ref[...] = part

            @pl.when(kk > 0)
            def _():
                acc_ref[...] += part

            @pl.when(kk == nk - 1)
            def _():
                o_ref[...] = acc_ref[...].astype(o_ref.dtype)

        if jobs:
            @pl.when((i == ni - 1) & (j == nj - 1) & (kk == nk - 1))
            def _():
                for jb, parts in zip(jobs, cut):
                    jb.finish(*parts)

    scratch = ([] if nk == 1 else [pltpu.VMEM((tm, tn), F32)]) + [s for jb in jobs for s in jb.sems]
    params = _params(*(["arbitrary"] * 3 if jobs else ["parallel", "parallel", "arbitrary"]))
    in_specs = [a_spec, b_spec] + [HBM_SPEC] * n_job_in
    out_specs = [o_spec] + [HBM_SPEC] * n_job_out
    out_shapes = [out_shape] + [s for jb in jobs for s in jb.out_shapes]
    operands = [a, b] + [x for jb in jobs for x in jb.ins]
    n_pre = 0 if split is None else 1
    aliases = _job_aliases(jobs, n_pre + 2, 1)
    if split is None:
        res = pl.pallas_call(
            body, name=name, grid=(ni, nj, nk), in_specs=in_specs, out_specs=out_specs, out_shape=out_shapes,
            scratch_shapes=scratch, input_output_aliases=aliases, compiler_params=params,
        )(*operands)
    else:
        grid_spec = pltpu.PrefetchScalarGridSpec(
            num_scalar_prefetch=1, grid=(ni, nj, nk), in_specs=in_specs, out_specs=out_specs, scratch_shapes=scratch)
        res = pl.pallas_call(body, name=name, grid_spec=grid_spec, out_shape=out_shapes, input_output_aliases=aliases,
                             compiler_params=params)(c_idx, *operands)
    if not jobs:
        return res[0]
    return res[0], _job_results(jobs, res[1:])


def _attn_tile(seq):
    return _pick(seq, (256, 128))


def _heads_stacked(ref, b):
    return jnp.concatenate(
        [ref[b * BLOCK:(b + 1) * BLOCK, g * HEAD_DIM:(g + 1) * HEAD_DIM] for g in range(GQA_GROUP)], axis=0)


def _attn_scores(q_ref, k_ref, v_ref, sink_ref, kvh, i, b, tq, seq):
    rows = GQA_GROUP * BLOCK
    q0 = i * tq + b * BLOCK
    k_off = pl.multiple_of(jnp.clip(q0 - BLOCK, 0, seq - BAND), BLOCK)
    kw = k_ref[pl.ds(k_off, BAND), :]
    vw = v_ref[pl.ds(k_off, BAND), :]
    q_pos = q0 + (lax.broadcasted_iota(jnp.int32, (rows, BAND), 0) & (BLOCK - 1))
    k_pos = k_off + lax.broadcasted_iota(jnp.int32, (rows, BAND), 1)
    valid = jnp.abs(k_pos - q_pos) <= WINDOW
    qs = _heads_stacked(q_ref, b)
    s = lax.dot_general(qs, kw, (((1,), (1,)), ((), ())), preferred_element_type=F32) * (HEAD_DIM ** -0.5)
    s = jnp.where(valid, s, NEG_INF)
    sink = jnp.concatenate(
        [jnp.broadcast_to(sink_ref[pl.ds(kvh * GQA_GROUP + g, 1), :][:, :1], (BLOCK, 1)) for g in range(GQA_GROUP)], axis=0)
    m = jnp.maximum(jnp.max(s, axis=-1, keepdims=True), sink)
    p = jnp.exp(s - m)
    p_sink = jnp.exp(sink - m)
    denom = jnp.sum(p, axis=-1, keepdims=True) + p_sink
    return k_off, kw, vw, qs, p / denom, p_sink / denom


def _attn_fwd(name, qr, kr, vb, sink_b):
    seq, dq = qr.shape
    nkv = kr.shape[1] // HEAD_DIM
    tq = _attn_tile(seq)
    gw = GQA_GROUP * HEAD_DIM

    def body(q_ref, k_ref, v_ref, sink_ref, o_ref):
        kvh, i = pl.program_id(0), pl.program_id(1)
        for b in range(tq // BLOCK):
            _, _, vw, _, pn, _ = _attn_scores(q_ref, k_ref, v_ref, sink_ref, kvh, i, b, tq, seq)
            o = jnp.dot(pn.astype(BF16), vw, preferred_element_type=F32).astype(o_ref.dtype)
            for g in range(GQA_GROUP):
                o_ref[b * BLOCK:(b + 1) * BLOCK, g * HEAD_DIM:(g + 1) * HEAD_DIM] = o[g * BLOCK:(g + 1) * BLOCK]

    return pl.pallas_call(
        body, name=name, grid=(nkv, seq // tq),
        in_specs=[
            pl.BlockSpec((tq, gw), lambda h, i: (i, h)),
            pl.BlockSpec((seq, HEAD_DIM), lambda h, i: (0, h)),
            pl.BlockSpec((seq, HEAD_DIM), lambda h, i: (0, h)),
            pl.BlockSpec(sink_b.shape, lambda h, i: (0, 0)),
        ],
        out_specs=pl.BlockSpec((tq, gw), lambda h, i: (i, h)),
        out_shape=jax.ShapeDtypeStruct((seq, dq), BF16),
        compiler_params=_params("arbitrary", "arbitrary"),
    )(qr, kr, vb, sink_b)


def _attn_bwd(name, qr, kr, vb, sink_b, d_att):
    seq, dq = qr.shape
    dkv = kr.shape[1]
    nkv = dkv // HEAD_DIM
    tq = _attn_tile(seq)
    gw = GQA_GROUP * HEAD_DIM
    tn_dims = (((0,), (0,)), ((), ()))

    def body(q_ref, k_ref, v_ref, sink_ref, do_ref, dq_ref, dk_ref, dv_ref, dsink_ref):
        kvh, i = pl.program_id(0), pl.program_id(1)

        @pl.when(i == 0)
        def _():
            dk_ref[...] = jnp.zeros(dk_ref.shape, F32)
            dv_ref[...] = jnp.zeros(dv_ref.shape, F32)

        @pl.when((i == 0) & (kvh == 0))
        def _():
            dsink_ref[...] = jnp.zeros(dsink_ref.shape, F32)

        for b in range(tq // BLOCK):
            k_off, kw, vw, qs, pn, pn_sink = _attn_scores(q_ref, k_ref, v_ref, sink_ref, kvh, i, b, tq, seq)
            dos = _heads_stacked(do_ref, b)
            dp = lax.dot_general(dos, vw, (((1,), (1,)), ((), ())), preferred_element_type=F32)
            delta = jnp.sum(pn * dp, axis=-1, keepdims=True)
            ds = (pn * (dp - delta) * (HEAD_DIM ** -0.5)).astype(BF16)
            dqs = jnp.dot(ds, kw, preferred_element_type=F32)
            sink_term = pn_sink * delta
            for g in range(GQA_GROUP):
                dq_ref[b * BLOCK:(b + 1) * BLOCK, g * HEAD_DIM:(g + 1) * HEAD_DIM] = dqs[g * BLOCK:(g + 1) * BLOCK]
                d_sink = -jnp.sum(sink_term[g * BLOCK:(g + 1) * BLOCK], axis=0, keepdims=True)
                dsink_ref[pl.ds(kvh * GQA_GROUP + g, 1), :] += jnp.broadcast_to(d_sink, (1, LANE))
            dk_ref[pl.ds(k_off, BAND), :] += lax.dot_general(ds, qs, tn_dims, preferred_element_type=F32)
            dv_ref[pl.ds(k_off, BAND), :] += lax.dot_general(pn.astype(BF16), dos, tn_dims, preferred_element_type=F32)

    return pl.pallas_call(
        body, name=name, grid=(nkv, seq // tq),
        in_specs=[
            pl.BlockSpec((tq, gw), lambda h, i: (i, h)),
            pl.BlockSpec((seq, HEAD_DIM), lambda h, i: (0, h)),
            pl.BlockSpec((seq, HEAD_DIM), lambda h, i: (0, h)),
            pl.BlockSpec(sink_b.shape, lambda h, i: (0, 0)),
            pl.BlockSpec((tq, gw), lambda h, i: (i, h)),
        ],
        out_specs=[
            pl.BlockSpec((tq, gw), lambda h, i: (i, h)),
            pl.BlockSpec((seq, HEAD_DIM), lambda h, i: (0, h)),
            pl.BlockSpec((seq, HEAD_DIM), lambda h, i: (0, h)),
            pl.BlockSpec(sink_b.shape, lambda h, i: (0, 0)),
        ],
        out_shape=[
            jax.ShapeDtypeStruct((seq, dq), F32),
            jax.ShapeDtypeStruct((seq, dkv), F32),
            jax.ShapeDtypeStruct((seq, dkv), F32),
            jax.ShapeDtypeStruct(sink_b.shape, F32),
        ],
        compiler_params=_params("arbitrary", "arbitrary"),
    )(qr, kr, vb, sink_b, d_att)


def _halo_specs(tr, width, n_rows):
    per, last = tr // CONV_HALO, n_rows // CONV_HALO - 1
    return [
        pl.BlockSpec((tr, width), lambda i: (i, 0)),
        pl.BlockSpec((CONV_HALO, width), lambda i: (jnp.maximum(i * per - 1, 0), 0)),
        pl.BlockSpec((CONV_HALO, width), lambda i: (jnp.minimum((i + 1) * per, last), 0)),
    ]


def _fill_ext(ext_ref, main_ref, prev_ref, next_ref, n_steps, tr):
    i = pl.program_id(0)
    ext_ref[0:CONV_HALO, :] = jnp.where(i > 0, prev_ref[...], 0.0)
    ext_ref[CONV_HALO:CONV_HALO + tr, :] = main_ref[...]
    ext_ref[CONV_HALO + tr:, :] = jnp.where(i < n_steps - 1, next_ref[...], 0.0)


def _conv_taps(ext_ref, w_ref, out_ref, tr, width, flip):
    cw = min(CONV_LANES, width)
    for cc in range(width // cw):
        cols = slice(cc * cw, (cc + 1) * cw)
        for rc in range(tr // CONV_ROWS):
            acc = jnp.zeros((CONV_ROWS, cw), F32)
            for t in range(CONV_WIDTH):
                wt = CONV_WIDTH - 1 - t if flip else t
                acc += ext_ref[rc * CONV_ROWS + 1 + t:rc * CONV_ROWS + 1 + t + CONV_ROWS, cols] * w_ref[wt:wt + 1, cols]
            out_ref[rc * CONV_ROWS:(rc + 1) * CONV_ROWS, cols] = acc


def _ln(v, g, b):
    mu = jnp.mean(v, axis=-1, keepdims=True)
    vc = v - mu
    var = jnp.mean(vc * vc, axis=-1, keepdims=True)
    return vc * lax.rsqrt(var + LN_EPS) * g + b


def _conv_fwd(name, u, w32, ln_g, ln_b):
    n_rows, width = u.shape
    tr = min(ROW_TILE, n_rows)
    n_steps = n_rows // tr

    def body(main_ref, prev_ref, next_ref, w_ref, g_ref, b_ref, u2_ref, cv_ref, ext_ref):
        _fill_ext(ext_ref, main_ref, prev_ref, next_ref, n_steps, tr)
        _conv_taps(ext_ref, w_ref, u2_ref, tr, width, flip=False)
        u3 = _ln(u2_ref[...], g_ref[...], b_ref[...])
        cv_ref[...] = (u3 * _sigmoid(u3)).astype(cv_ref.dtype)

    vec = lambda a: pl.BlockSpec(a.shape, lambda i: (0, 0))
    return pl.pallas_call(
        body, name=name, grid=(n_steps,),
        in_specs=_halo_specs(tr, width, n_rows) + [vec(w32), vec(ln_g), vec(ln_b)],
        out_specs=[pl.BlockSpec((tr, width), lambda i: (i, 0))] * 2,
        out_shape=[jax.ShapeDtypeStruct((n_rows, width), F32), jax.ShapeDtypeStruct((n_rows, width), BF16)],
        scratch_shapes=[pltpu.VMEM((tr + 2 * CONV_HALO, width), F32)],
        compiler_params=_params("arbitrary"),
    )(u, u, u, w32, ln_g, ln_b)


def _conv_bwd_a(name, u, u2, d_cv, ln_g, ln_b):
    n_rows, width = u.shape
    tr = min(ROW_TILE, n_rows)
    n_steps = n_rows // tr
    cw = min(CONV_LANES, width)

    def body(main_ref, prev_ref, next_ref, u2_ref, dcv_ref, g_ref, b_ref, du2_ref, dw_ref, dg_ref, db_ref, ext_ref):
        @pl.when(pl.program_id(0) == 0)
        def _():
            dw_ref[...] = jnp.zeros(dw_ref.shape, F32)
            dg_ref[...] = jnp.zeros(dg_ref.shape, F32)
            db_ref[...] = jnp.zeros(db_ref.shape, F32)

        _fill_ext(ext_ref, main_ref, prev_ref, next_ref, n_steps, tr)

        def swish_ln(v, g, b):
            u3 = _ln(v, g, b)
            return u3 * _sigmoid(u3)

        _, vjp = jax.vjp(swish_ln, u2_ref[...], g_ref[...], b_ref[...])
        du2, dg, db = vjp(dcv_ref[...])
        du2_ref[...] = du2
        dg_ref[...] += dg
        db_ref[...] += db
        for cc in range(width // cw):
            cols = slice(cc * cw, (cc + 1) * cw)
            for t in range(CONV_WIDTH):
                acc = jnp.zeros((CONV_ROWS, cw), F32)
                for rc in range(tr // CONV_ROWS):
                    r0 = rc * CONV_ROWS
                    acc += du2_ref[r0:r0 + CONV_ROWS, cols] * ext_ref[r0 + 1 + t:r0 + 1 + t + CONV_ROWS, cols]
                dw_ref[t:t + 1, cols] += jnp.sum(acc, axis=0, keepdims=True)

    vec = lambda a: pl.BlockSpec(a.shape, lambda i: (0, 0))
    row = pl.BlockSpec((tr, width), lambda i: (i, 0))
    return pl.pallas_call(
        body, name=name, grid=(n_steps,),
        in_specs=_halo_specs(tr, width, n_rows) + [row, row, vec(ln_g), vec(ln_b)],
        out_specs=[row, pl.BlockSpec((32, width), lambda i: (0, 0)), vec(ln_g), vec(ln_b)],
        out_shape=[jax.ShapeDtypeStruct((n_rows, width), F32), jax.ShapeDtypeStruct((32, width), F32),
                   jax.ShapeDtypeStruct(ln_g.shape, F32), jax.ShapeDtypeStruct(ln_b.shape, F32)],
        scratch_shapes=[pltpu.VMEM((tr + 2 * CONV_HALO, width), F32)],
        compiler_params=_params("arbitrary"),
    )(u, u, u, u2, d_cv, ln_g, ln_b)


def _conv_bwd_b(name, du2, z, off_a, off_b, w32):
    n_rows, width = du2.shape
    tr = min(ROW_TILE, n_rows)
    n_steps = n_rows // tr
    bw = math.gcd(math.gcd(off_a, off_b), width)
    npc = width // bw

    def body(*refs):
        main_ref, prev_ref, next_ref = refs[:3]
        a_refs, b_refs = refs[3:3 + npc], refs[3 + npc:3 + 2 * npc]
        w_ref, out_ref, ext_ref, du_ref = refs[3 + 2 * npc:]
        _fill_ext(ext_ref, main_ref, prev_ref, next_ref, n_steps, tr)
        _conv_taps(ext_ref, w_ref, du_ref, tr, width, flip=True)
        for p in range(npc):
            cols = slice(p * bw, (p + 1) * bw)
            du = du_ref[:, cols]
            sg = _sigmoid(b_refs[p][...])
            out_ref[:, p * bw:(p + 1) * bw] = (du * sg).astype(out_ref.dtype)
            out_ref[:, width + p * bw:width + (p + 1) * bw] = (du * a_refs[p][...] * sg * (1.0 - sg)).astype(out_ref.dtype)

    def piece(off, p):
        return pl.BlockSpec((tr, bw), functools.partial(lambda i, blk: (i, blk), blk=off // bw + p))

    in_specs = _halo_specs(tr, width, n_rows)
    in_specs += [piece(off_a, p) for p in range(npc)] + [piece(off_b, p) for p in range(npc)]
    in_specs.append(pl.BlockSpec(w32.shape, lambda i: (0, 0)))
    return pl.pallas_call(
        body, name=name, grid=(n_steps,), in_specs=in_specs,
        out_specs=pl.BlockSpec((tr, 2 * width), lambda i: (i, 0)),
        out_shape=jax.ShapeDtypeStruct((n_rows, 2 * width), BF16),
        scratch_shapes=[pltpu.VMEM((tr + 2 * CONV_HALO, width), F32), pltpu.VMEM((tr, width), F32)],
        compiler_params=_params("arbitrary"),
    )(du2, du2, du2, *([z] * (2 * npc)), w32)


def _place():
    return lax.axis_index("x"), lax.axis_index("y"), lax.axis_index("c")


def _flip(v, m):
    return 1 - v if m else v


def _gather_small(name, v, masks):
    varies = [any(m[a] for m in masks) for a in range(3)]
    n = len(masks) + 1

    def slot(pos):
        idx = 0
        for a in range(3):
            if varies[a]:
                idx = idx * 2 + pos[a]
        return idx

    def body(v_ref, o_ref, send_sems, recv_sems, local_sem):
        me = _place()
        mine = pltpu.make_async_copy(v_ref, o_ref.at[slot(me)], local_sem)
        mine.start()
        peers = [tuple(_flip(me[a], m[a]) for a in range(3)) for m in masks]
        sends = [pltpu.make_async_remote_copy(v_ref, o_ref.at[slot(me)], send_sems.at[k], recv_sems.at[k],
                                              device_id=peer, device_id_type=MESH) for k, peer in enumerate(peers)]
        for cp in sends:
            cp.start()
        for k, peer in enumerate(peers):
            pltpu.make_async_remote_copy(v_ref, o_ref.at[slot(peer)], send_sems.at[k], recv_sems.at[k],
                                         device_id=peer, device_id_type=MESH).wait_recv()
        for cp in sends:
            cp.wait_send()
        mine.wait()

    return pl.pallas_call(
        body, name=name, in_specs=[HBM_SPEC], out_specs=HBM_SPEC,
        out_shape=jax.ShapeDtypeStruct((n,) + v.shape, v.dtype),
        scratch_shapes=[pltpu.SemaphoreType.DMA((n - 1,)), pltpu.SemaphoreType.DMA((n - 1,)), pltpu.SemaphoreType.DMA(())],
    )(v)


ALL_DEVICES = [(mx, my, mc) for mx in (0, 1) for my in (0, 1) for mc in (0, 1)][1:]
SAME_CORE_CHIPS = [(1, 0, 0), (0, 1, 0), (1, 1, 0)]


def _chips(x, y):
    return [(1 - x, y), (x, 1 - y), (1 - x, 1 - y)]


def _cast_into(name, w, layer, kind, chip_idx):
    _, r, cc = w.shape
    tr = _rows_within(r, cc * 6)
    steps = r // tr
    if kind == "col":
        shape, o_spec = (r, 4 * cc), pl.BlockSpec((tr, cc), lambda i, s_ref: (i, s_ref[0]))
    else:
        shape, o_spec = (4 * r, cc), pl.BlockSpec((tr, cc), lambda i, s_ref: (s_ref[0] * steps + i, 0))

    def body(s_ref, w_ref, o_ref):
        o_ref[...] = w_ref[...].astype(o_ref.dtype)

    grid_spec = pltpu.PrefetchScalarGridSpec(
        num_scalar_prefetch=1, grid=(steps,),
        in_specs=[pl.BlockSpec((None, tr, cc), lambda i, s_ref: (layer, i, 0))], out_specs=o_spec)
    return pl.pallas_call(body, name=name, grid_spec=grid_spec, out_shape=jax.ShapeDtypeStruct(shape, BF16),
                          compiler_params=_params("arbitrary"))(chip_idx, w)


class _Job:
    def __init__(self, ins, out_shapes, aliases, sems, start, finish):
        self.ins, self.out_shapes, self.aliases, self.sems = list(ins), list(out_shapes), dict(aliases), list(sems)
        self.start, self.finish = start, finish


def _job_refs(jobs, in_refs, out_refs, sem_refs):
    cut, i, o, s = [], 0, 0, 0
    for jb in jobs:
        cut.append((in_refs[i:i + len(jb.ins)], out_refs[o:o + len(jb.out_shapes)], sem_refs[s:s + len(jb.sems)]))
        i, o, s = i + len(jb.ins), o + len(jb.out_shapes), s + len(jb.sems)
    return cut


def _job_aliases(jobs, first_in, first_out):
    aliases, i, o = {}, first_in, first_out
    for jb in jobs:
        for a, b in jb.aliases.items():
            aliases[i + a] = o + b
        i, o = i + len(jb.ins), o + len(jb.out_shapes)
    return aliases


def _run_jobs(name, jobs):
    n_in = sum(len(jb.ins) for jb in jobs)
    n_out = sum(len(jb.out_shapes) for jb in jobs)

    def body(*refs):
        cut = _job_refs(jobs, refs[:n_in], refs[n_in:n_in + n_out], refs[n_in + n_out:])
        for jb, parts in zip(jobs, cut):
            jb.start(*parts)
        for jb, parts in zip(jobs, cut):
            jb.finish(*parts)

    res = pl.pallas_call(
        body, name=name, in_specs=[HBM_SPEC] * n_in, out_specs=[HBM_SPEC] * n_out,
        out_shape=[s for jb in jobs for s in jb.out_shapes], input_output_aliases=_job_aliases(jobs, 0, 0),
        scratch_shapes=[s for jb in jobs for s in jb.sems],
    )(*[a for jb in jobs for a in jb.ins])
    return _job_results(jobs, res)


def _job_results(jobs, flat):
    out, o = [], 0
    for jb in jobs:
        out.append(list(flat[o:o + len(jb.out_shapes)]))
        o += len(jb.out_shapes)
    return out


def _job_gather(fulls, shapes, kinds):
    n = len(fulls)
    for r, _ in shapes:
        assert r % 32 == 0

    def window(o_ref, j, s, h):
        r, cc = shapes[j]
        hr = r // 2
        if kinds[j] == "col":
            return o_ref.at[pl.ds(pl.multiple_of(h * hr, 16), hr), pl.ds(pl.multiple_of(s * cc, LANE), cc)]
        return o_ref.at[pl.ds(pl.multiple_of(s * r + h * hr, 16), hr), :]

    def first_copies(outs, sems):
        x, y, c = _place()
        cps = []
        for j in range(n):
            mine = window(outs[j], j, 2 * x + y, c)
            for k, chip in enumerate(_chips(x, y)):
                cps.append(pltpu.make_async_remote_copy(mine, mine, sems[0].at[3 * j + k], sems[1].at[3 * j + k],
                                                        device_id=(*chip, c), device_id_type=MESH))
        return cps

    def start(ins, outs, sems):
        for cp in first_copies(outs, sems):
            cp.start()

    def finish(ins, outs, sems):
        x, y, c = _place()
        chips = _chips(x, y)
        sibling = (x, y, 1 - c)
        passed = []
        for j in range(n):
            for k, chip in enumerate(chips):
                win = window(outs[j], j, 2 * chip[0] + chip[1], c)
                pltpu.make_async_remote_copy(win, win, sems[0].at[3 * j + k], sems[1].at[3 * j + k],
                                             device_id=(*chip, c), device_id_type=MESH).wait_recv()
                cp = pltpu.make_async_remote_copy(win, win, sems[2].at[3 * j + k], sems[3].at[3 * j + k],
                                                  device_id=sibling, device_id_type=MESH)
                cp.start()
                passed.append(cp)
        for j in range(n):
            for k, chip in enumerate(chips):
                win = window(outs[j], j, 2 * chip[0] + chip[1], 1 - c)
                pltpu.make_async_remote_copy(win, win, sems[2].at[3 * j + k], sems[3].at[3 * j + k],
                                             device_id=sibling, device_id_type=MESH).wait_recv()
        for cp in first_copies(outs, sems) + passed:
            cp.wait_send()

    return _Job(fulls, [jax.ShapeDtypeStruct(f.shape, f.dtype) for f in fulls], {j: j for j in range(n)},
                [pltpu.SemaphoreType.DMA((3 * n,)) for _ in range(4)], start, finish)


def _job_pair(grads):
    n = len(grads)

    def copies(ins, outs, sems):
        x, y, c = _place()
        return [pltpu.make_async_remote_copy(ins[j].at[1], outs[j], sems[0].at[j], sems[1].at[j],
                                             device_id=(x, y, 1 - c), device_id_type=MESH) for j in range(n)]

    def start(ins, outs, sems):
        for cp in copies(ins, outs, sems):
            cp.start()

    def finish(ins, outs, sems):
        for cp in copies(ins, outs, sems):
            cp.wait()

    return _Job(grads, [jax.ShapeDtypeStruct(g.shape[1:], g.dtype) for g in grads], {},
                [pltpu.SemaphoreType.DMA((n,)), pltpu.SemaphoreType.DMA((n,))], start, finish)


def _job_chips(halves, kinds):
    n = len(halves)
    shapes = [(h.shape[0], h.shape[1] // 4) if kinds[j] == "col" else (h.shape[0] // 4, h.shape[1])
              for j, h in enumerate(halves)]

    def part(ref, j, s):
        r, cc = shapes[j]
        if kinds[j] == "col":
            return ref.at[:, pl.ds(pl.multiple_of(s * cc, LANE), cc)]
        return ref.at[pl.ds(pl.multiple_of(s * r, 16), r), :]

    def copies(ins, outs, sems):
        x, y, c = _place()
        s_me = 2 * x + y
        local = [pltpu.make_async_copy(part(ins[j], j, s_me), outs[j].at[s_me], sems[2].at[j]) for j in range(n)]
        sends, recvs = [], []
        for j in range(n):
            for k, chip in enumerate(_chips(x, y)):
                s_peer = 2 * chip[0] + chip[1]
                sends.append(pltpu.make_async_remote_copy(part(ins[j], j, s_peer), outs[j].at[s_me],
                                                          sems[0].at[3 * j + k], sems[1].at[3 * j + k],
                                                          device_id=(*chip, c), device_id_type=MESH))
                dst = outs[j].at[s_peer]
                recvs.append(pltpu.make_async_remote_copy(dst, dst, sems[0].at[3 * j + k], sems[1].at[3 * j + k],
                                                          device_id=(*chip, c), device_id_type=MESH))
        return local, sends, recvs

    def start(ins, outs, sems):
        local, sends, _ = copies(ins, outs, sems)
        for cp in local + sends:
            cp.start()

    def finish(ins, outs, sems):
        local, sends, recvs = copies(ins, outs, sems)
        for cp in recvs:
            cp.wait_recv()
        for cp in sends:
            cp.wait_send()
        for cp in local:
            cp.wait()

    return _Job(halves, [jax.ShapeDtypeStruct((4,) + shapes[j], halves[j].dtype) for j in range(n)], {},
                [pltpu.SemaphoreType.DMA((3 * n,)), pltpu.SemaphoreType.DMA((3 * n,)), pltpu.SemaphoreType.DMA((n,))],
                start, finish)


def _job_join(pairs):
    n = len(pairs)

    def copies(outs, sems):
        x, y, c = _place()
        sends, recvs = [], []
        for j in range(n):
            sends.append(pltpu.make_async_remote_copy(outs[j].at[c], outs[j].at[c], sems[0].at[j], sems[1].at[j],
                                                      device_id=(x, y, 1 - c), device_id_type=MESH))
            theirs = outs[j].at[1 - c]
            recvs.append(pltpu.make_async_remote_copy(theirs, theirs, sems[0].at[j], sems[1].at[j],
                                                      device_id=(x, y, 1 - c), device_id_type=MESH))
        return sends, recvs

    def start(ins, outs, sems):
        for cp in copies(outs, sems)[0]:
            cp.start()

    def finish(ins, outs, sems):
        sends, recvs = copies(outs, sems)
        for cp in recvs:
            cp.wait_recv()
        for cp in sends:
            cp.wait_send()

    return _Job(pairs, [jax.ShapeDtypeStruct(p.shape, p.dtype) for p in pairs], {j: j for j in range(n)},
                [pltpu.SemaphoreType.DMA((n,)), pltpu.SemaphoreType.DMA((n,))], start, finish)


def _pair_sum(name, mine_other, got):
    _, r, cc = mine_other.shape
    tr = _rows_within(r, 3 * cc * mine_other.dtype.itemsize, (256, 128, 64, 32, 16))

    def body(a_ref, b_ref, o_ref):
        o_ref[...] = (a_ref[...].astype(F32) + b_ref[...].astype(F32)).astype(o_ref.dtype)

    return pl.pallas_call(
        body, name=name, grid=(r // tr,),
        in_specs=[pl.BlockSpec((None, tr, cc), lambda i: (0, i, 0)), pl.BlockSpec((tr, cc), lambda i: (i, 0))],
        out_specs=pl.BlockSpec((tr, cc), lambda i: (i, 0)),
        out_shape=jax.ShapeDtypeStruct((r, cc), mine_other.dtype), compiler_params=_params("parallel"),
    )(mine_other, got)


def _sum_slots(name, parts, into_slot=None):
    n, r, cc = parts.shape
    tr = _rows_within(r, cc * (n * parts.dtype.itemsize + 4), (256, 128, 64, 32, 16, 8))

    def body(*refs):
        p_ref, o_ref = refs[-2:]
        acc = p_ref[0].astype(F32)
        for s in range(1, n):
            acc = acc + p_ref[s].astype(F32)
        o_ref[...] = acc

    in_spec = pl.BlockSpec((n, tr, cc), lambda i, *_: (0, i, 0))
    if into_slot is None:
        return pl.pallas_call(
            body, name=name, grid=(r // tr,), in_specs=[in_spec], out_specs=pl.BlockSpec((tr, cc), lambda i: (i, 0)),
            out_shape=jax.ShapeDtypeStruct((r, cc), F32), compiler_params=_params("parallel"),
        )(parts)
    grid_spec = pltpu.PrefetchScalarGridSpec(
        num_scalar_prefetch=1, grid=(r // tr,), in_specs=[in_spec],
        out_specs=pl.BlockSpec((None, tr, cc), lambda i, c_ref: (c_ref[0], i, 0)))
    return pl.pallas_call(body, name=name, grid_spec=grid_spec, out_shape=jax.ShapeDtypeStruct((2, r, cc), F32),
                          compiler_params=_params("arbitrary"))(into_slot, parts)


def _adamw_math(w, g, m, v):
    m = ADAM_B1 * m + (1.0 - ADAM_B1) * g
    v = ADAM_B2 * v + (1.0 - ADAM_B2) * jnp.square(g)
    m_hat = m / (1.0 - ADAM_B1 ** ADAM_STEP)
    v_hat = v / (1.0 - ADAM_B2 ** ADAM_STEP)
    delta = -ADAM_LR * (m_hat / (jnp.sqrt(v_hat) + ADAM_EPS) + ADAM_WD * w)
    return delta, m, v


def _adamw_layer(name, layer, g_pair, kind, w, m, v, prev):
    n_layers, r, cc = w.shape
    if prev is None:
        prev = tuple(lax.empty(w.shape, F32) for _ in range(4))
    if kind == "col":
        g = g_pair.reshape(r, cc)
        tr = _rows_within(r, 8 * cc * 4)
        grid = (r // tr,)
        g_spec = pl.BlockSpec((tr, cc), lambda i: (i, 0))
        blk = pl.BlockSpec((None, tr, cc), lambda i: (layer, i, 0))
    else:
        g = g_pair
        tr = _rows_within(r, 4 * cc * 4)
        grid = (r // tr, 2)
        g_spec = pl.BlockSpec((None, tr, cc // 2), lambda i, h: (h, i, 0))
        blk = pl.BlockSpec((None, tr, cc // 2), lambda i, h: (layer, i, h))

    def body(g_ref, w_ref, m_ref, v_ref, *rest):
        og_ref, od_ref, om_ref, ov_ref = rest[4:]
        gv = g_ref[...]
        delta, m2, v2 = _adamw_math(w_ref[...], gv, m_ref[...], v_ref[...])
        og_ref[...] = gv
        od_ref[...] = delta
        om_ref[...] = m2
        ov_ref[...] = v2

    return pl.pallas_call(
        body, name=name, grid=grid,
        in_specs=[g_spec, blk, blk, blk] + [HBM_SPEC] * 4,
        out_specs=[blk] * 4, out_shape=[jax.ShapeDtypeStruct(w.shape, F32)] * 4,
        input_output_aliases={4: 0, 5: 1, 6: 2, 7: 3}, compiler_params=_params(*(["parallel"] * len(grid))),
    )(g, w, m, v, *prev)


def _adamw_small(name, g, w, m, v):
    def body(g_ref, w_ref, m_ref, v_ref, od_ref, om_ref, ov_ref):
        delta, m2, v2 = _adamw_math(w_ref[...], g_ref[...], m_ref[...], v_ref[...])
        od_ref[...] = delta
        om_ref[...] = m2
        ov_ref[...] = v2

    return pl.pallas_call(body, name=name, out_shape=[jax.ShapeDtypeStruct(w.shape, F32)] * 3)(g, w, m, v)


def _adamw_ada(name, c16, dmod16, w, m, v):
    n_layers, d, cols = w.shape
    tr = _rows_within(d, 7 * cols * 4, (256, 128))
    blk = pl.BlockSpec((None, tr, cols), lambda l, i: (l, i, 0))

    def body(c_ref, dm_ref, w_ref, m_ref, v_ref, og_ref, od_ref, om_ref, ov_ref):
        gv = lax.dot_general(c_ref[...], dm_ref[...], (((0,), (0,)), ((), ())), preferred_element_type=F32)
        delta, m2, v2 = _adamw_math(w_ref[...], gv, m_ref[...], v_ref[...])
        og_ref[...] = gv
        od_ref[...] = delta
        om_ref[...] = m2
        ov_ref[...] = v2

    return pl.pallas_call(
        body, name=name, grid=(n_layers, d // tr),
        in_specs=[pl.BlockSpec((16, tr), lambda l, i: (0, i)), pl.BlockSpec((None, 16, cols), lambda l, i: (l, 0, 0)),
                  blk, blk, blk],
        out_specs=[blk] * 4, out_shape=[jax.ShapeDtypeStruct(w.shape, F32)] * 4,
        compiler_params=_params("parallel", "parallel"),
    )(c16, dmod16, w, m, v)


def kernel(x, c, w_ada, b_ada, w_in, sink, w_dw, conv_ln_g, conv_ln_b, w_oa, w_ob, w_out, ln1_g, ln1_b, w_gu, w_down, ln2_g, ln2_b, loss_target, m_w_ada, m_b_ada, m_w_in, m_sink, m_w_dw, m_conv_ln_g, m_conv_ln_b, m_w_oa, m_w_ob, m_w_out, m_ln1_g, m_ln1_b, m_w_gu, m_w_down, m_ln2_g, m_ln2_b, v_w_ada, v_b_ada, v_w_in, v_sink, v_w_dw, v_conv_ln_g, v_conv_ln_b, v_w_oa, v_w_ob, v_w_out, v_ln1_g, v_ln1_b, v_w_gu, v_w_down, v_ln2_g, v_ln2_b):
    seq, d = x.shape[1], x.shape[2]
    n_layers = w_in.shape[0]
    d_in = 4 * w_in.shape[2]
    d_ff = 4 * w_down.shape[1]
    hq = d // HEAD_DIM
    dkv = (hq // GQA_GROUP) * HEAD_DIM
    off_k, off_v, off_ga, off_gb = d, d + dkv, d + 2 * dkv, 2 * d + 2 * dkv
    off_gta, off_gtb = 3 * d + 2 * dkv, 4 * d + 2 * dkv
    assert d_in == 5 * d + 2 * dkv and seq % ROW_TILE == 0 and seq >= BAND
    alpha = (2.0 * n_layers) ** 0.25

    xi, yi, ci = _place()
    chip = 2 * xi + yi
    batch = 4 * xi + 2 * yi + ci
    c_idx = jnp.reshape(ci, (1,)).astype(jnp.int32)
    x2 = x[0]
    target = loss_target[0]

    c_act = jax.nn.silu(c)
    c_all = _gather_small("gather_c", c_act, ALL_DEVICES).reshape(8, d)
    c16 = jnp.concatenate([c_all, jnp.zeros((8, d), F32)], axis=0).astype(BF16)
    mod_cols = [_mm(f"mod_{l}", c16, w_ada, "nn", F32, b_layer=l) for l in range(n_layers)]
    mod_all = _gather_small("gather_mod", jnp.stack(mod_cols), SAME_CORE_CHIPS)
    mod = lax.dynamic_index_in_dim(mod_all, batch, axis=2, keepdims=False)
    mod = jnp.transpose(mod, (1, 0, 2)).reshape(n_layers, N_MOD * d) + b_ada
    mod = mod.reshape(n_layers, N_MOD, 1, d)
    sh_a, sc_a, gt_a, sh_f, sc_f, gt_f = (mod[:, j] for j in range(N_MOD))

    pos = jnp.arange(seq, dtype=F32)
    inv_freq = ROPE_THETA ** (-jnp.arange(0, ROPE_DIM, 2, dtype=F32) / ROPE_DIM)
    ang = pos[:, None] * inv_freq[None, :]
    cos, sin = jnp.cos(ang), jnp.sin(ang)
    half = ROPE_DIM // 2
    rest = HEAD_DIM - ROPE_DIM
    t_cs = jnp.concatenate([cos, cos, jnp.ones((seq, rest), F32)], axis=1)
    t_up = jnp.concatenate([-sin, jnp.zeros((seq, rest + half), F32)], axis=1)
    t_dn = jnp.concatenate([jnp.zeros((seq, half), F32), sin, jnp.zeros((seq, rest), F32)], axis=1)

    def rope(t, cs, up, dn):
        w = t.shape[1]
        reps = (1, w // HEAD_DIM)
        return (t * jnp.tile(cs, reps) + pltpu.roll(t, w - half, 1) * jnp.tile(up, reps)
                + pltpu.roll(t, half, 1) * jnp.tile(dn, reps))

    def rope_t(dt, cs, up, dn):
        w = dt.shape[1]
        reps = (1, w // HEAD_DIM)
        return (dt * jnp.tile(cs, reps) + pltpu.roll(dt * jnp.tile(up, reps), half, 1)
                + pltpu.roll(dt * jnp.tile(dn, reps), w - half, 1))

    tables = [(t_cs, 0, HEAD_DIM), (t_up, 0, HEAD_DIM), (t_dn, 0, HEAD_DIM)]

    kinds = ("col", "col", "row", "row", "row", "row")
    big_weights = (w_in, w_gu, w_oa, w_ob, w_out, w_down)
    chip_idx = jnp.reshape(chip, (1,)).astype(jnp.int32)
    shard_shapes = [w.shape[1:] for w in big_weights]
    fulls = [[_cast_into(f"cast_w_{l}_{j}", w, l, kinds[j], chip_idx) for j, w in enumerate(big_weights)]
             for l in range(n_layers)]

    def gather_job(l, which):
        return _job_gather([fulls[l][j] for j in which], [shard_shapes[j] for j in which], [kinds[j] for j in which])

    ride_in_proj, ride_ffn_up, ride_ffn_down = (0,), (1,), (2, 3, 4, 5)
    gathered = [_run_jobs("gather_w_0", [gather_job(0, range(6))])[0]] + [[None] * 6 for _ in range(1, n_layers)]
    w_dw_all = _gather_small("gather_dw", w_dw, SAME_CORE_CHIPS)
    w_dw_full = jnp.transpose(w_dw_all, (1, 2, 0, 3)).reshape(n_layers, CONV_WIDTH, d)
    w_dw32 = jnp.pad(w_dw_full, ((0, 0), (0, 32 - CONV_WIDTH), (0, 0)))
    sink_b = jnp.broadcast_to(sink[:, :, None], (n_layers, hq, LANE))

    def vec(a, l):
        return a[l][None, :]

    def res_ln(xprev, y, gt, g, b, scn, shn):
        xn = _ln(alpha * xprev + (1.0 + gt) * y, g, b)
        return xn, xn * (1.0 + scn) + shn

    def merge(ya, yb, ga, gb):
        return _sigmoid(ga) * ya + _sigmoid(gb) * yb

    def swiglu(gate, up):
        return gate * _sigmoid(gate) * up

    h = _rowwise("modulate_in", lambda xv, sc, sh: xv * (1.0 + sc) + sh, seq, [(x2, 0, d)], [sc_a[0], sh_a[0]],
                 [(d, BF16)])
    xprev = x2
    saved = []
    for l in range(n_layers):
        wi, wg, woa, wob, wout, wdn = gathered[l]
        nxt = l + 1 < n_layers

        def mm_carrying(name, a_, b_, which):
            if not nxt:
                return _mm(name, a_, b_, "nn", F32)
            res, (got,) = _mm(name, a_, b_, "nn", F32, jobs=[gather_job(l + 1, which)])
            for j, arr in zip(which, got):
                gathered[l + 1][j] = arr
            return res

        z = mm_carrying(f"in_proj_{l}", h, wi, ride_in_proj)
        qr, kr, vb = _rowwise(
            f"qkv_prep_{l}", lambda q, k, v, cs, up, dn: (rope(q, cs, up, dn), rope(k, cs, up, dn), v), seq,
            [(z, 0, d), (z, off_k, dkv), (z, off_v, dkv)] + tables, [], [(d, BF16), (dkv, BF16), (dkv, BF16)])
        att = _attn_fwd(f"attn_{l}", qr, kr, vb, sink_b[l])
        y_a = _mm(f"attn_out_{l}", att, woa, "nn", F32)
        u = _rowwise(f"glu_{l}", lambda a, b: a * _sigmoid(b), seq, [(z, off_ga, d), (z, off_gb, d)], [], [(d, F32)])
        u2, cv = _conv_fwd(f"conv_{l}", u, w_dw32[l], vec(conv_ln_g, l), vec(conv_ln_b, l))
        y_b = _mm(f"conv_out_{l}", cv, wob, "nn", F32)
        mg = _rowwise(f"merge_{l}", merge, seq, [(y_a, 0, d), (y_b, 0, d), (z, off_gta, d), (z, off_gtb, d)], [],
                      [(d, BF16)])
        o = _mm(f"mix_out_{l}", mg, wout, "nn", F32)
        x1, h2 = _rowwise(f"res_ln1_{l}", res_ln, seq, [(xprev, 0, d), (o, 0, d)],
                          [gt_a[l], vec(ln1_g, l), vec(ln1_b, l), sc_f[l], sh_f[l]], [(d, F32), (d, BF16)])
        gu = mm_carrying(f"ffn_up_{l}", h2, wg, ride_ffn_up)
        f = _rowwise(f"swiglu_{l}", swiglu, seq, [(gu, 0, d_ff), (gu, d_ff, d_ff)], [], [(d_ff, BF16)])
        ffn = mm_carrying(f"ffn_down_{l}", f, wdn, ride_ffn_down)
        saved.append(dict(xprev=xprev, h=h, z=z, qr=qr, kr=kr, vb=vb, att=att, u=u, u2=u2, cv=cv, y_a=y_a, y_b=y_b,
                          mg=mg, o=o, x1=x1, h2=h2, gu=gu, f=f, ffn=ffn))
        if l + 1 < n_layers:
            xprev, h = _rowwise(f"res_ln2_{l}", res_ln, seq, [(x1, 0, d), (ffn, 0, d)],
                                [gt_f[l], vec(ln2_g, l), vec(ln2_b, l), sc_a[l + 1], sh_a[l + 1]], [(d, F32), (d, BF16)])

    def res_ln_bwd(xp, y, dxn, dh, gt, g, b, scn, shn):
        _, vjp = jax.vjp(res_ln, xp, y, gt, g, b, scn, shn)
        return vjp((dxn, dh))

    def last_ln_bwd(xp, y, tgt, gt, g, b):
        def head(xp_, y_, gt_, g_, b_):
            return _ln(alpha * xp_ + (1.0 + gt_) * y_, g_, b_)
        out, vjp = jax.vjp(head, xp, y, gt, g, b)
        err = out - tgt
        loss = 0.5 * jnp.sum(jnp.sum(err * err, axis=-1, keepdims=True) / d, axis=0, keepdims=True)
        return vjp(err / d) + (jnp.broadcast_to(loss, (1, LANE)),)

    def merge_bwd(dmg, ya, yb, ga, gb):
        _, vjp = jax.vjp(merge, ya, yb, ga, gb)
        dya, dyb, dga, dgb = vjp(dmg)
        return dya, dyb, jnp.concatenate([dga, dgb], axis=1)

    def swiglu_bwd(df, gate, up):
        _, vjp = jax.vjp(swiglu, gate, up)
        return jnp.concatenate(vjp(df), axis=1)

    vec_d = ((1, d), F32)
    small = [None] * n_layers
    big = None
    loss_part = None
    dxn = dh = None
    pending = None
    stacks = ((w_in, m_w_in, v_w_in), (w_gu, m_w_gu, v_w_gu), (w_oa, m_w_oa, v_w_oa), (w_ob, m_w_ob, v_w_ob),
              (w_out, m_w_out, v_w_out), (w_down, m_w_down, v_w_down))

    def adamw_all(layer, full, prev):
        return [_adamw_layer(f"adamw_{layer}_{j}", layer, full[j], kinds[j], *stacks[j], None if prev is None else prev[j])
                for j in range(6)]

    for l in reversed(range(n_layers)):
        sv = saved[l]
        wi, wg, woa, wob, wout, wdn = gathered[l]
        ln2 = [gt_f[l], vec(ln2_g, l), vec(ln2_b, l)]
        if l + 1 == n_layers:
            dx1, dffn, d_gtf, d_g2, d_b2, loss_part = _rowwise(
                "last_ln_bwd", last_ln_bwd, seq, [(sv["x1"], 0, d), (sv["ffn"], 0, d), (target, 0, d)], ln2,
                [(d, F32), (d, BF16)], [vec_d, vec_d, vec_d, ((1, LANE), F32)])
            d_sca_next = d_sha_next = None
        else:
            dx1, dffn, d_gtf, d_g2, d_b2, d_sca_next, d_sha_next = _rowwise(
                f"res_ln2_bwd_{l}", res_ln_bwd, seq, [(sv["x1"], 0, d), (sv["ffn"], 0, d), (dxn, 0, d), (dh, 0, d)],
                ln2 + [sc_a[l + 1], sh_a[l + 1]], [(d, F32), (d, BF16)], [vec_d] * 5)
            small[l + 1]["sc_a"], small[l + 1]["sh_a"] = d_sca_next, d_sha_next
        def riding(make_job, *job_args):
            return [] if pending is None else [make_job(*job_args)]

        def unpack(res):
            return res if pending is not None else (res, [None])

        df, (got,) = unpack(_mm(f"ffn_down_dx_{l}", dffn, wdn, "nt", F32, jobs=riding(_job_pair, pending)))
        halves = None if pending is None else [
            _pair_sum(f"pair_sum_{l + 1}_{j}", pending[j], got[j]) for j in range(6)]
        g_down = _mm(f"ffn_down_dw_{l}", sv["f"], dffn, "tn", BF16, split="cols", c_idx=c_idx)
        dgu = _rowwise(f"swiglu_bwd_{l}", swiglu_bwd, seq, [(df, 0, d_ff), (sv["gu"], 0, d_ff), (sv["gu"], d_ff, d_ff)],
                       [], [(2 * d_ff, BF16)])
        dh2, (parts_in,) = unpack(_mm(f"ffn_up_dx_{l}", dgu, wg, "nt", F32,
                                      jobs=riding(lambda: _job_chips(halves[:1], kinds[:1]))))
        g_gu, (parts_gu,) = unpack(_mm(f"ffn_up_dw_{l}", sv["h2"], dgu, "tn", BF16, split="rows", c_idx=c_idx,
                                       jobs=riding(lambda: _job_chips(halves[1:2], kinds[1:2]))))
        dxp, d_o, d_gta, d_g1, d_b1, d_scf, d_shf = _rowwise(
            f"res_ln1_bwd_{l}", res_ln_bwd, seq, [(sv["xprev"], 0, d), (sv["o"], 0, d), (dx1, 0, d), (dh2, 0, d)],
            [gt_a[l], vec(ln1_g, l), vec(ln1_b, l), sc_f[l], sh_f[l]], [(d, F32), (d, BF16)], [vec_d] * 5)
        dmg = _mm(f"mix_out_dx_{l}", d_o, wout, "nt", F32)
        g_out = _mm(f"mix_out_dw_{l}", sv["mg"], d_o, "tn", BF16, split="cols", c_idx=c_idx)
        z = sv["z"]
        dya, dyb, d_gates = _rowwise(
            f"merge_bwd_{l}", merge_bwd, seq,
            [(dmg, 0, d), (sv["y_a"], 0, d), (sv["y_b"], 0, d), (z, off_gta, d), (z, off_gtb, d)], [],
            [(d, BF16), (d, BF16), (2 * d, BF16)])
        d_att = _mm(f"attn_out_dx_{l}", dya, woa, "nt", BF16)
        g_oa = _mm(f"attn_out_dw_{l}", sv["att"], dya, "tn", BF16, split="cols", c_idx=c_idx)
        d_cv = _mm(f"conv_out_dx_{l}", dyb, wob, "nt", F32)
        g_ob = _mm(f"conv_out_dw_{l}", sv["cv"], dyb, "tn", BF16, split="cols", c_idx=c_idx)
        du2, d_wdw, d_cg, d_cb = _conv_bwd_a(f"conv_bwd_a_{l}", sv["u"], sv["u2"], d_cv, vec(conv_ln_g, l),
                                             vec(conv_ln_b, l))
        d_glu = _conv_bwd_b(f"conv_bwd_b_{l}", du2, z, off_ga, off_gb, w_dw32[l])
        dqr, dkr, dvb, d_sink = _attn_bwd(f"attn_bwd_{l}", sv["qr"], sv["kr"], sv["vb"], sink_b[l], d_att)
        d_qkv = _rowwise(
            f"qkv_bwd_{l}",
            lambda dq_, dk_, dv_, cs, up, dn: jnp.concatenate([rope_t(dq_, cs, up, dn), rope_t(dk_, cs, up, dn), dv_], axis=1),
            seq, [(dqr, 0, d), (dkr, 0, dkv), (dvb, 0, dkv)] + tables, [], [(d + 2 * dkv, BF16)])
        dz = jnp.concatenate([d_qkv, d_glu, d_gates], axis=1)
        dh, (parts_rest,) = unpack(_mm(f"in_proj_dx_{l}", dz, wi, "nt", F32,
                                       jobs=riding(lambda: _job_chips(halves[2:], kinds[2:]))))
        reduced = None if pending is None else [
            _sum_slots(f"sum_chips_{l + 1}_{j}", p, into_slot=c_idx) for j, p in enumerate(parts_in + parts_gu + parts_rest)]
        g_in, (full,) = unpack(_mm(f"in_proj_dw_{l}", sv["h"], dz, "tn", BF16, split="rows", c_idx=c_idx,
                                   jobs=riding(lambda: _job_join(reduced))))
        if pending is not None:
            big = adamw_all(l + 1, full, big)
        dxn = dxp
        small[l] = dict(gt_a=d_gta, sh_f=d_shf, sc_f=d_scf, gt_f=d_gtf, ln1_g=d_g1, ln1_b=d_b1, ln2_g=d_g2, ln2_b=d_b2,
                        conv_ln_g=d_cg, conv_ln_b=d_cb, sink=d_sink[:, :1].reshape(1, hq), w_dw=d_wdw[:CONV_WIDTH])
        pending = [g_in, g_gu, g_oa, g_ob, g_out, g_down]

    (got,) = _run_jobs("rs_pair_0", [_job_pair(pending)])
    halves = [_pair_sum(f"pair_sum_0_{j}", pending[j], got[j]) for j in range(6)]
    (parts,) = _run_jobs("rs_chips_0", [_job_chips(halves, kinds)])
    reduced = [_sum_slots(f"sum_chips_0_{j}", parts[j], into_slot=c_idx) for j in range(6)]
    (full,) = _run_jobs("rs_join_0", [_job_join(reduced)])
    big = adamw_all(0, full, big)

    grad_x, d_sca0, d_sha0 = _rowwise(
        "modulate_in_bwd", lambda xv, dhv, dxv, sc: (dxv + dhv * (1.0 + sc), jnp.sum(dhv * xv, axis=0, keepdims=True),
                                                     jnp.sum(dhv, axis=0, keepdims=True)),
        seq, [(x2, 0, d), (dh, 0, d), (dxn, 0, d)], [sc_a[0]], [(d, F32)], [vec_d, vec_d])
    small[0]["sc_a"], small[0]["sh_a"] = d_sca0, d_sha0

    order = ("sh_a", "sc_a", "gt_a", "sh_f", "sc_f", "gt_f", "conv_ln_g", "conv_ln_b", "ln1_g", "ln1_b", "ln2_g", "ln2_b")
    rows = []
    for l in range(n_layers):
        rows += [small[l][k] for k in order]
        rows.append(jnp.pad(small[l]["sink"], ((0, 0), (0, d - hq))))
        rows.append(small[l]["w_dw"])
    rows.append(jnp.pad(loss_part, ((0, 0), (0, d - LANE))))
    n_small = sum(r.shape[0] for r in rows)
    pad_rows = (-n_small) % 8
    packed = jnp.concatenate(rows + [jnp.zeros((pad_rows, d), F32)], axis=0)
    everyone = _gather_small("gather_small_grads", packed, ALL_DEVICES)
    total = _sum_slots("sum_small_grads", everyone)
    per_layer = len(order) + 1 + CONV_WIDTH
    tot = total[:n_layers * per_layer].reshape(n_layers, per_layer, d)
    g_mod = tot[:, :N_MOD].reshape(n_layers, N_MOD * d)
    g_small = {k: tot[:, N_MOD + j] for j, k in enumerate(order[N_MOD:])}
    g_sink = tot[:, len(order), :hq]
    g_dw_full = tot[:, len(order) + 1:]
    cols_dw = w_dw.shape[2]
    g_dw = lax.dynamic_slice_in_dim(g_dw_full, chip * cols_dw, cols_dw, axis=2)
    loss = total[n_layers * per_layer, 0]

    d_mod_all = everyone[:, :n_layers * per_layer].reshape(8, n_layers, per_layer, d)[:, :, :N_MOD]
    d_mod_all = d_mod_all.reshape(8, n_layers, N_MOD * d)
    cols_ada = w_ada.shape[2]
    d_mod_mine = lax.dynamic_slice_in_dim(d_mod_all, chip * cols_ada, cols_ada, axis=2)
    dmod16 = jnp.concatenate([d_mod_mine, jnp.zeros_like(d_mod_mine)], axis=0)
    dmod16 = jnp.transpose(dmod16, (1, 0, 2)).astype(BF16)
    ada = _adamw_ada("adamw_ada", c16, dmod16, w_ada, m_w_ada, v_w_ada)

    def small_step(name, g, w, m, v):
        shp = w.shape
        g2, w2, m2, v2 = (a.reshape(-1, shp[-1]) for a in (g, w, m, v))
        return (g,) + tuple(a.reshape(shp) for a in _adamw_small(name, g2, w2, m2, v2))

    res = {
        "w_ada": ada,
        "b_ada": small_step("adamw_b_ada", g_mod, b_ada, m_b_ada, v_b_ada),
        "sink": small_step("adamw_sink", g_sink, sink, m_sink, v_sink),
        "w_dw": small_step("adamw_w_dw", g_dw, w_dw, m_w_dw, v_w_dw),
        "conv_ln_g": small_step("adamw_conv_ln_g", g_small["conv_ln_g"], conv_ln_g, m_conv_ln_g, v_conv_ln_g),
        "conv_ln_b": small_step("adamw_conv_ln_b", g_small["conv_ln_b"], conv_ln_b, m_conv_ln_b, v_conv_ln_b),
        "ln1_g": small_step("adamw_ln1_g", g_small["ln1_g"], ln1_g, m_ln1_g, v_ln1_g),
        "ln1_b": small_step("adamw_ln1_b", g_small["ln1_b"], ln1_b, m_ln1_b, v_ln1_b),
        "ln2_g": small_step("adamw_ln2_g", g_small["ln2_g"], ln2_g, m_ln2_g, v_ln2_g),
        "ln2_b": small_step("adamw_ln2_b", g_small["ln2_b"], ln2_b, m_ln2_b, v_ln2_b),
        "w_in": big[0], "w_gu": big[1], "w_oa": big[2], "w_ob": big[3], "w_out": big[4], "w_down": big[5],
    }
    names = ("w_ada", "b_ada", "w_in", "sink", "w_dw", "conv_ln_g", "conv_ln_b", "w_oa", "w_ob", "w_out", "ln1_g", "ln1_b",
             "w_gu", "w_down", "ln2_g", "ln2_b")
    outs = [loss, grad_x[None]]
    for field in range(4):
        outs += [res[k][field] for k in names]
    return tuple(outs)
```

```python
import functools
import math

import jax
import jax.numpy as jnp
from jax import lax
from jax.experimental import pallas as pl
from jax.experimental.pallas import tpu as pltpu

F32 = jnp.float32
BF16 = jnp.bfloat16
MESH = pl.DeviceIdType.MESH

HEAD_DIM = 128
GQA_GROUP = 4
WINDOW = 128
BLOCK = 128
BAND = 3 * BLOCK
ROPE_DIM = HEAD_DIM // 4
ROPE_THETA = 500000.0
CONV_WIDTH = 31
CONV_PAD = CONV_WIDTH // 2
CONV_HALO = 16
N_MOD = 6
LN_EPS = 1e-5
NEG_INF = -1e30
ADAM_LR = 0.001
ADAM_B1 = 0.9
ADAM_B2 = 0.999
ADAM_EPS = 1e-08
ADAM_WD = 0.01
ADAM_STEP = 10

LANE = 128
SUBLANES = 8
V7X_VMEM_LIMIT = 56 * 1024 * 1024
ROW_TILE = 256
CONV_ROWS = 32
CONV_LANES = 256

HBM_SPEC = pl.BlockSpec(memory_space=pltpu.HBM)


def _params(*sem):
    return pltpu.CompilerParams(dimension_semantics=sem, vmem_limit_bytes=V7X_VMEM_LIMIT)


def _pick(n, cands, even=False):
    for t in cands:
        if n % t == 0 and (not even or (n // t) % 2 == 0):
            return t
    raise ValueError(f"no tile for {n} in {cands}")


BLOCK_BUDGET = 10 * 1024 * 1024
MM_BLOCK_BUDGET = 40 * 1024 * 1024


def _rows_within(n_rows, bytes_per_row, cands=(256, 128, 64, 32, 16, 8)):
    fit = [t for t in cands if n_rows % t == 0]
    for t in fit:
        if t * bytes_per_row <= BLOCK_BUDGET:
            return t
    return fit[-1]


def _sigmoid(v):
    return jax.nn.sigmoid(v)


def _const_map(ndim):
    return lambda *_: (0,) * ndim


def _rowwise(name, fn, n_rows, row_ins, vec_ins, row_outs, vec_outs=()):
    per_row = sum(w * a.dtype.itemsize for a, _, w in row_ins) + sum(w * jnp.dtype(dt).itemsize for w, dt in row_outs)
    tr = _rows_within(n_rows, per_row, (ROW_TILE, 128, 64))
    in_specs, args, pieces = [], [], []
    for arr, off, width in row_ins:
        bw = math.gcd(off, width) if off else width
        assert bw % LANE == 0 and arr.shape[0] == n_rows
        pieces.append(width // bw)
        for p in range(width // bw):
            in_specs.append(pl.BlockSpec((tr, bw), functools.partial(lambda i, blk: (i, blk), blk=off // bw + p)))
            args.append(arr)
    for v in vec_ins:
        in_specs.append(pl.BlockSpec(v.shape, _const_map(v.ndim)))
        args.append(v)
    out_shape = [jax.ShapeDtypeStruct((n_rows, w), dt) for w, dt in row_outs]
    out_specs = [pl.BlockSpec((tr, w), lambda i: (i, 0)) for w, _ in row_outs]
    for shp, dt in vec_outs:
        out_shape.append(jax.ShapeDtypeStruct(shp, dt))
        out_specs.append(pl.BlockSpec(shp, _const_map(len(shp))))
    n_in, n_row_out = len(args), len(row_outs)

    def body(*refs):
        in_refs, out_refs = refs[:n_in], refs[n_in:]
        vals, k = [], 0
        for npc in pieces:
            ps = [in_refs[k + p][...] for p in range(npc)]
            k += npc
            vals.append(ps[0] if npc == 1 else jnp.concatenate(ps, axis=1))
        for _ in vec_ins:
            vals.append(in_refs[k][...])
            k += 1
        outs = fn(*vals)
        if not isinstance(outs, (tuple, list)):
            outs = (outs,)
        assert len(outs) == len(out_refs)
        for j in range(n_row_out):
            out_refs[j][...] = outs[j].astype(out_refs[j].dtype)
        if vec_outs:
            @pl.when(pl.program_id(0) == 0)
            def _():
                for j in range(n_row_out, len(out_refs)):
                    out_refs[j][...] = jnp.zeros(out_refs[j].shape, out_refs[j].dtype)
            for j in range(n_row_out, len(out_refs)):
                out_refs[j][...] += outs[j].astype(out_refs[j].dtype)

    res = pl.pallas_call(
        body, name=name, grid=(n_rows // tr,), in_specs=in_specs, out_specs=out_specs, out_shape=out_shape,
        compiler_params=_params("arbitrary"),
    )(*args)
    return res[0] if len(res) == 1 else res


def _mm(name, a, b, mode, out_dtype, b_layer=None, split=None, c_idx=None, jobs=()):
    bshape = b.shape[1:] if b_layer is not None else b.shape
    if mode == "nn":
        (m, k), (k2, n) = a.shape, bshape
        dims = (((1,), (0,)), ((), ()))
    elif mode == "nt":
        (m, k), (n, k2) = a.shape, bshape
        dims = (((1,), (1,)), ((), ()))
    else:
        (k, m), (k2, n) = a.shape, bshape
        dims = (((0,), (0,)), ((), ()))
    assert k == k2, (name, a.shape, b.shape)
    tm = _pick(m, (1024, 512, 256, 128, 16), even=(split == "rows"))
    tn = _pick(n, (1024, 512, 256, 128), even=(split == "cols"))
    out_bytes = jnp.dtype(out_dtype).itemsize
    b_bytes = b.dtype.itemsize

    def blocks_fit(t):
        acc = 0 if t == k else tm * tn * 4
        return 2 * (tm * t * a.dtype.itemsize + tn * t * b_bytes + tm * tn * out_bytes) + acc <= MM_BLOCK_BUDGET

    tk = next(t for t in (4096, 2816, 2048, 1408, 1024, 704, 512, 256, 128) if k % t == 0 and (blocks_fit(t) or t == 128))
    ni, nj, nk = m // tm, n // tn, k // tk

    if mode == "nn":
        a_spec = pl.BlockSpec((tm, tk), lambda i, j, kk, *_: (i, kk))
        b_blk, b_map = (tk, tn), (lambda i, j, kk: (kk, j))
    elif mode == "nt":
        a_spec = pl.BlockSpec((tm, tk), lambda i, j, kk, *_: (i, kk))
        b_blk, b_map = (tn, tk), (lambda i, j, kk: (j, kk))
    else:
        a_spec = pl.BlockSpec((tk, tm), lambda i, j, kk, *_: (kk, i))
        b_blk, b_map = (tk, tn), (lambda i, j, kk: (kk, j))
    if b_layer is None:
        b_spec = pl.BlockSpec(b_blk, lambda i, j, kk, *_: b_map(i, j, kk))
    else:
        b_spec = pl.BlockSpec((None,) + b_blk, lambda i, j, kk, *_: (b_layer,) + b_map(i, j, kk))

    if split is None:
        out_shape = jax.ShapeDtypeStruct((m, n), out_dtype)
        o_spec = pl.BlockSpec((tm, tn), lambda i, j, kk, *_: (i, j))
    elif split == "rows":
        out_shape = jax.ShapeDtypeStruct((2, m // 2, n), out_dtype)
        o_spec = pl.BlockSpec(
            (None, tm, tn), lambda i, j, kk, c_ref: (jnp.where(i // (ni // 2) == c_ref[0], 0, 1), i % (ni // 2), j))
    else:
        out_shape = jax.ShapeDtypeStruct((2, m, n // 2), out_dtype)
        o_spec = pl.BlockSpec(
            (None, tm, tn), lambda i, j, kk, c_ref: (jnp.where(j // (nj // 2) == c_ref[0], 0, 1), i, j % (nj // 2)))

    n_job_in = sum(len(jb.ins) for jb in jobs)
    n_job_out = sum(len(jb.out_shapes) for jb in jobs)
    n_acc = 0 if nk == 1 else 1

    def body(*refs):
        if split is not None:
            refs = refs[1:]
        a_ref, b_ref = refs[:2]
        job_ins = refs[2:2 + n_job_in]
        o_ref = refs[2 + n_job_in]
        job_outs = refs[3 + n_job_in:3 + n_job_in + n_job_out]
        scratch_refs = refs[3 + n_job_in + n_job_out:]
        cut = _job_refs(jobs, job_ins, job_outs, scratch_refs[n_acc:])
        i, j, kk = pl.program_id(0), pl.program_id(1), pl.program_id(2)

        if jobs:
            @pl.when((i == 0) & (j == 0) & (kk == 0))
            def _():
                for jb, parts in zip(jobs, cut):
                    jb.start(*parts)

        part = lax.dot_general(a_ref[...].astype(BF16), b_ref[...].astype(BF16), dims, preferred_element_type=F32)
        if nk == 1:
            o_ref[...] = part.astype(o_ref.dtype)
        else:
            acc_ref = scratch_refs[0]

            @pl.when(kk == 0)
            def _():
                acc_ref[...] = part

            @pl.when(kk > 0)
            def _():
                acc_ref[...] += part

            @pl.when(kk == nk - 1)
            def _():
                o_ref[...] = acc_ref[...].astype(o_ref.dtype)

        if jobs:
            @pl.when((i == ni - 1) & (j == nj - 1) & (kk == nk - 1))
            def _():
                for jb, parts in zip(jobs, cut):
                    jb.finish(*parts)

    scratch = ([] if nk == 1 else [pltpu.VMEM((tm, tn), F32)]) + [s for jb in jobs for s in jb.sems]
    params = _params(*(["arbitrary"] * 3 if jobs else ["parallel", "parallel", "arbitrary"]))
    in_specs = [a_spec, b_spec] + [HBM_SPEC] * n_job_in
    out_specs = [o_spec] + [HBM_SPEC] * n_job_out
    out_shapes = [out_shape] + [s for jb in jobs for s in jb.out_shapes]
    operands = [a, b] + [x for jb in jobs for x in jb.ins]
    n_pre = 0 if split is None else 1
    aliases = _job_aliases(jobs, n_pre + 2, 1)
    if split is None:
        res = pl.pallas_call(
            body, name=name, grid=(ni, nj, nk), in_specs=in_specs, out_specs=out_specs, out_shape=out_shapes,
            scratch_shapes=scratch, input_output_aliases=aliases, compiler_params=params,
        )(*operands)
    else:
        grid_spec = pltpu.PrefetchScalarGridSpec(
            num_scalar_prefetch=1, grid=(ni, nj, nk), in_specs=in_specs, out_specs=out_specs, scratch_shapes=scratch)
        res = pl.pallas_call(body, name=name, grid_spec=grid_spec, out_shape=out_shapes, input_output_aliases=aliases,
                             compiler_params=params)(c_idx, *operands)
    if not jobs:
        return res[0]
    return res[0], _job_results(jobs, res[1:])


def _attn_tile(seq):
    return _pick(seq, (256, 128))


def _heads_stacked(ref, b):
    return jnp.concatenate(
        [ref[b * BLOCK:(b + 1) * BLOCK, g * HEAD_DIM:(g + 1) * HEAD_DIM] for g in range(GQA_GROUP)], axis=0)


def _attn_scores(q_ref, k_ref, v_ref, sink_ref, kvh, i, b, tq, seq):
    rows = GQA_GROUP * BLOCK
    q0 = i * tq + b * BLOCK
    k_off = pl.multiple_of(jnp.clip(q0 - BLOCK, 0, seq - BAND), BLOCK)
    kw = k_ref[pl.ds(k_off, BAND), :]
    vw = v_ref[pl.ds(k_off, BAND), :]
    q_pos = q0 + (lax.broadcasted_iota(jnp.int32, (rows, BAND), 0) & (BLOCK - 1))
    k_pos = k_off + lax.broadcasted_iota(jnp.int32, (rows, BAND), 1)
    valid = jnp.abs(k_pos - q_pos) <= WINDOW
    qs = _heads_stacked(q_ref, b)
    s = lax.dot_general(qs, kw, (((1,), (1,)), ((), ())), preferred_element_type=F32) * (HEAD_DIM ** -0.5)
    s = jnp.where(valid, s, NEG_INF)
    sink = jnp.concatenate(
        [jnp.broadcast_to(sink_ref[pl.ds(kvh * GQA_GROUP + g, 1), :][:, :1], (BLOCK, 1)) for g in range(GQA_GROUP)], axis=0)
    m = jnp.maximum(jnp.max(s, axis=-1, keepdims=True), sink)
    p = jnp.exp(s - m)
    p_sink = jnp.exp(sink - m)
    denom = jnp.sum(p, axis=-1, keepdims=True) + p_sink
    return k_off, kw, vw, qs, p / denom, p_sink / denom


def _attn_fwd(name, qr, kr, vb, sink_b):
    seq, dq = qr.shape
    nkv = kr.shape[1] // HEAD_DIM
    tq = _attn_tile(seq)
    gw = GQA_GROUP * HEAD_DIM

    def body(q_ref, k_ref, v_ref, sink_ref, o_ref):
        kvh, i = pl.program_id(0), pl.program_id(1)
        for b in range(tq // BLOCK):
            _, _, vw, _, pn, _ = _attn_scores(q_ref, k_ref, v_ref, sink_ref, kvh, i, b, tq, seq)
            o = jnp.dot(pn.astype(BF16), vw, preferred_element_type=F32).astype(o_ref.dtype)
            for g in range(GQA_GROUP):
                o_ref[b * BLOCK:(b + 1) * BLOCK, g * HEAD_DIM:(g + 1) * HEAD_DIM] = o[g * BLOCK:(g + 1) * BLOCK]

    return pl.pallas_call(
        body, name=name, grid=(nkv, seq // tq),
        in_specs=[
            pl.BlockSpec((tq, gw), lambda h, i: (i, h)),
            pl.BlockSpec((seq, HEAD_DIM), lambda h, i: (0, h)),
            pl.BlockSpec((seq, HEAD_DIM), lambda h, i: (0, h)),
            pl.BlockSpec(sink_b.shape, lambda h, i: (0, 0)),
        ],
        out_specs=pl.BlockSpec((tq, gw), lambda h, i: (i, h)),
        out_shape=jax.ShapeDtypeStruct((seq, dq), BF16),
        compiler_params=_params("arbitrary", "arbitrary"),
    )(qr, kr, vb, sink_b)


def _attn_bwd(name, qr, kr, vb, sink_b, d_att):
    seq, dq = qr.shape
    dkv = kr.shape[1]
    nkv = dkv // HEAD_DIM
    tq = _attn_tile(seq)
    gw = GQA_GROUP * HEAD_DIM
    tn_dims = (((0,), (0,)), ((), ()))

    def body(q_ref, k_ref, v_ref, sink_ref, do_ref, dq_ref, dk_ref, dv_ref, dsink_ref):
        kvh, i = pl.program_id(0), pl.program_id(1)

        @pl.when(i == 0)
        def _():
            dk_ref[...] = jnp.zeros(dk_ref.shape, F32)
            dv_ref[...] = jnp.zeros(dv_ref.shape, F32)

        @pl.when((i == 0) & (kvh == 0))
        def _():
            dsink_ref[...] = jnp.zeros(dsink_ref.shape, F32)

        for b in range(tq // BLOCK):
            k_off, kw, vw, qs, pn, pn_sink = _attn_scores(q_ref, k_ref, v_ref, sink_ref, kvh, i, b, tq, seq)
            dos = _heads_stacked(do_ref, b)
            dp = lax.dot_general(dos, vw, (((1,), (1,)), ((), ())), preferred_element_type=F32)
            delta = jnp.sum(pn * dp, axis=-1, keepdims=True)
            ds = (pn * (dp - delta) * (HEAD_DIM ** -0.5)).astype(BF16)
            dqs = jnp.dot(ds, kw, preferred_element_type=F32)
            sink_term = pn_sink * delta
            for g in range(GQA_GROUP):
                dq_ref[b * BLOCK:(b + 1) * BLOCK, g * HEAD_DIM:(g + 1) * HEAD_DIM] = dqs[g * BLOCK:(g + 1) * BLOCK]
                d_sink = -jnp.sum(sink_term[g * BLOCK:(g + 1) * BLOCK], axis=0, keepdims=True)
                dsink_ref[pl.ds(kvh * GQA_GROUP + g, 1), :] += jnp.broadcast_to(d_sink, (1, LANE))
            dk_ref[pl.ds(k_off, BAND), :] += lax.dot_general(ds, qs, tn_dims, preferred_element_type=F32)
            dv_ref[pl.ds(k_off, BAND), :] += lax.dot_general(pn.astype(BF16), dos, tn_dims, preferred_element_type=F32)

    return pl.pallas_call(
        body, name=name, grid=(nkv, seq // tq),
        in_specs=[
            pl.BlockSpec((tq, gw), lambda h, i: (i, h)),
            pl.BlockSpec((seq, HEAD_DIM), lambda h, i: (0, h)),
            pl.BlockSpec((seq, HEAD_DIM), lambda h, i: (0, h)),
            pl.BlockSpec(sink_b.shape, lambda h, i: (0, 0)),
            pl.BlockSpec((tq, gw), lambda h, i: (i, h)),
        ],
        out_specs=[
            pl.BlockSpec((tq, gw), lambda h, i: (i, h)),
            pl.BlockSpec((seq, HEAD_DIM), lambda h, i: (0, h)),
            pl.BlockSpec((seq, HEAD_DIM), lambda h, i: (0, h)),
            pl.BlockSpec(sink_b.shape, lambda h, i: (0, 0)),
        ],
        out_shape=[
            jax.ShapeDtypeStruct((seq, dq), F32),
            jax.ShapeDtypeStruct((seq, dkv), F32),
            jax.ShapeDtypeStruct((seq, dkv), F32),
            jax.ShapeDtypeStruct(sink_b.shape, F32),
        ],
        compiler_params=_params("arbitrary", "arbitrary"),
    )(qr, kr, vb, sink_b, d_att)


def _halo_specs(tr, width, n_rows):
    per, last = tr // CONV_HALO, n_rows // CONV_HALO - 1
    return [
        pl.BlockSpec((tr, width), lambda i: (i, 0)),
        pl.BlockSpec((CONV_HALO, width), lambda i: (jnp.maximum(i * per - 1, 0), 0)),
        pl.BlockSpec((CONV_HALO, width), lambda i: (jnp.minimum((i + 1) * per, last), 0)),
    ]


def _ext_scratch(tr, width):
    return pltpu.VMEM((SUBLANES, tr + 2 * CONV_HALO, width), F32)


def _fill_ext(ext_ref, main_ref, prev_ref, next_ref, n_steps, tr):
    i = pl.program_id(0)
    ext_ref[0, 0:CONV_HALO, :] = jnp.where(i > 0, prev_ref[...], 0.0)
    ext_ref[0, CONV_HALO:CONV_HALO + tr, :] = main_ref[...]
    ext_ref[0, CONV_HALO + tr:, :] = jnp.where(i < n_steps - 1, next_ref[...], 0.0)
    rows = tr + 2 * CONV_HALO - SUBLANES
    for p in range(1, SUBLANES):
        ext_ref[p, 0:rows, :] = ext_ref[0, p:p + rows, :]


def _tap(ext_ref, r0, t, cols):
    whole, phase = divmod(1 + t, SUBLANES)
    return ext_ref[phase, r0 + whole * SUBLANES:r0 + whole * SUBLANES + CONV_ROWS, cols]


def _conv_taps(ext_ref, w_ref, out_ref, tr, width, flip):
    cw = min(CONV_LANES, width)
    for cc in range(width // cw):
        cols = slice(cc * cw, (cc + 1) * cw)
        for rc in range(tr // CONV_ROWS):
            acc = jnp.zeros((CONV_ROWS, cw), F32)
            for t in range(CONV_WIDTH):
                wt = CONV_WIDTH - 1 - t if flip else t
                acc += _tap(ext_ref, rc * CONV_ROWS, t, cols) * w_ref[wt:wt + 1, cols]
            out_ref[rc * CONV_ROWS:(rc + 1) * CONV_ROWS, cols] = acc


def _ln(v, g, b):
    mu = jnp.mean(v, axis=-1, keepdims=True)
    vc = v - mu
    var = jnp.mean(vc * vc, axis=-1, keepdims=True)
    return vc * lax.rsqrt(var + LN_EPS) * g + b


def _conv_fwd(name, u, w32, ln_g, ln_b):
    n_rows, width = u.shape
    tr = min(ROW_TILE, n_rows)
    n_steps = n_rows // tr

    def body(main_ref, prev_ref, next_ref, w_ref, g_ref, b_ref, u2_ref, cv_ref, ext_ref):
        _fill_ext(ext_ref, main_ref, prev_ref, next_ref, n_steps, tr)
        _conv_taps(ext_ref, w_ref, u2_ref, tr, width, flip=False)
        u3 = _ln(u2_ref[...], g_ref[...], b_ref[...])
        cv_ref[...] = (u3 * _sigmoid(u3)).astype(cv_ref.dtype)

    vec = lambda a: pl.BlockSpec(a.shape, lambda i: (0, 0))
    return pl.pallas_call(
        body, name=name, grid=(n_steps,),
        in_specs=_halo_specs(tr, width, n_rows) + [vec(w32), vec(ln_g), vec(ln_b)],
        out_specs=[pl.BlockSpec((tr, width), lambda i: (i, 0))] * 2,
        out_shape=[jax.ShapeDtypeStruct((n_rows, width), F32), jax.ShapeDtypeStruct((n_rows, width), BF16)],
        scratch_shapes=[_ext_scratch(tr, width)],
        compiler_params=_params("arbitrary"),
    )(u, u, u, w32, ln_g, ln_b)


def _conv_bwd_a(name, u, u2, d_cv, ln_g, ln_b):
    n_rows, width = u.shape
    tr = min(ROW_TILE // 2, n_rows)
    n_steps = n_rows // tr

    def body(main_ref, prev_ref, next_ref, u2_ref, dcv_ref, g_ref, b_ref, du2_ref, dw_ref, dg_ref, db_ref, ext_ref):
        @pl.when(pl.program_id(0) == 0)
        def _():
            dw_ref[...] = jnp.zeros(dw_ref.shape, F32)
            dg_ref[...] = jnp.zeros(dg_ref.shape, F32)
            db_ref[...] = jnp.zeros(db_ref.shape, F32)

        _fill_ext(ext_ref, main_ref, prev_ref, next_ref, n_steps, tr)

        def swish_ln(v, g, b):
            u3 = _ln(v, g, b)
            return u3 * _sigmoid(u3)

        _, vjp = jax.vjp(swish_ln, u2_ref[...], g_ref[...], b_ref[...])
        du2, dg, db = vjp(dcv_ref[...])
        du2_ref[...] = du2
        dg_ref[...] += dg
        db_ref[...] += db
        for cc in range(width // LANE):
            cols = slice(cc * LANE, (cc + 1) * LANE)
            for t0 in range(0, CONV_WIDTH, 16):
                taps = range(t0, min(t0 + 16, CONV_WIDTH))
                accs = {t: jnp.zeros((SUBLANES, LANE), F32) for t in taps}
                for rc in range(tr // CONV_ROWS):
                    r0 = rc * CONV_ROWS
                    d_blk = du2_ref[r0:r0 + CONV_ROWS, cols]
                    for t in taps:
                        prod = d_blk * _tap(ext_ref, r0, t, cols)
                        for q in range(CONV_ROWS // SUBLANES):
                            accs[t] = accs[t] + prod[q * SUBLANES:(q + 1) * SUBLANES]
                for t in taps:
                    dw_ref[t:t + 1, cols] += jnp.sum(accs[t], axis=0, keepdims=True)

    vec = lambda a: pl.BlockSpec(a.shape, lambda i: (0, 0))
    row = pl.BlockSpec((tr, width), lambda i: (i, 0))
    return pl.pallas_call(
        body, name=name, grid=(n_steps,),
        in_specs=_halo_specs(tr, width, n_rows) + [row, row, vec(ln_g), vec(ln_b)],
        out_specs=[row, pl.BlockSpec((32, width), lambda i: (0, 0)), vec(ln_g), vec(ln_b)],
        out_shape=[jax.ShapeDtypeStruct((n_rows, width), F32), jax.ShapeDtypeStruct((32, width), F32),
                   jax.ShapeDtypeStruct(ln_g.shape, F32), jax.ShapeDtypeStruct(ln_b.shape, F32)],
        scratch_shapes=[_ext_scratch(tr, width)],
        compiler_params=_params("arbitrary"),
    )(u, u, u, u2, d_cv, ln_g, ln_b)


def _conv_bwd_b(name, du2, z, off_a, off_b, w32):
    n_rows, width = du2.shape
    tr = min(ROW_TILE, n_rows)
    n_steps = n_rows // tr
    bw = math.gcd(math.gcd(off_a, off_b), width)
    npc = width // bw

    def body(*refs):
        main_ref, prev_ref, next_ref = refs[:3]
        a_refs, b_refs = refs[3:3 + npc], refs[3 + npc:3 + 2 * npc]
        w_ref, out_ref, ext_ref, du_ref = refs[3 + 2 * npc:]
        _fill_ext(ext_ref, main_ref, prev_ref, next_ref, n_steps, tr)
        _conv_taps(ext_ref, w_ref, du_ref, tr, width, flip=True)
        for p in range(npc):
            cols = slice(p * bw, (p + 1) * bw)
            du = du_ref[:, cols]
            sg = _sigmoid(b_refs[p][...])
            out_ref[:, p * bw:(p + 1) * bw] = (du * sg).astype(out_ref.dtype)
            out_ref[:, width + p * bw:width + (p + 1) * bw] = (du * a_refs[p][...] * sg * (1.0 - sg)).astype(out_ref.dtype)

    def piece(off, p):
        return pl.BlockSpec((tr, bw), functools.partial(lambda i, blk: (i, blk), blk=off // bw + p))

    in_specs = _halo_specs(tr, width, n_rows)
    in_specs += [piece(off_a, p) for p in range(npc)] + [piece(off_b, p) for p in range(npc)]
    in_specs.append(pl.BlockSpec(w32.shape, lambda i: (0, 0)))
    return pl.pallas_call(
        body, name=name, grid=(n_steps,), in_specs=in_specs,
        out_specs=pl.BlockSpec((tr, 2 * width), lambda i: (i, 0)),
        out_shape=jax.ShapeDtypeStruct((n_rows, 2 * width), BF16),
        scratch_shapes=[_ext_scratch(tr, width), pltpu.VMEM((tr, width), F32)],
        compiler_params=_params("arbitrary"),
    )(du2, du2, du2, *([z] * (2 * npc)), w32)


def _place():
    return lax.axis_index("x"), lax.axis_index("y"), lax.axis_index("c")


def _flip(v, m):
    return 1 - v if m else v


def _gather_small(name, v, masks):
    varies = [any(m[a] for m in masks) for a in range(3)]
    n = len(masks) + 1

    def slot(pos):
        idx = 0
        for a in range(3):
            if varies[a]:
                idx = idx * 2 + pos[a]
        return idx

    def body(v_ref, o_ref, send_sems, recv_sems, local_sem):
        me = _place()
        mine = pltpu.make_async_copy(v_ref, o_ref.at[slot(me)], local_sem)
        mine.start()
        peers = [tuple(_flip(me[a], m[a]) for a in range(3)) for m in masks]
        sends = [pltpu.make_async_remote_copy(v_ref, o_ref.at[slot(me)], send_sems.at[k], recv_sems.at[k],
                                              device_id=peer, device_id_type=MESH) for k, peer in enumerate(peers)]
        for cp in sends:
            cp.start()
        for k, peer in enumerate(peers):
            pltpu.make_async_remote_copy(v_ref, o_ref.at[slot(peer)], send_sems.at[k], recv_sems.at[k],
                                         device_id=peer, device_id_type=MESH).wait_recv()
        for cp in sends:
            cp.wait_send()
        mine.wait()

    return pl.pallas_call(
        body, name=name, in_specs=[HBM_SPEC], out_specs=HBM_SPEC,
        out_shape=jax.ShapeDtypeStruct((n,) + v.shape, v.dtype),
        scratch_shapes=[pltpu.SemaphoreType.DMA((n - 1,)), pltpu.SemaphoreType.DMA((n - 1,)), pltpu.SemaphoreType.DMA(())],
    )(v)


ALL_DEVICES = [(mx, my, mc) for mx in (0, 1) for my in (0, 1) for mc in (0, 1)][1:]
SAME_CORE_CHIPS = [(1, 0, 0), (0, 1, 0), (1, 1, 0)]


def _chips(x, y):
    return [(1 - x, y), (x, 1 - y), (1 - x, 1 - y)]


def _cast_into(name, w, layer, kind, chip_idx):
    _, r, cc = w.shape
    tr = _rows_within(r, cc * 6)
    steps = r // tr
    if kind == "col":
        shape, o_spec = (r, 4 * cc), pl.BlockSpec((tr, cc), lambda i, s_ref: (i, s_ref[0]))
    else:
        shape, o_spec = (4 * r, cc), pl.BlockSpec((tr, cc), lambda i, s_ref: (s_ref[0] * steps + i, 0))

    def body(s_ref, w_ref, o_ref):
        o_ref[...] = w_ref[...].astype(o_ref.dtype)

    grid_spec = pltpu.PrefetchScalarGridSpec(
        num_scalar_prefetch=1, grid=(steps,),
        in_specs=[pl.BlockSpec((None, tr, cc), lambda i, s_ref: (layer, i, 0))], out_specs=o_spec)
    return pl.pallas_call(body, name=name, grid_spec=grid_spec, out_shape=jax.ShapeDtypeStruct(shape, BF16),
                          compiler_params=_params("arbitrary"))(chip_idx, w)


class _Job:
    def __init__(self, ins, out_shapes, aliases, sems, start, finish):
        self.ins, self.out_shapes, self.aliases, self.sems = list(ins), list(out_shapes), dict(aliases), list(sems)
        self.start, self.finish = start, finish


def _job_refs(jobs, in_refs, out_refs, sem_refs):
    cut, i, o, s = [], 0, 0, 0
    for jb in jobs:
        cut.append((in_refs[i:i + len(jb.ins)], out_refs[o:o + len(jb.out_shapes)], sem_refs[s:s + len(jb.sems)]))
        i, o, s = i + len(jb.ins), o + len(jb.out_shapes), s + len(jb.sems)
    return cut


def _job_aliases(jobs, first_in, first_out):
    aliases, i, o = {}, first_in, first_out
    for jb in jobs:
        for a, b in jb.aliases.items():
            aliases[i + a] = o + b
        i, o = i + len(jb.ins), o + len(jb.out_shapes)
    return aliases


def _run_jobs(name, jobs):
    n_in = sum(len(jb.ins) for jb in jobs)
    n_out = sum(len(jb.out_shapes) for jb in jobs)

    def body(*refs):
        cut = _job_refs(jobs, refs[:n_in], refs[n_in:n_in + n_out], refs[n_in + n_out:])
        for jb, parts in zip(jobs, cut):
            jb.start(*parts)
        for jb, parts in zip(jobs, cut):
            jb.finish(*parts)

    res = pl.pallas_call(
        body, name=name, in_specs=[HBM_SPEC] * n_in, out_specs=[HBM_SPEC] * n_out,
        out_shape=[s for jb in jobs for s in jb.out_shapes], input_output_aliases=_job_aliases(jobs, 0, 0),
        scratch_shapes=[s for jb in jobs for s in jb.sems],
    )(*[a for jb in jobs for a in jb.ins])
    return _job_results(jobs, res)


def _job_results(jobs, flat):
    out, o = [], 0
    for jb in jobs:
        out.append(list(flat[o:o + len(jb.out_shapes)]))
        o += len(jb.out_shapes)
    return out


def _job_gather(fulls, shapes, kinds):
    n = len(fulls)
    for r, _ in shapes:
        assert r % 32 == 0

    def window(o_ref, j, s, h):
        r, cc = shapes[j]
        hr = r // 2
        if kinds[j] == "col":
            return o_ref.at[pl.ds(pl.multiple_of(h * hr, 16), hr), pl.ds(pl.multiple_of(s * cc, LANE), cc)]
        return o_ref.at[pl.ds(pl.multiple_of(s * r + h * hr, 16), hr), :]

    def first_copies(outs, sems):
        x, y, c = _place()
        cps = []
        for j in range(n):
            mine = window(outs[j], j, 2 * x + y, c)
            for k, chip in enumerate(_chips(x, y)):
                cps.append(pltpu.make_async_remote_copy(mine, mine, sems[0].at[3 * j + k], sems[1].at[3 * j + k],
                                                        device_id=(*chip, c), device_id_type=MESH))
        return cps

    def start(ins, outs, sems):
        for cp in first_copies(outs, sems):
            cp.start()

    def finish(ins, outs, sems):
        x, y, c = _place()
        chips = _chips(x, y)
        sibling = (x, y, 1 - c)
        passed = []
        for j in range(n):
            for k, chip in enumerate(chips):
                win = window(outs[j], j, 2 * chip[0] + chip[1], c)
                pltpu.make_async_remote_copy(win, win, sems[0].at[3 * j + k], sems[1].at[3 * j + k],
                                             device_id=(*chip, c), device_id_type=MESH).wait_recv()
                cp = pltpu.make_async_remote_copy(win, win, sems[2].at[3 * j + k], sems[3].at[3 * j + k],
                                                  device_id=sibling, device_id_type=MESH)
                cp.start()
                passed.append(cp)
        for j in range(n):
            for k, chip in enumerate(chips):
                win = window(outs[j], j, 2 * chip[0] + chip[1], 1 - c)
                pltpu.make_async_remote_copy(win, win, sems[2].at[3 * j + k], sems[3].at[3 * j + k],
                                             device_id=sibling, device_id_type=MESH).wait_recv()
        for cp in first_copies(outs, sems) + passed:
            cp.wait_send()

    return _Job(fulls, [jax.ShapeDtypeStruct(f.shape, f.dtype) for f in fulls], {j: j for j in range(n)},
                [pltpu.SemaphoreType.DMA((3 * n,)) for _ in range(4)], start, finish)


def _job_pair(grads):
    n = len(grads)

    def copies(ins, outs, sems):
        x, y, c = _place()
        return [pltpu.make_async_remote_copy(ins[j].at[1], outs[j], sems[0].at[j], sems[1].at[j],
                                             device_id=(x, y, 1 - c), device_id_type=MESH) for j in range(n)]

    def start(ins, outs, sems):
        for cp in copies(ins, outs, sems):
            cp.start()

    def finish(ins, outs, sems):
        for cp in copies(ins, outs, sems):
            cp.wait()

    return _Job(grads, [jax.ShapeDtypeStruct(g.shape[1:], g.dtype) for g in grads], {},
                [pltpu.SemaphoreType.DMA((n,)), pltpu.SemaphoreType.DMA((n,))], start, finish)


def _job_chips(halves, kinds):
    n = len(halves)
    shapes = [(h.shape[0], h.shape[1] // 4) if kinds[j] == "col" else (h.shape[0] // 4, h.shape[1])
              for j, h in enumerate(halves)]

    def part(ref, j, s):
        r, cc = shapes[j]
        if kinds[j] == "col":
            return ref.at[:, pl.ds(pl.multiple_of(s * cc, LANE), cc)]
        return ref.at[pl.ds(pl.multiple_of(s * r, 16), r), :]

    def copies(ins, outs, sems):
        x, y, c = _place()
        s_me = 2 * x + y
        local = [pltpu.make_async_copy(part(ins[j], j, s_me), outs[j].at[s_me], sems[2].at[j]) for j in range(n)]
        sends, recvs = [], []
        for j in range(n):
            for k, chip in enumerate(_chips(x, y)):
                s_peer = 2 * chip[0] + chip[1]
                sends.append(pltpu.make_async_remote_copy(part(ins[j], j, s_peer), outs[j].at[s_me],
                                                          sems[0].at[3 * j + k], sems[1].at[3 * j + k],
                                                          device_id=(*chip, c), device_id_type=MESH))
                dst = outs[j].at[s_peer]
                recvs.append(pltpu.make_async_remote_copy(dst, dst, sems[0].at[3 * j + k], sems[1].at[3 * j + k],
                                                          device_id=(*chip, c), device_id_type=MESH))
        return local, sends, recvs

    def start(ins, outs, sems):
        local, sends, _ = copies(ins, outs, sems)
        for cp in local + sends:
            cp.start()

    def finish(ins, outs, sems):
        local, sends, recvs = copies(ins, outs, sems)
        for cp in recvs:
            cp.wait_recv()
        for cp in sends:
            cp.wait_send()
        for cp in local:
            cp.wait()

    return _Job(halves, [jax.ShapeDtypeStruct((4,) + shapes[j], halves[j].dtype) for j in range(n)], {},
                [pltpu.SemaphoreType.DMA((3 * n,)), pltpu.SemaphoreType.DMA((3 * n,)), pltpu.SemaphoreType.DMA((n,))],
                start, finish)


def _job_join(pairs):
    n = len(pairs)

    def copies(outs, sems):
        x, y, c = _place()
        sends, recvs = [], []
        for j in range(n):
            sends.append(pltpu.make_async_remote_copy(outs[j].at[c], outs[j].at[c], sems[0].at[j], sems[1].at[j],
                                                      device_id=(x, y, 1 - c), device_id_type=MESH))
            theirs = outs[j].at[1 - c]
            recvs.append(pltpu.make_async_remote_copy(theirs, theirs, sems[0].at[j], sems[1].at[j],
                                                      device_id=(x, y, 1 - c), device_id_type=MESH))
        return sends, recvs

    def start(ins, outs, sems):
        for cp in copies(outs, sems)[0]:
            cp.start()

    def finish(ins, outs, sems):
        sends, recvs = copies(outs, sems)
        for cp in recvs:
            cp.wait_recv()
        for cp in sends:
            cp.wait_send()

    return _Job(pairs, [jax.ShapeDtypeStruct(p.shape, p.dtype) for p in pairs], {j: j for j in range(n)},
                [pltpu.SemaphoreType.DMA((n,)), pltpu.SemaphoreType.DMA((n,))], start, finish)


def _pair_sum(name, mine_other, got):
    _, r, cc = mine_other.shape
    tr = _rows_within(r, 3 * cc * mine_other.dtype.itemsize, (256, 128, 64, 32, 16))

    def body(a_ref, b_ref, o_ref):
        o_ref[...] = (a_ref[...].astype(F32) + b_ref[...].astype(F32)).astype(o_ref.dtype)

    return pl.pallas_call(
        body, name=name, grid=(r // tr,),
        in_specs=[pl.BlockSpec((None, tr, cc), lambda i: (0, i, 0)), pl.BlockSpec((tr, cc), lambda i: (i, 0))],
        out_specs=pl.BlockSpec((tr, cc), lambda i: (i, 0)),
        out_shape=jax.ShapeDtypeStruct((r, cc), mine_other.dtype), compiler_params=_params("parallel"),
    )(mine_other, got)


def _sum_slots(name, parts, into_slot=None):
    n, r, cc = parts.shape
    tr = _rows_within(r, cc * (n * parts.dtype.itemsize + 4), (256, 128, 64, 32, 16, 8))

    def body(*refs):
        p_ref, o_ref = refs[-2:]
        acc = p_ref[0].astype(F32)
        for s in range(1, n):
            acc = acc + p_ref[s].astype(F32)
        o_ref[...] = acc

    in_spec = pl.BlockSpec((n, tr, cc), lambda i, *_: (0, i, 0))
    if into_slot is None:
        return pl.pallas_call(
            body, name=name, grid=(r // tr,), in_specs=[in_spec], out_specs=pl.BlockSpec((tr, cc), lambda i: (i, 0)),
            out_shape=jax.ShapeDtypeStruct((r, cc), F32), compiler_params=_params("parallel"),
        )(parts)
    grid_spec = pltpu.PrefetchScalarGridSpec(
        num_scalar_prefetch=1, grid=(r // tr,), in_specs=[in_spec],
        out_specs=pl.BlockSpec((None, tr, cc), lambda i, c_ref: (c_ref[0], i, 0)))
    return pl.pallas_call(body, name=name, grid_spec=grid_spec, out_shape=jax.ShapeDtypeStruct((2, r, cc), F32),
                          compiler_params=_params("arbitrary"))(into_slot, parts)


def _adamw_math(w, g, m, v):
    m = ADAM_B1 * m + (1.0 - ADAM_B1) * g
    v = ADAM_B2 * v + (1.0 - ADAM_B2) * jnp.square(g)
    m_hat = m / (1.0 - ADAM_B1 ** ADAM_STEP)
    v_hat = v / (1.0 - ADAM_B2 ** ADAM_STEP)
    delta = -ADAM_LR * (m_hat / (jnp.sqrt(v_hat) + ADAM_EPS) + ADAM_WD * w)
    return delta, m, v


def _adamw_layer(name, layer, g_pair, kind, w, m, v, prev):
    n_layers, r, cc = w.shape
    if prev is None:
        prev = tuple(lax.empty(w.shape, F32) for _ in range(4))
    if kind == "col":
        g = g_pair.reshape(r, cc)
        tr = _rows_within(r, 8 * cc * 4)
        grid = (r // tr,)
        g_spec = pl.BlockSpec((tr, cc), lambda i: (i, 0))
        blk = pl.BlockSpec((None, tr, cc), lambda i: (layer, i, 0))
    else:
        g = g_pair
        tr = _rows_within(r, 4 * cc * 4)
        grid = (r // tr, 2)
        g_spec = pl.BlockSpec((None, tr, cc // 2), lambda i, h: (h, i, 0))
        blk = pl.BlockSpec((None, tr, cc // 2), lambda i, h: (layer, i, h))

    def body(g_ref, w_ref, m_ref, v_ref, *rest):
        og_ref, od_ref, om_ref, ov_ref = rest[4:]
        gv = g_ref[...]
        delta, m2, v2 = _adamw_math(w_ref[...], gv, m_ref[...], v_ref[...])
        og_ref[...] = gv
        od_ref[...] = delta
        om_ref[...] = m2
        ov_ref[...] = v2

    return pl.pallas_call(
        body, name=name, grid=grid,
        in_specs=[g_spec, blk, blk, blk] + [HBM_SPEC] * 4,
        out_specs=[blk] * 4, out_shape=[jax.ShapeDtypeStruct(w.shape, F32)] * 4,
        input_output_aliases={4: 0, 5: 1, 6: 2, 7: 3}, compiler_params=_params(*(["parallel"] * len(grid))),
    )(g, w, m, v, *prev)


def _adamw_small(name, g, w, m, v):
    def body(g_ref, w_ref, m_ref, v_ref, od_ref, om_ref, ov_ref):
        delta, m2, v2 = _adamw_math(w_ref[...], g_ref[...], m_ref[...], v_ref[...])
        od_ref[...] = delta
        om_ref[...] = m2
        ov_ref[...] = v2

    return pl.pallas_call(body, name=name, out_shape=[jax.ShapeDtypeStruct(w.shape, F32)] * 3)(g, w, m, v)


def _adamw_ada(name, c16, dmod16, w, m, v):
    n_layers, d, cols = w.shape
    tr = _rows_within(d, 7 * cols * 4, (256, 128))
    blk = pl.BlockSpec((None, tr, cols), lambda l, i: (l, i, 0))

    def body(c_ref, dm_ref, w_ref, m_ref, v_ref, og_ref, od_ref, om_ref, ov_ref):
        gv = lax.dot_general(c_ref[...], dm_ref[...], (((0,), (0,)), ((), ())), preferred_element_type=F32)
        delta, m2, v2 = _adamw_math(w_ref[...], gv, m_ref[...], v_ref[...])
        og_ref[...] = gv
        od_ref[...] = delta
        om_ref[...] = m2
        ov_ref[...] = v2

    return pl.pallas_call(
        body, name=name, grid=(n_layers, d // tr),
        in_specs=[pl.BlockSpec((16, tr), lambda l, i: (0, i)), pl.BlockSpec((None, 16, cols), lambda l, i: (l, 0, 0)),
                  blk, blk, blk],
        out_specs=[blk] * 4, out_shape=[jax.ShapeDtypeStruct(w.shape, F32)] * 4,
        compiler_params=_params("parallel", "parallel"),
    )(c16, dmod16, w, m, v)


def kernel(x, c, w_ada, b_ada, w_in, sink, w_dw, conv_ln_g, conv_ln_b, w_oa, w_ob, w_out, ln1_g, ln1_b, w_gu, w_down, ln2_g, ln2_b, loss_target, m_w_ada, m_b_ada, m_w_in, m_sink, m_w_dw, m_conv_ln_g, m_conv_ln_b, m_w_oa, m_w_ob, m_w_out, m_ln1_g, m_ln1_b, m_w_gu, m_w_down, m_ln2_g, m_ln2_b, v_w_ada, v_b_ada, v_w_in, v_sink, v_w_dw, v_conv_ln_g, v_conv_ln_b, v_w_oa, v_w_ob, v_w_out, v_ln1_g, v_ln1_b, v_w_gu, v_w_down, v_ln2_g, v_ln2_b):
    seq, d = x.shape[1], x.shape[2]
    n_layers = w_in.shape[0]
    d_in = 4 * w_in.shape[2]
    d_ff = 4 * w_down.shape[1]
    hq = d // HEAD_DIM
    dkv = (hq // GQA_GROUP) * HEAD_DIM
    off_k, off_v, off_ga, off_gb = d, d + dkv, d + 2 * dkv, 2 * d + 2 * dkv
    off_gta, off_gtb = 3 * d + 2 * dkv, 4 * d + 2 * dkv
    assert d_in == 5 * d + 2 * dkv and seq % ROW_TILE == 0 and seq >= BAND
    alpha = (2.0 * n_layers) ** 0.25

    xi, yi, ci = _place()
    chip = 2 * xi + yi
    batch = 4 * xi + 2 * yi + ci
    c_idx = jnp.reshape(ci, (1,)).astype(jnp.int32)
    x2 = x[0]
    target = loss_target[0]

    c_act = jax.nn.silu(c)
    c_all = _gather_small("gather_c", c_act, ALL_DEVICES).reshape(8, d)
    c16 = jnp.concatenate([c_all, jnp.zeros((8, d), F32)], axis=0).astype(BF16)
    mod_cols = [_mm(f"mod_{l}", c16, w_ada, "nn", F32, b_layer=l) for l in range(n_layers)]
    mod_all = _gather_small("gather_mod", jnp.stack(mod_cols), SAME_CORE_CHIPS)
    mod = lax.dynamic_index_in_dim(mod_all, batch, axis=2, keepdims=False)
    mod = jnp.transpose(mod, (1, 0, 2)).reshape(n_layers, N_MOD * d) + b_ada
    mod = mod.reshape(n_layers, N_MOD, 1, d)
    sh_a, sc_a, gt_a, sh_f, sc_f, gt_f = (mod[:, j] for j in range(N_MOD))

    pos = jnp.arange(seq, dtype=F32)
    inv_freq = ROPE_THETA ** (-jnp.arange(0, ROPE_DIM, 2, dtype=F32) / ROPE_DIM)
    ang = pos[:, None] * inv_freq[None, :]
    cos, sin = jnp.cos(ang), jnp.sin(ang)
    half = ROPE_DIM // 2
    rest = HEAD_DIM - ROPE_DIM
    t_cs = jnp.concatenate([cos, cos, jnp.ones((seq, rest), F32)], axis=1)
    t_up = jnp.concatenate([-sin, jnp.zeros((seq, rest + half), F32)], axis=1)
    t_dn = jnp.concatenate([jnp.zeros((seq, half), F32), sin, jnp.zeros((seq, rest), F32)], axis=1)

    def rope(t, cs, up, dn):
        w = t.shape[1]
        reps = (1, w // HEAD_DIM)
        return (t * jnp.tile(cs, reps) + pltpu.roll(t, w - half, 1) * jnp.tile(up, reps)
                + pltpu.roll(t, half, 1) * jnp.tile(dn, reps))

    def rope_t(dt, cs, up, dn):
        w = dt.shape[1]
        reps = (1, w // HEAD_DIM)
        return (dt * jnp.tile(cs, reps) + pltpu.roll(dt * jnp.tile(up, reps), half, 1)
                + pltpu.roll(dt * jnp.tile(dn, reps), w - half, 1))

    tables = [(t_cs, 0, HEAD_DIM), (t_up, 0, HEAD_DIM), (t_dn, 0, HEAD_DIM)]

    kinds = ("col", "col", "row", "row", "row", "row")
    big_weights = (w_in, w_gu, w_oa, w_ob, w_out, w_down)
    chip_idx = jnp.reshape(chip, (1,)).astype(jnp.int32)
    shard_shapes = [w.shape[1:] for w in big_weights]
    fulls = [[_cast_into(f"cast_w_{l}_{j}", w, l, kinds[j], chip_idx) for j, w in enumerate(big_weights)]
             for l in range(n_layers)]

    def gather_job(l, which):
        return _job_gather([fulls[l][j] for j in which], [shard_shapes[j] for j in which], [kinds[j] for j in which])

    ride_in_proj, ride_ffn_up, ride_attn_out, ride_conv_out, ride_mix_out, ride_ffn_down = (0,), (1,), (2,), (3,), (4,), (5,)
    gathered = [_run_jobs("gather_w_0", [gather_job(0, range(6))])[0]] + [[None] * 6 for _ in range(1, n_layers)]
    w_dw_all = _gather_small("gather_dw", w_dw, SAME_CORE_CHIPS)
    w_dw_full = jnp.transpose(w_dw_all, (1, 2, 0, 3)).reshape(n_layers, CONV_WIDTH, d)
    w_dw32 = jnp.pad(w_dw_full, ((0, 0), (0, 32 - CONV_WIDTH), (0, 0)))
    sink_b = jnp.broadcast_to(sink[:, :, None], (n_layers, hq, LANE))

    def vec(a, l):
        return a[l][None, :]

    def res_ln(xprev, y, gt, g, b, scn, shn):
        xn = _ln(alpha * xprev + (1.0 + gt) * y, g, b)
        return xn, xn * (1.0 + scn) + shn

    def merge(ya, yb, ga, gb):
        return _sigmoid(ga) * ya + _sigmoid(gb) * yb

    def swiglu(gate, up):
        return gate * _sigmoid(gate) * up

    h = _rowwise("modulate_in", lambda xv, sc, sh: xv * (1.0 + sc) + sh, seq, [(x2, 0, d)], [sc_a[0], sh_a[0]],
                 [(d, BF16)])
    xprev = x2
    saved = []
    for l in range(n_layers):
        wi, wg, woa, wob, wout, wdn = gathered[l]
        nxt = l + 1 < n_layers

        def mm_carrying(name, a_, b_, which):
            if not nxt:
                return _mm(name, a_, b_, "nn", F32)
            res, (got,) = _mm(name, a_, b_, "nn", F32, jobs=[gather_job(l + 1, which)])
            for j, arr in zip(which, got):
                gathered[l + 1][j] = arr
            return res

        z = mm_carrying(f"in_proj_{l}", h, wi, ride_in_proj)
        qr, kr, vb = _rowwise(
            f"qkv_prep_{l}", lambda q, k, v, cs, up, dn: (rope(q, cs, up, dn), rope(k, cs, up, dn), v), seq,
            [(z, 0, d), (z, off_k, dkv), (z, off_v, dkv)] + tables, [], [(d, BF16), (dkv, BF16), (dkv, BF16)])
        att = _attn_fwd(f"attn_{l}", qr, kr, vb, sink_b[l])
        y_a = mm_carrying(f"attn_out_{l}", att, woa, ride_attn_out)
        u = _rowwise(f"glu_{l}", lambda a, b: a * _sigmoid(b), seq, [(z, off_ga, d), (z, off_gb, d)], [], [(d, F32)])
        u2, cv = _conv_fwd(f"conv_{l}", u, w_dw32[l], vec(conv_ln_g, l), vec(conv_ln_b, l))
        y_b = mm_carrying(f"conv_out_{l}", cv, wob, ride_conv_out)
        mg = _rowwise(f"merge_{l}", merge, seq, [(y_a, 0, d), (y_b, 0, d), (z, off_gta, d), (z, off_gtb, d)], [],
                      [(d, BF16)])
        o = mm_carrying(f"mix_out_{l}", mg, wout, ride_mix_out)
        x1, h2 = _rowwise(f"res_ln1_{l}", res_ln, seq, [(xprev, 0, d), (o, 0, d)],
                          [gt_a[l], vec(ln1_g, l), vec(ln1_b, l), sc_f[l], sh_f[l]], [(d, F32), (d, BF16)])
        gu = mm_carrying(f"ffn_up_{l}", h2, wg, ride_ffn_up)
        f = _rowwise(f"swiglu_{l}", swiglu, seq, [(gu, 0, d_ff), (gu, d_ff, d_ff)], [], [(d_ff, BF16)])
        ffn = mm_carrying(f"ffn_down_{l}", f, wdn, ride_ffn_down)
        saved.append(dict(xprev=xprev, h=h, z=z, qr=qr, kr=kr, vb=vb, att=att, u=u, u2=u2, cv=cv, y_a=y_a, y_b=y_b,
                          mg=mg, o=o, x1=x1, h2=h2, gu=gu, f=f, ffn=ffn))
        if l + 1 < n_layers:
            xprev, h = _rowwise(f"res_ln2_{l}", res_ln, seq, [(x1, 0, d), (ffn, 0, d)],
                                [gt_f[l], vec(ln2_g, l), vec(ln2_b, l), sc_a[l + 1], sh_a[l + 1]], [(d, F32), (d, BF16)])

    def res_ln_bwd(xp, y, dxn, dh, gt, g, b, scn, shn):
        _, vjp = jax.vjp(res_ln, xp, y, gt, g, b, scn, shn)
        return vjp((dxn, dh))

    def last_ln_bwd(xp, y, tgt, gt, g, b):
        def head(xp_, y_, gt_, g_, b_):
            return _ln(alpha * xp_ + (1.0 + gt_) * y_, g_, b_)
        out, vjp = jax.vjp(head, xp, y, gt, g, b)
        err = out - tgt
        loss = 0.5 * jnp.sum(jnp.sum(err * err, axis=-1, keepdims=True) / d, axis=0, keepdims=True)
        return vjp(err / d) + (jnp.broadcast_to(loss, (1, LANE)),)

    def merge_bwd(dmg, ya, yb, ga, gb):
        _, vjp = jax.vjp(merge, ya, yb, ga, gb)
        dya, dyb, dga, dgb = vjp(dmg)
        return dya, dyb, jnp.concatenate([dga, dgb], axis=1)

    def swiglu_bwd(df, gate, up):
        _, vjp = jax.vjp(swiglu, gate, up)
        return jnp.concatenate(vjp(df), axis=1)

    vec_d = ((1, d), F32)
    small = [None] * n_layers
    big = None
    loss_part = None
    dxn = dh = None
    pending = None
    stacks = ((w_in, m_w_in, v_w_in), (w_gu, m_w_gu, v_w_gu), (w_oa, m_w_oa, v_w_oa), (w_ob, m_w_ob, v_w_ob),
              (w_out, m_w_out, v_w_out), (w_down, m_w_down, v_w_down))

    def adamw_all(layer, full, prev):
        return [_adamw_layer(f"adamw_{layer}_{j}", layer, full[j], kinds[j], *stacks[j], None if prev is None else prev[j])
                for j in range(6)]

    for l in reversed(range(n_layers)):
        sv = saved[l]
        wi, wg, woa, wob, wout, wdn = gathered[l]
        ln2 = [gt_f[l], vec(ln2_g, l), vec(ln2_b, l)]
        if l + 1 == n_layers:
            dx1, dffn, d_gtf, d_g2, d_b2, loss_part = _rowwise(
                "last_ln_bwd", last_ln_bwd, seq, [(sv["x1"], 0, d), (sv["ffn"], 0, d), (target, 0, d)], ln2,
                [(d, F32), (d, BF16)], [vec_d, vec_d, vec_d, ((1, LANE), F32)])
            d_sca_next = d_sha_next = None
        else:
            dx1, dffn, d_gtf, d_g2, d_b2, d_sca_next, d_sha_next = _rowwise(
                f"res_ln2_bwd_{l}", res_ln_bwd, seq, [(sv["x1"], 0, d), (sv["ffn"], 0, d), (dxn, 0, d), (dh, 0, d)],
                ln2 + [sc_a[l + 1], sh_a[l + 1]], [(d, F32), (d, BF16)], [vec_d] * 5)
            small[l + 1]["sc_a"], small[l + 1]["sh_a"] = d_sca_next, d_sha_next
        def riding(make_job, *job_args):
            return [] if pending is None else [make_job(*job_args)]

        def unpack(res):
            return res if pending is not None else (res, [None])

        df, (got,) = unpack(_mm(f"ffn_down_dx_{l}", dffn, wdn, "nt", F32, jobs=riding(_job_pair, pending)))
        halves = None if pending is None else [
            _pair_sum(f"pair_sum_{l + 1}_{j}", pending[j], got[j]) for j in range(6)]
        g_down = _mm(f"ffn_down_dw_{l}", sv["f"], dffn, "tn", BF16, split="cols", c_idx=c_idx)
        dgu = _rowwise(f"swiglu_bwd_{l}", swiglu_bwd, seq, [(df, 0, d_ff), (sv["gu"], 0, d_ff), (sv["gu"], d_ff, d_ff)],
                       [], [(2 * d_ff, BF16)])
        dh2, (parts_in,) = unpack(_mm(f"ffn_up_dx_{l}", dgu, wg, "nt", F32,
                                      jobs=riding(lambda: _job_chips(halves[:1], kinds[:1]))))
        g_gu, (parts_gu,) = unpack(_mm(f"ffn_up_dw_{l}", sv["h2"], dgu, "tn", BF16, split="rows", c_idx=c_idx,
                                       jobs=riding(lambda: _job_chips(halves[1:2], kinds[1:2]))))
        dxp, d_o, d_gta, d_g1, d_b1, d_scf, d_shf = _rowwise(
            f"res_ln1_bwd_{l}", res_ln_bwd, seq, [(sv["xprev"], 0, d), (sv["o"], 0, d), (dx1, 0, d), (dh2, 0, d)],
            [gt_a[l], vec(ln1_g, l), vec(ln1_b, l), sc_f[l], sh_f[l]], [(d, F32), (d, BF16)], [vec_d] * 5)
        dmg = _mm(f"mix_out_dx_{l}", d_o, wout, "nt", F32)
        g_out = _mm(f"mix_out_dw_{l}", sv["mg"], d_o, "tn", BF16, split="cols", c_idx=c_idx)
        z = sv["z"]
        dya, dyb, d_gates = _rowwise(
            f"merge_bwd_{l}", merge_bwd, seq,
            [(dmg, 0, d), (sv["y_a"], 0, d), (sv["y_b"], 0, d), (z, off_gta, d), (z, off_gtb, d)], [],
            [(d, BF16), (d, BF16), (2 * d, BF16)])
        d_att = _mm(f"attn_out_dx_{l}", dya, woa, "nt", BF16)
        g_oa = _mm(f"attn_out_dw_{l}", sv["att"], dya, "tn", BF16, split="cols", c_idx=c_idx)
        d_cv = _mm(f"conv_out_dx_{l}", dyb, wob, "nt", F32)
        g_ob = _mm(f"conv_out_dw_{l}", sv["cv"], dyb, "tn", BF16, split="cols", c_idx=c_idx)
        du2, d_wdw, d_cg, d_cb = _conv_bwd_a(f"conv_bwd_a_{l}", sv["u"], sv["u2"], d_cv, vec(conv_ln_g, l),
                                             vec(conv_ln_b, l))
        d_glu = _conv_bwd_b(f"conv_bwd_b_{l}", du2, z, off_ga, off_gb, w_dw32[l])
        dqr, dkr, dvb, d_sink = _attn_bwd(f"attn_bwd_{l}", sv["qr"], sv["kr"], sv["vb"], sink_b[l], d_att)
        d_qkv = _rowwise(
            f"qkv_bwd_{l}",
            lambda dq_, dk_, dv_, cs, up, dn: jnp.concatenate([rope_t(dq_, cs, up, dn), rope_t(dk_, cs, up, dn), dv_], axis=1),
            seq, [(dqr, 0, d), (dkr, 0, dkv), (dvb, 0, dkv)] + tables, [], [(d + 2 * dkv, BF16)])
        dz = jnp.concatenate([d_qkv, d_glu, d_gates], axis=1)
        dh, (parts_rest,) = unpack(_mm(f"in_proj_dx_{l}", dz, wi, "nt", F32,
                                       jobs=riding(lambda: _job_chips(halves[2:], kinds[2:]))))
        reduced = None if pending is None else [
            _sum_slots(f"sum_chips_{l + 1}_{j}", p, into_slot=c_idx) for j, p in enumerate(parts_in + parts_gu + parts_rest)]
        g_in, (full,) = unpack(_mm(f"in_proj_dw_{l}", sv["h"], dz, "tn", BF16, split="rows", c_idx=c_idx,
                                   jobs=riding(lambda: _job_join(reduced))))
        if pending is not None:
            big = adamw_all(l + 1, full, big)
        dxn = dxp
        small[l] = dict(gt_a=d_gta, sh_f=d_shf, sc_f=d_scf, gt_f=d_gtf, ln1_g=d_g1, ln1_b=d_b1, ln2_g=d_g2, ln2_b=d_b2,
                        conv_ln_g=d_cg, conv_ln_b=d_cb, sink=d_sink[:, :1].reshape(1, hq), w_dw=d_wdw[:CONV_WIDTH])
        pending = [g_in, g_gu, g_oa, g_ob, g_out, g_down]

    (got,) = _run_jobs("rs_pair_0", [_job_pair(pending)])
    halves = [_pair_sum(f"pair_sum_0_{j}", pending[j], got[j]) for j in range(6)]
    (parts,) = _run_jobs("rs_chips_0", [_job_chips(halves, kinds)])
    reduced = [_sum_slots(f"sum_chips_0_{j}", parts[j], into_slot=c_idx) for j in range(6)]
    (full,) = _run_jobs("rs_join_0", [_job_join(reduced)])
    big = adamw_all(0, full, big)

    grad_x, d_sca0, d_sha0 = _rowwise(
        "modulate_in_bwd", lambda xv, dhv, dxv, sc: (dxv + dhv * (1.0 + sc), jnp.sum(dhv * xv, axis=0, keepdims=True),
                                                     jnp.sum(dhv, axis=0, keepdims=True)),
        seq, [(x2, 0, d), (dh, 0, d), (dxn, 0, d)], [sc_a[0]], [(d, F32)], [vec_d, vec_d])
    small[0]["sc_a"], small[0]["sh_a"] = d_sca0, d_sha0

    order = ("sh_a", "sc_a", "gt_a", "sh_f", "sc_f", "gt_f", "conv_ln_g", "conv_ln_b", "ln1_g", "ln1_b", "ln2_g", "ln2_b")
    rows = []
    for l in range(n_layers):
        rows += [small[l][k] for k in order]
        rows.append(jnp.pad(small[l]["sink"], ((0, 0), (0, d - hq))))
        rows.append(small[l]["w_dw"])
    rows.append(jnp.pad(loss_part, ((0, 0), (0, d - LANE))))
    n_small = sum(r.shape[0] for r in rows)
    pad_rows = (-n_small) % 8
    packed = jnp.concatenate(rows + [jnp.zeros((pad_rows, d), F32)], axis=0)
    everyone = _gather_small("gather_small_grads", packed, ALL_DEVICES)
    total = _sum_slots("sum_small_grads", everyone)
    per_layer = len(order) + 1 + CONV_WIDTH
    tot = total[:n_layers * per_layer].reshape(n_layers, per_layer, d)
    g_mod = tot[:, :N_MOD].reshape(n_layers, N_MOD * d)
    g_small = {k: tot[:, N_MOD + j] for j, k in enumerate(order[N_MOD:])}
    g_sink = tot[:, len(order), :hq]
    g_dw_full = tot[:, len(order) + 1:]
    cols_dw = w_dw.shape[2]
    g_dw = lax.dynamic_slice_in_dim(g_dw_full, chip * cols_dw, cols_dw, axis=2)
    loss = total[n_layers * per_layer, 0]

    d_mod_all = everyone[:, :n_layers * per_layer].reshape(8, n_layers, per_layer, d)[:, :, :N_MOD]
    d_mod_all = d_mod_all.reshape(8, n_layers, N_MOD * d)
    cols_ada = w_ada.shape[2]
    d_mod_mine = lax.dynamic_slice_in_dim(d_mod_all, chip * cols_ada, cols_ada, axis=2)
    dmod16 = jnp.concatenate([d_mod_mine, jnp.zeros_like(d_mod_mine)], axis=0)
    dmod16 = jnp.transpose(dmod16, (1, 0, 2)).astype(BF16)
    ada = _adamw_ada("adamw_ada", c16, dmod16, w_ada, m_w_ada, v_w_ada)

    def small_step(name, g, w, m, v):
        shp = w.shape
        g2, w2, m2, v2 = (a.reshape(-1, shp[-1]) for a in (g, w, m, v))
        return (g,) + tuple(a.reshape(shp) for a in _adamw_small(name, g2, w2, m2, v2))

    res = {
        "w_ada": ada,
        "b_ada": small_step("adamw_b_ada", g_mod, b_ada, m_b_ada, v_b_ada),
        "sink": small_step("adamw_sink", g_sink, sink, m_sink, v_sink),
        "w_dw": small_step("adamw_w_dw", g_dw, w_dw, m_w_dw, v_w_dw),
        "conv_ln_g": small_step("adamw_conv_ln_g", g_small["conv_ln_g"], conv_ln_g, m_conv_ln_g, v_conv_ln_g),
        "conv_ln_b": small_step("adamw_conv_ln_b", g_small["conv_ln_b"], conv_ln_b, m_conv_ln_b, v_conv_ln_b),
        "ln1_g": small_step("adamw_ln1_g", g_small["ln1_g"], ln1_g, m_ln1_g, v_ln1_g),
        "ln1_b": small_step("adamw_ln1_b", g_small["ln1_b"], ln1_b, m_ln1_b, v_ln1_b),
        "ln2_g": small_step("adamw_ln2_g", g_small["ln2_g"], ln2_g, m_ln2_g, v_ln2_g),
        "ln2_b": small_step("adamw_ln2_b", g_small["ln2_b"], ln2_b, m_ln2_b, v_ln2_b),
        "w_in": big[0], "w_gu": big[1], "w_oa": big[2], "w_ob": big[3], "w_out": big[4], "w_down": big[5],
    }
    names = ("w_ada", "b_ada", "w_in", "sink", "w_dw", "conv_ln_g", "conv_ln_b", "w_oa", "w_ob", "w_out", "ln1_g", "ln1_b",
             "w_gu", "w_down", "ln2_g", "ln2_b")
    outs = [loss, grad_x[None]]
    for field in range(4):
        outs += [res[k][field] for k in names]
    return tuple(outs)
```

```python
import functools
import math

import jax
import jax.numpy as jnp
from jax import lax
from jax.experimental import pallas as pl
from jax.experimental.pallas import tpu as pltpu

F32 = jnp.float32
BF16 = jnp.bfloat16
MESH = pl.DeviceIdType.MESH

HEAD_DIM = 128
GQA_GROUP = 4
WINDOW = 128
BLOCK = 128
BAND = 3 * BLOCK
ROPE_DIM = HEAD_DIM // 4
ROPE_THETA = 500000.0
CONV_WIDTH = 31
CONV_PAD = CONV_WIDTH // 2
CONV_HALO = 16
N_MOD = 6
LN_EPS = 1e-5
NEG_INF = -1e30
ADAM_LR = 0.001
ADAM_B1 = 0.9
ADAM_B2 = 0.999
ADAM_EPS = 1e-08
ADAM_WD = 0.01
ADAM_STEP = 10

LANE = 128
SUBLANES = 8
V7X_VMEM_LIMIT = 56 * 1024 * 1024
ROW_TILE = 256
CONV_ROWS = 32
CONV_LANES = 256

HBM_SPEC = pl.BlockSpec(memory_space=pltpu.HBM)


def _params(*sem):
    return pltpu.CompilerParams(dimension_semantics=sem, vmem_limit_bytes=V7X_VMEM_LIMIT)


def _pick(n, cands, even=False):
    for t in cands:
        if n % t == 0 and (not even or (n // t) % 2 == 0):
            return t
    raise ValueError(f"no tile for {n} in {cands}")


BLOCK_BUDGET = 10 * 1024 * 1024
MM_BLOCK_BUDGET = 40 * 1024 * 1024


def _rows_within(n_rows, bytes_per_row, cands=(256, 128, 64, 32, 16, 8)):
    fit = [t for t in cands if n_rows % t == 0]
    for t in fit:
        if t * bytes_per_row <= BLOCK_BUDGET:
            return t
    return fit[-1]


def _sigmoid(v):
    return jax.nn.sigmoid(v)


def _const_map(ndim):
    return lambda *_: (0,) * ndim


def _rowwise(name, fn, n_rows, row_ins, vec_ins, row_outs, vec_outs=()):
    per_row = sum(w * a.dtype.itemsize for a, _, w in row_ins) + sum(w * jnp.dtype(dt).itemsize for w, dt in row_outs)
    tr = _rows_within(n_rows, per_row, (ROW_TILE, 128, 64))
    in_specs, args, pieces = [], [], []
    for arr, off, width in row_ins:
        bw = math.gcd(off, width) if off else width
        assert bw % LANE == 0 and arr.shape[0] == n_rows
        pieces.append(width // bw)
        for p in range(width // bw):
            in_specs.append(pl.BlockSpec((tr, bw), functools.partial(lambda i, blk: (i, blk), blk=off // bw + p)))
            args.append(arr)
    for v in vec_ins:
        in_specs.append(pl.BlockSpec(v.shape, _const_map(v.ndim)))
        args.append(v)
    out_shape = [jax.ShapeDtypeStruct((n_rows, w), dt) for w, dt in row_outs]
    out_specs = [pl.BlockSpec((tr, w), lambda i: (i, 0)) for w, _ in row_outs]
    for shp, dt in vec_outs:
        out_shape.append(jax.ShapeDtypeStruct(shp, dt))
        out_specs.append(pl.BlockSpec(shp, _const_map(len(shp))))
    n_in, n_row_out = len(args), len(row_outs)

    def body(*refs):
        in_refs, out_refs = refs[:n_in], refs[n_in:]
        vals, k = [], 0
        for npc in pieces:
            ps = [in_refs[k + p][...] for p in range(npc)]
            k += npc
            vals.append(ps[0] if npc == 1 else jnp.concatenate(ps, axis=1))
        for _ in vec_ins:
            vals.append(in_refs[k][...])
            k += 1
        outs = fn(*vals)
        if not isinstance(outs, (tuple, list)):
            outs = (outs,)
        assert len(outs) == len(out_refs)
        for j in range(n_row_out):
            out_refs[j][...] = outs[j].astype(out_refs[j].dtype)
        if vec_outs:
            @pl.when(pl.program_id(0) == 0)
            def _():
                for j in range(n_row_out, len(out_refs)):
                    out_refs[j][...] = jnp.zeros(out_refs[j].shape, out_refs[j].dtype)
            for j in range(n_row_out, len(out_refs)):
                out_refs[j][...] += outs[j].astype(out_refs[j].dtype)

    res = pl.pallas_call(
        body, name=name, grid=(n_rows // tr,), in_specs=in_specs, out_specs=out_specs, out_shape=out_shape,
        compiler_params=_params("arbitrary"),
    )(*args)
    return res[0] if len(res) == 1 else res


def _mm(name, a, b, mode, out_dtype, b_layer=None, split=None, c_idx=None, jobs=()):
    bshape = b.shape[1:] if b_layer is not None else b.shape
    if mode == "nn":
        (m, k), (k2, n) = a.shape, bshape
        dims = (((1,), (0,)), ((), ()))
    elif mode == "nt":
        (m, k), (n, k2) = a.shape, bshape
        dims = (((1,), (1,)), ((), ()))
    else:
        (k, m), (k2, n) = a.shape, bshape
        dims = (((0,), (0,)), ((), ()))
    assert k == k2, (name, a.shape, b.shape)
    tm = _pick(m, (1024, 512, 256, 128, 16), even=(split == "rows"))
    tn = _pick(n, (1024, 512, 256, 128), even=(split == "cols"))
    out_bytes = jnp.dtype(out_dtype).itemsize
    b_bytes = b.dtype.itemsize

    def blocks_fit(t):
        acc = 0 if t == k else tm * tn * 4
        return 2 * (tm * t * a.dtype.itemsize + tn * t * b_bytes + tm * tn * out_bytes) + acc <= MM_BLOCK_BUDGET

    tk = next(t for t in (4096, 2816, 2048, 1408, 1024, 704, 512, 256, 128) if k % t == 0 and (blocks_fit(t) or t == 128))
    ni, nj, nk = m // tm, n // tn, k // tk

    if mode == "nn":
        a_spec = pl.BlockSpec((tm, tk), lambda i, j, kk, *_: (i, kk))
        b_blk, b_map = (tk, tn), (lambda i, j, kk: (kk, j))
    elif mode == "nt":
        a_spec = pl.BlockSpec((tm, tk), lambda i, j, kk, *_: (i, kk))
        b_blk, b_map = (tn, tk), (lambda i, j, kk: (j, kk))
    else:
        a_spec = pl.BlockSpec((tk, tm), lambda i, j, kk, *_: (kk, i))
        b_blk, b_map = (tk, tn), (lambda i, j, kk: (kk, j))
    if b_layer is None:
        b_spec = pl.BlockSpec(b_blk, lambda i, j, kk, *_: b_map(i, j, kk))
    else:
        b_spec = pl.BlockSpec((None,) + b_blk, lambda i, j, kk, *_: (b_layer,) + b_map(i, j, kk))

    if split is None:
        out_shape = jax.ShapeDtypeStruct((m, n), out_dtype)
        o_spec = pl.BlockSpec((tm, tn), lambda i, j, kk, *_: (i, j))
    elif split == "rows":
        out_shape = jax.ShapeDtypeStruct((2, m // 2, n), out_dtype)
        o_spec = pl.BlockSpec(
            (None, tm, tn), lambda i, j, kk, c_ref: (jnp.where(i // (ni // 2) == c_ref[0], 0, 1), i % (ni // 2), j))
    else:
        out_shape = jax.ShapeDtypeStruct((2, m, n // 2), out_dtype)
        o_spec = pl.BlockSpec(
            (None, tm, tn), lambda i, j, kk, c_ref: (jnp.where(j // (nj // 2) == c_ref[0], 0, 1), i, j % (nj // 2)))

    n_job_in = sum(len(jb.ins) for jb in jobs)
    n_job_out = sum(len(jb.out_shapes) for jb in jobs)
    n_acc = 0 if nk == 1 else 1

    def body(*refs):
        if split is not None:
            refs = refs[1:]
        a_ref, b_ref = refs[:2]
        job_ins = refs[2:2 + n_job_in]
        o_ref = refs[2 + n_job_in]
        job_outs = refs[3 + n_job_in:3 + n_job_in + n_job_out]
        scratch_refs = refs[3 + n_job_in + n_job_out:]
        cut = _job_refs(jobs, job_ins, job_outs, scratch_refs[n_acc:])
        i, j, kk = pl.program_id(0), pl.program_id(1), pl.program_id(2)

        if jobs:
            @pl.when((i == 0) & (j == 0) & (kk == 0))
            def _():
                for jb, parts in zip(jobs, cut):
                    jb.start(*parts)

        part = lax.dot_general(a_ref[...].astype(BF16), b_ref[...].astype(BF16), dims, preferred_element_type=F32)
        if nk == 1:
            o_ref[...] = part.astype(o_ref.dtype)
        else:
            acc_ref = scratch_refs[0]

            @pl.when(kk == 0)
            def _():
                acc_ref[...] = part

            @pl.when(kk > 0)
            def _():
                acc_ref[...] += part

            @pl.when(kk == nk - 1)
            def _():
                o_ref[...] = acc_ref[...].astype(o_ref.dtype)

        if jobs:
            @pl.when((i == ni - 1) & (j == nj - 1) & (kk == nk - 1))
            def _():
                for jb, parts in zip(jobs, cut):
                    jb.finish(*parts)

    scratch = ([] if nk == 1 else [pltpu.VMEM((tm, tn), F32)]) + [s for jb in jobs for s in jb.sems]
    params = _params(*(["arbitrary"] * 3 if jobs else ["parallel", "parallel", "arbitrary"]))
    in_specs = [a_spec, b_spec] + [HBM_SPEC] * n_job_in
    out_specs = [o_spec] + [HBM_SPEC] * n_job_out
    out_shapes = [out_shape] + [s for jb in jobs for s in jb.out_shapes]
    operands = [a, b] + [x for jb in jobs for x in jb.ins]
    n_pre = 0 if split is None else 1
    aliases = _job_aliases(jobs, n_pre + 2, 1)
    if split is None:
        res = pl.pallas_call(
            body, name=name, grid=(ni, nj, nk), in_specs=in_specs, out_specs=out_specs, out_shape=out_shapes,
            scratch_shapes=scratch, input_output_aliases=aliases, compiler_params=params,
        )(*operands)
    else:
        grid_spec = pltpu.PrefetchScalarGridSpec(
            num_scalar_prefetch=1, grid=(ni, nj, nk), in_specs=in_specs, out_specs=out_specs, scratch_shapes=scratch)
        res = pl.pallas_call(body, name=name, grid_spec=grid_spec, out_shape=out_shapes, input_output_aliases=aliases,
                             compiler_params=params)(c_idx, *operands)
    if not jobs:
        return res[0]
    return res[0], _job_results(jobs, res[1:])


def _attn_tile(seq):
    return _pick(seq, (256, 128))


def _heads_stacked(ref, b):
    return jnp.concatenate(
        [ref[b * BLOCK:(b + 1) * BLOCK, g * HEAD_DIM:(g + 1) * HEAD_DIM] for g in range(GQA_GROUP)], axis=0)


def _attn_scores(q_ref, k_ref, v_ref, sink_ref, kvh, i, b, tq, seq):
    rows = GQA_GROUP * BLOCK
    q0 = i * tq + b * BLOCK
    k_off = pl.multiple_of(jnp.clip(q0 - BLOCK, 0, seq - BAND), BLOCK)
    kw = k_ref[pl.ds(k_off, BAND), :]
    vw = v_ref[pl.ds(k_off, BAND), :]
    q_pos = q0 + (lax.broadcasted_iota(jnp.int32, (rows, BAND), 0) & (BLOCK - 1))
    k_pos = k_off + lax.broadcasted_iota(jnp.int32, (rows, BAND), 1)
    valid = jnp.abs(k_pos - q_pos) <= WINDOW
    qs = _heads_stacked(q_ref, b)
    s = lax.dot_general(qs, kw, (((1,), (1,)), ((), ())), preferred_element_type=F32) * (HEAD_DIM ** -0.5)
    s = jnp.where(valid, s, NEG_INF)
    sink = jnp.concatenate(
        [jnp.broadcast_to(sink_ref[pl.ds(kvh * GQA_GROUP + g, 1), :][:, :1], (BLOCK, 1)) for g in range(GQA_GROUP)], axis=0)
    m = jnp.maximum(jnp.max(s, axis=-1, keepdims=True), sink)
    p = jnp.exp(s - m)
    p_sink = jnp.exp(sink - m)
    denom = jnp.sum(p, axis=-1, keepdims=True) + p_sink
    return k_off, kw, vw, qs, p / denom, p_sink / denom


def _attn_fwd(name, qr, kr, vb, sink_b, jobs=()):
    seq, dq = qr.shape
    nkv = kr.shape[1] // HEAD_DIM
    tq = _attn_tile(seq)
    gw = GQA_GROUP * HEAD_DIM

    def body(q_ref, k_ref, v_ref, sink_ref, o_ref):
        kvh, i = pl.program_id(0), pl.program_id(1)
        for b in range(tq // BLOCK):
            _, _, vw, _, pn, _ = _attn_scores(q_ref, k_ref, v_ref, sink_ref, kvh, i, b, tq, seq)
            o = jnp.dot(pn.astype(BF16), vw, preferred_element_type=F32).astype(o_ref.dtype)
            for g in range(GQA_GROUP):
                o_ref[b * BLOCK:(b + 1) * BLOCK, g * HEAD_DIM:(g + 1) * HEAD_DIM] = o[g * BLOCK:(g + 1) * BLOCK]

    (att,), job_res = _call(
        name, body, (nkv, seq // tq),
        [
            pl.BlockSpec((tq, gw), lambda h, i: (i, h)),
            pl.BlockSpec((seq, HEAD_DIM), lambda h, i: (0, h)),
            pl.BlockSpec((seq, HEAD_DIM), lambda h, i: (0, h)),
            pl.BlockSpec(sink_b.shape, lambda h, i: (0, 0)),
        ],
        [pl.BlockSpec((tq, gw), lambda h, i: (i, h))], [jax.ShapeDtypeStruct((seq, dq), BF16)], [],
        (qr, kr, vb, sink_b), ("arbitrary", "arbitrary"), jobs)
    return (att, job_res) if jobs else att


def _attn_bwd(name, qr, kr, vb, sink_b, d_att, jobs=()):
    seq, dq = qr.shape
    dkv = kr.shape[1]
    nkv = dkv // HEAD_DIM
    tq = _attn_tile(seq)
    gw = GQA_GROUP * HEAD_DIM
    tn_dims = (((0,), (0,)), ((), ()))

    def body(q_ref, k_ref, v_ref, sink_ref, do_ref, dq_ref, dk_ref, dv_ref, dsink_ref):
        kvh, i = pl.program_id(0), pl.program_id(1)

        @pl.when(i == 0)
        def _():
            dk_ref[...] = jnp.zeros(dk_ref.shape, F32)
            dv_ref[...] = jnp.zeros(dv_ref.shape, F32)

        @pl.when((i == 0) & (kvh == 0))
        def _():
            dsink_ref[...] = jnp.zeros(dsink_ref.shape, F32)

        for b in range(tq // BLOCK):
            k_off, kw, vw, qs, pn, pn_sink = _attn_scores(q_ref, k_ref, v_ref, sink_ref, kvh, i, b, tq, seq)
            dos = _heads_stacked(do_ref, b)
            dp = lax.dot_general(dos, vw, (((1,), (1,)), ((), ())), preferred_element_type=F32)
            delta = jnp.sum(pn * dp, axis=-1, keepdims=True)
            ds = (pn * (dp - delta) * (HEAD_DIM ** -0.5)).astype(BF16)
            dqs = jnp.dot(ds, kw, preferred_element_type=F32)
            sink_term = pn_sink * delta
            for g in range(GQA_GROUP):
                dq_ref[b * BLOCK:(b + 1) * BLOCK, g * HEAD_DIM:(g + 1) * HEAD_DIM] = dqs[g * BLOCK:(g + 1) * BLOCK]
                d_sink = -jnp.sum(sink_term[g * BLOCK:(g + 1) * BLOCK], axis=0, keepdims=True)
                dsink_ref[pl.ds(kvh * GQA_GROUP + g, 1), :] += jnp.broadcast_to(d_sink, (1, LANE))
            dk_ref[pl.ds(k_off, BAND), :] += lax.dot_general(ds, qs, tn_dims, preferred_element_type=F32)
            dv_ref[pl.ds(k_off, BAND), :] += lax.dot_general(pn.astype(BF16), dos, tn_dims, preferred_element_type=F32)

    res, job_res = _call(
        name, body, (nkv, seq // tq),
        [
            pl.BlockSpec((tq, gw), lambda h, i: (i, h)),
            pl.BlockSpec((seq, HEAD_DIM), lambda h, i: (0, h)),
            pl.BlockSpec((seq, HEAD_DIM), lambda h, i: (0, h)),
            pl.BlockSpec(sink_b.shape, lambda h, i: (0, 0)),
            pl.BlockSpec((tq, gw), lambda h, i: (i, h)),
        ],
        [
            pl.BlockSpec((tq, gw), lambda h, i: (i, h)),
            pl.BlockSpec((seq, HEAD_DIM), lambda h, i: (0, h)),
            pl.BlockSpec((seq, HEAD_DIM), lambda h, i: (0, h)),
            pl.BlockSpec(sink_b.shape, lambda h, i: (0, 0)),
        ],
        [
            jax.ShapeDtypeStruct((seq, dq), F32),
            jax.ShapeDtypeStruct((seq, dkv), F32),
            jax.ShapeDtypeStruct((seq, dkv), F32),
            jax.ShapeDtypeStruct(sink_b.shape, F32),
        ],
        [], (qr, kr, vb, sink_b, d_att), ("arbitrary", "arbitrary"), jobs)
    return (res, job_res) if jobs else res


def _halo_specs(tr, width, n_rows):
    per, last = tr // CONV_HALO, n_rows // CONV_HALO - 1
    return [
        pl.BlockSpec((tr, width), lambda i: (i, 0)),
        pl.BlockSpec((CONV_HALO, width), lambda i: (jnp.maximum(i * per - 1, 0), 0)),
        pl.BlockSpec((CONV_HALO, width), lambda i: (jnp.minimum((i + 1) * per, last), 0)),
    ]


def _ext_scratch(tr, width):
    return pltpu.VMEM((SUBLANES, tr + 2 * CONV_HALO, width), F32)


def _fill_ext(ext_ref, main_ref, prev_ref, next_ref, n_steps, tr):
    i = pl.program_id(0)
    ext_ref[0, 0:CONV_HALO, :] = jnp.where(i > 0, prev_ref[...], 0.0)
    ext_ref[0, CONV_HALO:CONV_HALO + tr, :] = main_ref[...]
    ext_ref[0, CONV_HALO + tr:, :] = jnp.where(i < n_steps - 1, next_ref[...], 0.0)
    rows = tr + 2 * CONV_HALO - SUBLANES
    for p in range(1, SUBLANES):
        ext_ref[p, 0:rows, :] = ext_ref[0, p:p + rows, :]


def _tap(ext_ref, r0, t, cols):
    whole, phase = divmod(1 + t, SUBLANES)
    return ext_ref[phase, r0 + whole * SUBLANES:r0 + whole * SUBLANES + CONV_ROWS, cols]


def _conv_taps(ext_ref, w_ref, out_ref, tr, width, flip):
    cw = min(CONV_LANES, width)
    for cc in range(width // cw):
        cols = slice(cc * cw, (cc + 1) * cw)
        for rc in range(tr // CONV_ROWS):
            acc = jnp.zeros((CONV_ROWS, cw), F32)
            for t in range(CONV_WIDTH):
                wt = CONV_WIDTH - 1 - t if flip else t
                acc += _tap(ext_ref, rc * CONV_ROWS, t, cols) * w_ref[wt:wt + 1, cols]
            out_ref[rc * CONV_ROWS:(rc + 1) * CONV_ROWS, cols] = acc


def _ln(v, g, b):
    mu = jnp.mean(v, axis=-1, keepdims=True)
    vc = v - mu
    var = jnp.mean(vc * vc, axis=-1, keepdims=True)
    return vc * lax.rsqrt(var + LN_EPS) * g + b


def _conv_fwd(name, u, w32, ln_g, ln_b, jobs=()):
    n_rows, width = u.shape
    tr = min(ROW_TILE, n_rows)
    n_steps = n_rows // tr

    def body(main_ref, prev_ref, next_ref, w_ref, g_ref, b_ref, u2_ref, cv_ref, ext_ref):
        _fill_ext(ext_ref, main_ref, prev_ref, next_ref, n_steps, tr)
        _conv_taps(ext_ref, w_ref, u2_ref, tr, width, flip=False)
        u3 = _ln(u2_ref[...], g_ref[...], b_ref[...])
        cv_ref[...] = (u3 * _sigmoid(u3)).astype(cv_ref.dtype)

    vec = lambda a: pl.BlockSpec(a.shape, lambda i: (0, 0))
    res, job_res = _call(
        name, body, (n_steps,), _halo_specs(tr, width, n_rows) + [vec(w32), vec(ln_g), vec(ln_b)],
        [pl.BlockSpec((tr, width), lambda i: (i, 0))] * 2,
        [jax.ShapeDtypeStruct((n_rows, width), F32), jax.ShapeDtypeStruct((n_rows, width), BF16)],
        [_ext_scratch(tr, width)], (u, u, u, w32, ln_g, ln_b), ("arbitrary",), jobs)
    return (res, job_res) if jobs else res


def _conv_bwd_a(name, u, u2, d_cv, ln_g, ln_b, jobs=()):
    n_rows, width = u.shape
    tr = min(ROW_TILE // 2, n_rows)
    n_steps = n_rows // tr

    def body(main_ref, prev_ref, next_ref, u2_ref, dcv_ref, g_ref, b_ref, du2_ref, dw_ref, dg_ref, db_ref, ext_ref):
        @pl.when(pl.program_id(0) == 0)
        def _():
            dw_ref[...] = jnp.zeros(dw_ref.shape, F32)
            dg_ref[...] = jnp.zeros(dg_ref.shape, F32)
            db_ref[...] = jnp.zeros(db_ref.shape, F32)

        _fill_ext(ext_ref, main_ref, prev_ref, next_ref, n_steps, tr)

        def swish_ln(v, g, b):
            u3 = _ln(v, g, b)
            return u3 * _sigmoid(u3)

        _, vjp = jax.vjp(swish_ln, u2_ref[...], g_ref[...], b_ref[...])
        du2, dg, db = vjp(dcv_ref[...])
        du2_ref[...] = du2
        dg_ref[...] += dg
        db_ref[...] += db
        for cc in range(width // LANE):
            cols = slice(cc * LANE, (cc + 1) * LANE)
            for t0 in range(0, CONV_WIDTH, 16):
                taps = range(t0, min(t0 + 16, CONV_WIDTH))
                accs = {t: jnp.zeros((SUBLANES, LANE), F32) for t in taps}
                for rc in range(tr // CONV_ROWS):
                    r0 = rc * CONV_ROWS
                    d_blk = du2_ref[r0:r0 + CONV_ROWS, cols]
                    for t in taps:
                        prod = d_blk * _tap(ext_ref, r0, t, cols)
                        for q in range(CONV_ROWS // SUBLANES):
                            accs[t] = accs[t] + prod[q * SUBLANES:(q + 1) * SUBLANES]
                for t in taps:
                    dw_ref[t:t + 1, cols] += jnp.sum(accs[t], axis=0, keepdims=True)

    vec = lambda a: pl.BlockSpec(a.shape, lambda i: (0, 0))
    row = pl.BlockSpec((tr, width), lambda i: (i, 0))
    res, job_res = _call(
        name, body, (n_steps,), _halo_specs(tr, width, n_rows) + [row, row, vec(ln_g), vec(ln_b)],
        [row, pl.BlockSpec((32, width), lambda i: (0, 0)), vec(ln_g), vec(ln_b)],
        [jax.ShapeDtypeStruct((n_rows, width), F32), jax.ShapeDtypeStruct((32, width), F32),
         jax.ShapeDtypeStruct(ln_g.shape, F32), jax.ShapeDtypeStruct(ln_b.shape, F32)],
        [_ext_scratch(tr, width)], (u, u, u, u2, d_cv, ln_g, ln_b), ("arbitrary",), jobs)
    return (res, job_res) if jobs else res


def _conv_bwd_b(name, du2, z, off_a, off_b, w32, jobs=()):
    n_rows, width = du2.shape
    tr = min(ROW_TILE, n_rows)
    n_steps = n_rows // tr
    bw = math.gcd(math.gcd(off_a, off_b), width)
    npc = width // bw

    def body(*refs):
        main_ref, prev_ref, next_ref = refs[:3]
        a_refs, b_refs = refs[3:3 + npc], refs[3 + npc:3 + 2 * npc]
        w_ref, out_ref, ext_ref, du_ref = refs[3 + 2 * npc:]
        _fill_ext(ext_ref, main_ref, prev_ref, next_ref, n_steps, tr)
        _conv_taps(ext_ref, w_ref, du_ref, tr, width, flip=True)
        for p in range(npc):
            cols = slice(p * bw, (p + 1) * bw)
            du = du_ref[:, cols]
            sg = _sigmoid(b_refs[p][...])
            out_ref[:, p * bw:(p + 1) * bw] = (du * sg).astype(out_ref.dtype)
            out_ref[:, width + p * bw:width + (p + 1) * bw] = (du * a_refs[p][...] * sg * (1.0 - sg)).astype(out_ref.dtype)

    def piece(off, p):
        return pl.BlockSpec((tr, bw), functools.partial(lambda i, blk: (i, blk), blk=off // bw + p))

    in_specs = _halo_specs(tr, width, n_rows)
    in_specs += [piece(off_a, p) for p in range(npc)] + [piece(off_b, p) for p in range(npc)]
    in_specs.append(pl.BlockSpec(w32.shape, lambda i: (0, 0)))
    (d_glu,), job_res = _call(
        name, body, (n_steps,), in_specs, [pl.BlockSpec((tr, 2 * width), lambda i: (i, 0))],
        [jax.ShapeDtypeStruct((n_rows, 2 * width), BF16)], [_ext_scratch(tr, width), pltpu.VMEM((tr, width), F32)],
        (du2, du2, du2, *([z] * (2 * npc)), w32), ("arbitrary",), jobs)
    return (d_glu, job_res) if jobs else d_glu


def _place():
    return lax.axis_index("x"), lax.axis_index("y"), lax.axis_index("c")


def _flip(v, m):
    return 1 - v if m else v


def _gather_small(name, v, masks):
    varies = [any(m[a] for m in masks) for a in range(3)]
    n = len(masks) + 1

    def slot(pos):
        idx = 0
        for a in range(3):
            if varies[a]:
                idx = idx * 2 + pos[a]
        return idx

    def body(v_ref, o_ref, send_sems, recv_sems, local_sem):
        me = _place()
        mine = pltpu.make_async_copy(v_ref, o_ref.at[slot(me)], local_sem)
        mine.start()
        peers = [tuple(_flip(me[a], m[a]) for a in range(3)) for m in masks]
        sends = [pltpu.make_async_remote_copy(v_ref, o_ref.at[slot(me)], send_sems.at[k], recv_sems.at[k],
                                              device_id=peer, device_id_type=MESH) for k, peer in enumerate(peers)]
        for cp in sends:
            cp.start()
        for k, peer in enumerate(peers):
            pltpu.make_async_remote_copy(v_ref, o_ref.at[slot(peer)], send_sems.at[k], recv_sems.at[k],
                                         device_id=peer, device_id_type=MESH).wait_recv()
        for cp in sends:
            cp.wait_send()
        mine.wait()

    return pl.pallas_call(
        body, name=name, in_specs=[HBM_SPEC], out_specs=HBM_SPEC,
        out_shape=jax.ShapeDtypeStruct((n,) + v.shape, v.dtype),
        scratch_shapes=[pltpu.SemaphoreType.DMA((n - 1,)), pltpu.SemaphoreType.DMA((n - 1,)), pltpu.SemaphoreType.DMA(())],
    )(v)


ALL_DEVICES = [(mx, my, mc) for mx in (0, 1) for my in (0, 1) for mc in (0, 1)][1:]
SAME_CORE_CHIPS = [(1, 0, 0), (0, 1, 0), (1, 1, 0)]


def _chips(x, y):
    return [(1 - x, y), (x, 1 - y), (1 - x, 1 - y)]


def _cast_into(name, w, layer, kind, chip_idx):
    _, r, cc = w.shape
    tr = _rows_within(r, cc * 6)
    steps = r // tr
    if kind == "col":
        shape, o_spec = (r, 4 * cc), pl.BlockSpec((tr, cc), lambda i, s_ref: (i, s_ref[0]))
    else:
        shape, o_spec = (4 * r, cc), pl.BlockSpec((tr, cc), lambda i, s_ref: (s_ref[0] * steps + i, 0))

    def body(s_ref, w_ref, o_ref):
        o_ref[...] = w_ref[...].astype(o_ref.dtype)

    grid_spec = pltpu.PrefetchScalarGridSpec(
        num_scalar_prefetch=1, grid=(steps,),
        in_specs=[pl.BlockSpec((None, tr, cc), lambda i, s_ref: (layer, i, 0))], out_specs=o_spec)
    return pl.pallas_call(body, name=name, grid_spec=grid_spec, out_shape=jax.ShapeDtypeStruct(shape, BF16),
                          compiler_params=_params("arbitrary"))(chip_idx, w)


class _Job:
    def __init__(self, ins, out_shapes, aliases, sems, start, finish):
        self.ins, self.out_shapes, self.aliases, self.sems = list(ins), list(out_shapes), dict(aliases), list(sems)
        self.start, self.finish = start, finish


def _job_refs(jobs, in_refs, out_refs, sem_refs):
    cut, i, o, s = [], 0, 0, 0
    for jb in jobs:
        cut.append((in_refs[i:i + len(jb.ins)], out_refs[o:o + len(jb.out_shapes)], sem_refs[s:s + len(jb.sems)]))
        i, o, s = i + len(jb.ins), o + len(jb.out_shapes), s + len(jb.sems)
    return cut


def _job_aliases(jobs, first_in, first_out):
    aliases, i, o = {}, first_in, first_out
    for jb in jobs:
        for a, b in jb.aliases.items():
            aliases[i + a] = o + b
        i, o = i + len(jb.ins), o + len(jb.out_shapes)
    return aliases


def _run_jobs(name, jobs):
    n_in = sum(len(jb.ins) for jb in jobs)
    n_out = sum(len(jb.out_shapes) for jb in jobs)

    def body(*refs):
        cut = _job_refs(jobs, refs[:n_in], refs[n_in:n_in + n_out], refs[n_in + n_out:])
        for jb, parts in zip(jobs, cut):
            jb.start(*parts)
        for jb, parts in zip(jobs, cut):
            jb.finish(*parts)

    res = pl.pallas_call(
        body, name=name, in_specs=[HBM_SPEC] * n_in, out_specs=[HBM_SPEC] * n_out,
        out_shape=[s for jb in jobs for s in jb.out_shapes], input_output_aliases=_job_aliases(jobs, 0, 0),
        scratch_shapes=[s for jb in jobs for s in jb.sems],
    )(*[a for jb in jobs for a in jb.ins])
    return _job_results(jobs, res)


def _job_results(jobs, flat):
    out, o = [], 0
    for jb in jobs:
        out.append(list(flat[o:o + len(jb.out_shapes)]))
        o += len(jb.out_shapes)
    return out


def _call(name, body, grid, in_specs, out_specs, out_shape, scratch, operands, sem, jobs=()):
    n_in, n_out, n_scr = len(in_specs), len(out_specs), len(scratch)
    n_job_in = sum(len(jb.ins) for jb in jobs)
    n_job_out = sum(len(jb.out_shapes) for jb in jobs)

    def full_body(*refs):
        ins, job_ins = refs[:n_in], refs[n_in:n_in + n_job_in]
        rest = refs[n_in + n_job_in:]
        outs, job_outs = rest[:n_out], rest[n_out:n_out + n_job_out]
        scr, sems = rest[n_out + n_job_out:n_out + n_job_out + n_scr], rest[n_out + n_job_out + n_scr:]
        cut = _job_refs(jobs, job_ins, job_outs, sems)
        ids = [pl.program_id(a) for a in range(len(grid))]
        if jobs:
            @pl.when(functools.reduce(lambda p, q: p & q, [i == 0 for i in ids]))
            def _():
                for jb, parts in zip(jobs, cut):
                    jb.start(*parts)
        body(*ins, *outs, *scr)
        if jobs:
            @pl.when(functools.reduce(lambda p, q: p & q, [i == g - 1 for i, g in zip(ids, grid)]))
            def _():
                for jb, parts in zip(jobs, cut):
                    jb.finish(*parts)

    res = pl.pallas_call(
        full_body, name=name, grid=grid, in_specs=list(in_specs) + [HBM_SPEC] * n_job_in,
        out_specs=list(out_specs) + [HBM_SPEC] * n_job_out,
        out_shape=list(out_shape) + [s for jb in jobs for s in jb.out_shapes],
        scratch_shapes=list(scratch) + [s for jb in jobs for s in jb.sems],
        input_output_aliases=_job_aliases(jobs, n_in, n_out), compiler_params=_params(*sem),
    )(*operands, *[a for jb in jobs for a in jb.ins])
    return list(res[:n_out]), _job_results(jobs, res[n_out:])


def _job_gather(fulls, shapes, kinds):
    n = len(fulls)
    for r, _ in shapes:
        assert r % 32 == 0

    def window(o_ref, j, s, h):
        r, cc = shapes[j]
        hr = r // 2
        if kinds[j] == "col":
            return o_ref.at[pl.ds(pl.multiple_of(h * hr, 16), hr), pl.ds(pl.multiple_of(s * cc, LANE), cc)]
        return o_ref.at[pl.ds(pl.multiple_of(s * r + h * hr, 16), hr), :]

    def first_copies(outs, sems):
        x, y, c = _place()
        cps = []
        for j in range(n):
            mine = window(outs[j], j, 2 * x + y, c)
            for k, chip in enumerate(_chips(x, y)):
                cps.append(pltpu.make_async_remote_copy(mine, mine, sems[0].at[3 * j + k], sems[1].at[3 * j + k],
                                                        device_id=(*chip, c), device_id_type=MESH))
        return cps

    def start(ins, outs, sems):
        for cp in first_copies(outs, sems):
            cp.start()

    def finish(ins, outs, sems):
        x, y, c = _place()
        chips = _chips(x, y)
        sibling = (x, y, 1 - c)
        passed = []
        for j in range(n):
            for k, chip in enumerate(chips):
                win = window(outs[j], j, 2 * chip[0] + chip[1], c)
                pltpu.make_async_remote_copy(win, win, sems[0].at[3 * j + k], sems[1].at[3 * j + k],
                                             device_id=(*chip, c), device_id_type=MESH).wait_recv()
                cp = pltpu.make_async_remote_copy(win, win, sems[2].at[3 * j + k], sems[3].at[3 * j + k],
                                                  device_id=sibling, device_id_type=MESH)
                cp.start()
                passed.append(cp)
        for j in range(n):
            for k, chip in enumerate(chips):
                win = window(outs[j], j, 2 * chip[0] + chip[1], 1 - c)
                pltpu.make_async_remote_copy(win, win, sems[2].at[3 * j + k], sems[3].at[3 * j + k],
                                             device_id=sibling, device_id_type=MESH).wait_recv()
        for cp in first_copies(outs, sems) + passed:
            cp.wait_send()

    return _Job(fulls, [jax.ShapeDtypeStruct(f.shape, f.dtype) for f in fulls], {j: j for j in range(n)},
                [pltpu.SemaphoreType.DMA((3 * n,)) for _ in range(4)], start, finish)


def _job_pair(grads):
    n = len(grads)

    def copies(ins, outs, sems):
        x, y, c = _place()
        return [pltpu.make_async_remote_copy(ins[j].at[1], outs[j], sems[0].at[j], sems[1].at[j],
                                             device_id=(x, y, 1 - c), device_id_type=MESH) for j in range(n)]

    def start(ins, outs, sems):
        for cp in copies(ins, outs, sems):
            cp.start()

    def finish(ins, outs, sems):
        for cp in copies(ins, outs, sems):
            cp.wait()

    return _Job(grads, [jax.ShapeDtypeStruct(g.shape[1:], g.dtype) for g in grads], {},
                [pltpu.SemaphoreType.DMA((n,)), pltpu.SemaphoreType.DMA((n,))], start, finish)


def _job_chips(halves, kinds):
    n = len(halves)
    shapes = [(h.shape[0], h.shape[1] // 4) if kinds[j] == "col" else (h.shape[0] // 4, h.shape[1])
              for j, h in enumerate(halves)]

    def part(ref, j, s):
        r, cc = shapes[j]
        if kinds[j] == "col":
            return ref.at[:, pl.ds(pl.multiple_of(s * cc, LANE), cc)]
        return ref.at[pl.ds(pl.multiple_of(s * r, 16), r), :]

    def copies(ins, outs, sems):
        x, y, c = _place()
        s_me = 2 * x + y
        local = [pltpu.make_async_copy(part(ins[j], j, s_me), outs[j].at[s_me], sems[2].at[j]) for j in range(n)]
        sends, recvs = [], []
        for j in range(n):
            for k, chip in enumerate(_chips(x, y)):
                s_peer = 2 * chip[0] + chip[1]
                sends.append(pltpu.make_async_remote_copy(part(ins[j], j, s_peer), outs[j].at[s_me],
                                                          sems[0].at[3 * j + k], sems[1].at[3 * j + k],
                                                          device_id=(*chip, c), device_id_type=MESH))
                dst = outs[j].at[s_peer]
                recvs.append(pltpu.make_async_remote_copy(dst, dst, sems[0].at[3 * j + k], sems[1].at[3 * j + k],
                                                          device_id=(*chip, c), device_id_type=MESH))
        return local, sends, recvs

    def start(ins, outs, sems):
        local, sends, _ = copies(ins, outs, sems)
        for cp in local + sends:
            cp.start()

    def finish(ins, outs, sems):
        local, sends, recvs = copies(ins, outs, sems)
        for cp in recvs:
            cp.wait_recv()
        for cp in sends:
            cp.wait_send()
        for cp in local:
            cp.wait()

    return _Job(halves, [jax.ShapeDtypeStruct((4,) + shapes[j], halves[j].dtype) for j in range(n)], {},
                [pltpu.SemaphoreType.DMA((3 * n,)), pltpu.SemaphoreType.DMA((3 * n,)), pltpu.SemaphoreType.DMA((n,))],
                start, finish)


def _job_join(pairs):
    n = len(pairs)

    def copies(outs, sems):
        x, y, c = _place()
        sends, recvs = [], []
        for j in range(n):
            sends.append(pltpu.make_async_remote_copy(outs[j].at[c], outs[j].at[c], sems[0].at[j], sems[1].at[j],
                                                      device_id=(x, y, 1 - c), device_id_type=MESH))
            theirs = outs[j].at[1 - c]
            recvs.append(pltpu.make_async_remote_copy(theirs, theirs, sems[0].at[j], sems[1].at[j],
                                                      device_id=(x, y, 1 - c), device_id_type=MESH))
        return sends, recvs

    def start(ins, outs, sems):
        for cp in copies(outs, sems)[0]:
            cp.start()

    def finish(ins, outs, sems):
        sends, recvs = copies(outs, sems)
        for cp in recvs:
            cp.wait_recv()
        for cp in sends:
            cp.wait_send()

    return _Job(pairs, [jax.ShapeDtypeStruct(p.shape, p.dtype) for p in pairs], {j: j for j in range(n)},
                [pltpu.SemaphoreType.DMA((n,)), pltpu.SemaphoreType.DMA((n,))], start, finish)


def _pair_sum(name, mine_other, got):
    _, r, cc = mine_other.shape
    tr = _rows_within(r, 3 * cc * mine_other.dtype.itemsize, (256, 128, 64, 32, 16))

    def body(a_ref, b_ref, o_ref):
        o_ref[...] = (a_ref[...].astype(F32) + b_ref[...].astype(F32)).astype(o_ref.dtype)

    return pl.pallas_call(
        body, name=name, grid=(r // tr,),
        in_specs=[pl.BlockSpec((None, tr, cc), lambda i: (0, i, 0)), pl.BlockSpec((tr, cc), lambda i: (i, 0))],
        out_specs=pl.BlockSpec((tr, cc), lambda i: (i, 0)),
        out_shape=jax.ShapeDtypeStruct((r, cc), mine_other.dtype), compiler_params=_params("parallel"),
    )(mine_other, got)


def _sum_slots(name, parts, into_slot=None):
    n, r, cc = parts.shape
    tr = _rows_within(r, cc * (n * parts.dtype.itemsize + 4), (256, 128, 64, 32, 16, 8))

    def body(*refs):
        p_ref, o_ref = refs[-2:]
        acc = p_ref[0].astype(F32)
        for s in range(1, n):
            acc = acc + p_ref[s].astype(F32)
        o_ref[...] = acc

    in_spec = pl.BlockSpec((n, tr, cc), lambda i, *_: (0, i, 0))
    if into_slot is None:
        return pl.pallas_call(
            body, name=name, grid=(r // tr,), in_specs=[in_spec], out_specs=pl.BlockSpec((tr, cc), lambda i: (i, 0)),
            out_shape=jax.ShapeDtypeStruct((r, cc), F32), compiler_params=_params("parallel"),
        )(parts)
    grid_spec = pltpu.PrefetchScalarGridSpec(
        num_scalar_prefetch=1, grid=(r // tr,), in_specs=[in_spec],
        out_specs=pl.BlockSpec((None, tr, cc), lambda i, c_ref: (c_ref[0], i, 0)))
    return pl.pallas_call(body, name=name, grid_spec=grid_spec, out_shape=jax.ShapeDtypeStruct((2, r, cc), F32),
                          compiler_params=_params("arbitrary"))(into_slot, parts)


def _adamw_math(w, g, m, v):
    m = ADAM_B1 * m + (1.0 - ADAM_B1) * g
    v = ADAM_B2 * v + (1.0 - ADAM_B2) * jnp.square(g)
    m_hat = m / (1.0 - ADAM_B1 ** ADAM_STEP)
    v_hat = v / (1.0 - ADAM_B2 ** ADAM_STEP)
    delta = -ADAM_LR * (m_hat / (jnp.sqrt(v_hat) + ADAM_EPS) + ADAM_WD * w)
    return delta, m, v


def _adamw_layer(name, layer, g_pair, kind, w, m, v, prev):
    n_layers, r, cc = w.shape
    if prev is None:
        prev = tuple(lax.empty(w.shape, F32) for _ in range(4))
    if kind == "col":
        g = g_pair.reshape(r, cc)
        tr = _rows_within(r, 8 * cc * 4)
        grid = (r // tr,)
        g_spec = pl.BlockSpec((tr, cc), lambda i: (i, 0))
        blk = pl.BlockSpec((None, tr, cc), lambda i: (layer, i, 0))
    else:
        g = g_pair
        tr = _rows_within(r, 4 * cc * 4)
        grid = (r // tr, 2)
        g_spec = pl.BlockSpec((None, tr, cc // 2), lambda i, h: (h, i, 0))
        blk = pl.BlockSpec((None, tr, cc // 2), lambda i, h: (layer, i, h))

    def body(g_ref, w_ref, m_ref, v_ref, *rest):
        og_ref, od_ref, om_ref, ov_ref = rest[4:]
        gv = g_ref[...]
        delta, m2, v2 = _adamw_math(w_ref[...], gv, m_ref[...], v_ref[...])
        og_ref[...] = gv
        od_ref[...] = delta
        om_ref[...] = m2
        ov_ref[...] = v2

    return pl.pallas_call(
        body, name=name, grid=grid,
        in_specs=[g_spec, blk, blk, blk] + [HBM_SPEC] * 4,
        out_specs=[blk] * 4, out_shape=[jax.ShapeDtypeStruct(w.shape, F32)] * 4,
        input_output_aliases={4: 0, 5: 1, 6: 2, 7: 3}, compiler_params=_params(*(["parallel"] * len(grid))),
    )(g, w, m, v, *prev)


def _adamw_small(name, g, w, m, v):
    def body(g_ref, w_ref, m_ref, v_ref, od_ref, om_ref, ov_ref):
        delta, m2, v2 = _adamw_math(w_ref[...], g_ref[...], m_ref[...], v_ref[...])
        od_ref[...] = delta
        om_ref[...] = m2
        ov_ref[...] = v2

    return pl.pallas_call(body, name=name, out_shape=[jax.ShapeDtypeStruct(w.shape, F32)] * 3)(g, w, m, v)


def _adamw_ada(name, c16, dmod16, w, m, v):
    n_layers, d, cols = w.shape
    tr = _rows_within(d, 7 * cols * 4, (256, 128))
    blk = pl.BlockSpec((None, tr, cols), lambda l, i: (l, i, 0))

    def body(c_ref, dm_ref, w_ref, m_ref, v_ref, og_ref, od_ref, om_ref, ov_ref):
        gv = lax.dot_general(c_ref[...], dm_ref[...], (((0,), (0,)), ((), ())), preferred_element_type=F32)
        delta, m2, v2 = _adamw_math(w_ref[...], gv, m_ref[...], v_ref[...])
        og_ref[...] = gv
        od_ref[...] = delta
        om_ref[...] = m2
        ov_ref[...] = v2

    return pl.pallas_call(
        body, name=name, grid=(n_layers, d // tr),
        in_specs=[pl.BlockSpec((16, tr), lambda l, i: (0, i)), pl.BlockSpec((None, 16, cols), lambda l, i: (l, 0, 0)),
                  blk, blk, blk],
        out_specs=[blk] * 4, out_shape=[jax.ShapeDtypeStruct(w.shape, F32)] * 4,
        compiler_params=_params("parallel", "parallel"),
    )(c16, dmod16, w, m, v)


def kernel(x, c, w_ada, b_ada, w_in, sink, w_dw, conv_ln_g, conv_ln_b, w_oa, w_ob, w_out, ln1_g, ln1_b, w_gu, w_down, ln2_g, ln2_b, loss_target, m_w_ada, m_b_ada, m_w_in, m_sink, m_w_dw, m_conv_ln_g, m_conv_ln_b, m_w_oa, m_w_ob, m_w_out, m_ln1_g, m_ln1_b, m_w_gu, m_w_down, m_ln2_g, m_ln2_b, v_w_ada, v_b_ada, v_w_in, v_sink, v_w_dw, v_conv_ln_g, v_conv_ln_b, v_w_oa, v_w_ob, v_w_out, v_ln1_g, v_ln1_b, v_w_gu, v_w_down, v_ln2_g, v_ln2_b):
    seq, d = x.shape[1], x.shape[2]
    n_layers = w_in.shape[0]
    d_in = 4 * w_in.shape[2]
    d_ff = 4 * w_down.shape[1]
    hq = d // HEAD_DIM
    dkv = (hq // GQA_GROUP) * HEAD_DIM
    off_k, off_v, off_ga, off_gb = d, d + dkv, d + 2 * dkv, 2 * d + 2 * dkv
    off_gta, off_gtb = 3 * d + 2 * dkv, 4 * d + 2 * dkv
    assert d_in == 5 * d + 2 * dkv and seq % ROW_TILE == 0 and seq >= BAND
    alpha = (2.0 * n_layers) ** 0.25

    xi, yi, ci = _place()
    chip = 2 * xi + yi
    batch = 4 * xi + 2 * yi + ci
    c_idx = jnp.reshape(ci, (1,)).astype(jnp.int32)
    x2 = x[0]
    target = loss_target[0]

    c_act = jax.nn.silu(c)
    c_all = _gather_small("gather_c", c_act, ALL_DEVICES).reshape(8, d)
    c16 = jnp.concatenate([c_all, jnp.zeros((8, d), F32)], axis=0).astype(BF16)
    mod_cols = [_mm(f"mod_{l}", c16, w_ada, "nn", F32, b_layer=l) for l in range(n_layers)]
    mod_all = _gather_small("gather_mod", jnp.stack(mod_cols), SAME_CORE_CHIPS)
    mod = lax.dynamic_index_in_dim(mod_all, batch, axis=2, keepdims=False)
    mod = jnp.transpose(mod, (1, 0, 2)).reshape(n_layers, N_MOD * d) + b_ada
    mod = mod.reshape(n_layers, N_MOD, 1, d)
    sh_a, sc_a, gt_a, sh_f, sc_f, gt_f = (mod[:, j] for j in range(N_MOD))

    pos = jnp.arange(seq, dtype=F32)
    inv_freq = ROPE_THETA ** (-jnp.arange(0, ROPE_DIM, 2, dtype=F32) / ROPE_DIM)
    ang = pos[:, None] * inv_freq[None, :]
    cos, sin = jnp.cos(ang), jnp.sin(ang)
    half = ROPE_DIM // 2
    rest = HEAD_DIM - ROPE_DIM
    t_cs = jnp.concatenate([cos, cos, jnp.ones((seq, rest), F32)], axis=1)
    t_up = jnp.concatenate([-sin, jnp.zeros((seq, rest + half), F32)], axis=1)
    t_dn = jnp.concatenate([jnp.zeros((seq, half), F32), sin, jnp.zeros((seq, rest), F32)], axis=1)

    def rope(t, cs, up, dn):
        w = t.shape[1]
        reps = (1, w // HEAD_DIM)
        return (t * jnp.tile(cs, reps) + pltpu.roll(t, w - half, 1) * jnp.tile(up, reps)
                + pltpu.roll(t, half, 1) * jnp.tile(dn, reps))

    def rope_t(dt, cs, up, dn):
        w = dt.shape[1]
        reps = (1, w // HEAD_DIM)
        return (dt * jnp.tile(cs, reps) + pltpu.roll(dt * jnp.tile(up, reps), half, 1)
                + pltpu.roll(dt * jnp.tile(dn, reps), w - half, 1))

    tables = [(t_cs, 0, HEAD_DIM), (t_up, 0, HEAD_DIM), (t_dn, 0, HEAD_DIM)]

    kinds = ("col", "col", "row", "row", "row", "row")
    big_weights = (w_in, w_gu, w_oa, w_ob, w_out, w_down)
    chip_idx = jnp.reshape(chip, (1,)).astype(jnp.int32)
    shard_shapes = [w.shape[1:] for w in big_weights]
    fulls = [[_cast_into(f"cast_w_{l}_{j}", w, l, kinds[j], chip_idx) for j, w in enumerate(big_weights)]
             for l in range(n_layers)]

    def gather_job(l, which):
        return _job_gather([fulls[l][j] for j in which], [shard_shapes[j] for j in which], [kinds[j] for j in which])

    ride_in_proj, ride_ffn_up, ride_attn_out, ride_conv_out, ride_mix_out, ride_ffn_down = (0,), (1,), (2,), (3,), (4,), (5,)
    ahead, ride_attn_0, ride_conv_0 = (0, 2), (3, 4, 5), (1,)
    gathered = [[None] * 6 for _ in range(n_layers)]

    def keep(layer, which, arrays):
        for j, arr in zip(which, arrays):
            gathered[layer][j] = arr

    keep(0, ahead, _run_jobs("gather_w_0", [gather_job(0, ahead)])[0])
    w_dw_all = _gather_small("gather_dw", w_dw, SAME_CORE_CHIPS)
    w_dw_full = jnp.transpose(w_dw_all, (1, 2, 0, 3)).reshape(n_layers, CONV_WIDTH, d)
    w_dw32 = jnp.pad(w_dw_full, ((0, 0), (0, 32 - CONV_WIDTH), (0, 0)))
    sink_b = jnp.broadcast_to(sink[:, :, None], (n_layers, hq, LANE))

    def vec(a, l):
        return a[l][None, :]

    def res_ln(xprev, y, gt, g, b, scn, shn):
        xn = _ln(alpha * xprev + (1.0 + gt) * y, g, b)
        return xn, xn * (1.0 + scn) + shn

    def merge(ya, yb, ga, gb):
        return _sigmoid(ga) * ya + _sigmoid(gb) * yb

    def swiglu(gate, up):
        return gate * _sigmoid(gate) * up

    h = _rowwise("modulate_in", lambda xv, sc, sh: xv * (1.0 + sc) + sh, seq, [(x2, 0, d)], [sc_a[0], sh_a[0]],
                 [(d, BF16)])
    xprev = x2
    saved = []
    for l in range(n_layers):
        nxt = l + 1 < n_layers

        def mm_carrying(name, a_, j_weight, which):
            if not nxt:
                return _mm(name, a_, gathered[l][j_weight], "nn", F32)
            res, (got,) = _mm(name, a_, gathered[l][j_weight], "nn", F32, jobs=[gather_job(l + 1, which)])
            keep(l + 1, which, got)
            return res

        z = mm_carrying(f"in_proj_{l}", h, 0, ride_in_proj)
        qr, kr, vb = _rowwise(
            f"qkv_prep_{l}", lambda q, k, v, cs, up, dn: (rope(q, cs, up, dn), rope(k, cs, up, dn), v), seq,
            [(z, 0, d), (z, off_k, dkv), (z, off_v, dkv)] + tables, [], [(d, BF16), (dkv, BF16), (dkv, BF16)])
        if l == 0:
            att, (got,) = _attn_fwd(f"attn_{l}", qr, kr, vb, sink_b[l], jobs=[gather_job(0, ride_attn_0)])
            keep(0, ride_attn_0, got)
        else:
            att = _attn_fwd(f"attn_{l}", qr, kr, vb, sink_b[l])
        y_a = mm_carrying(f"attn_out_{l}", att, 2, ride_attn_out)
        u = _rowwise(f"glu_{l}", lambda a, b: a * _sigmoid(b), seq, [(z, off_ga, d), (z, off_gb, d)], [], [(d, F32)])
        if l == 0:
            (u2, cv), (got,) = _conv_fwd(f"conv_{l}", u, w_dw32[l], vec(conv_ln_g, l), vec(conv_ln_b, l),
                                         jobs=[gather_job(0, ride_conv_0)])
            keep(0, ride_conv_0, got)
        else:
            u2, cv = _conv_fwd(f"conv_{l}", u, w_dw32[l], vec(conv_ln_g, l), vec(conv_ln_b, l))
        y_b = mm_carrying(f"conv_out_{l}", cv, 3, ride_conv_out)
        mg = _rowwise(f"merge_{l}", merge, seq, [(y_a, 0, d), (y_b, 0, d), (z, off_gta, d), (z, off_gtb, d)], [],
                      [(d, BF16)])
        o = mm_carrying(f"mix_out_{l}", mg, 4, ride_mix_out)
        x1, h2 = _rowwise(f"res_ln1_{l}", res_ln, seq, [(xprev, 0, d), (o, 0, d)],
                          [gt_a[l], vec(ln1_g, l), vec(ln1_b, l), sc_f[l], sh_f[l]], [(d, F32), (d, BF16)])
        gu = mm_carrying(f"ffn_up_{l}", h2, 1, ride_ffn_up)
        f = _rowwise(f"swiglu_{l}", swiglu, seq, [(gu, 0, d_ff), (gu, d_ff, d_ff)], [], [(d_ff, BF16)])
        ffn = mm_carrying(f"ffn_down_{l}", f, 5, ride_ffn_down)
        saved.append(dict(xprev=xprev, h=h, z=z, qr=qr, kr=kr, vb=vb, att=att, u=u, u2=u2, cv=cv, y_a=y_a, y_b=y_b,
                          mg=mg, o=o, x1=x1, h2=h2, gu=gu, f=f, ffn=ffn))
        if l + 1 < n_layers:
            xprev, h = _rowwise(f"res_ln2_{l}", res_ln, seq, [(x1, 0, d), (ffn, 0, d)],
                                [gt_f[l], vec(ln2_g, l), vec(ln2_b, l), sc_a[l + 1], sh_a[l + 1]], [(d, F32), (d, BF16)])

    def res_ln_bwd(xp, y, dxn, dh, gt, g, b, scn, shn):
        _, vjp = jax.vjp(res_ln, xp, y, gt, g, b, scn, shn)
        return vjp((dxn, dh))

    def last_ln_bwd(xp, y, tgt, gt, g, b):
        def head(xp_, y_, gt_, g_, b_):
            return _ln(alpha * xp_ + (1.0 + gt_) * y_, g_, b_)
        out, vjp = jax.vjp(head, xp, y, gt, g, b)
        err = out - tgt
        loss = 0.5 * jnp.sum(jnp.sum(err * err, axis=-1, keepdims=True) / d, axis=0, keepdims=True)
        return vjp(err / d) + (jnp.broadcast_to(loss, (1, LANE)),)

    def merge_bwd(dmg, ya, yb, ga, gb):
        _, vjp = jax.vjp(merge, ya, yb, ga, gb)
        dya, dyb, dga, dgb = vjp(dmg)
        return dya, dyb, jnp.concatenate([dga, dgb], axis=1)

    def swiglu_bwd(df, gate, up):
        _, vjp = jax.vjp(swiglu, gate, up)
        return jnp.concatenate(vjp(df), axis=1)

    vec_d = ((1, d), F32)
    small = [None] * n_layers
    big = None
    loss_part = None
    dxn = dh = None
    pending = None
    stacks = ((w_in, m_w_in, v_w_in), (w_gu, m_w_gu, v_w_gu), (w_oa, m_w_oa, v_w_oa), (w_ob, m_w_ob, v_w_ob),
              (w_out, m_w_out, v_w_out), (w_down, m_w_down, v_w_down))

    def adamw_all(layer, full, prev):
        return [_adamw_layer(f"adamw_{layer}_{j}", layer, full[j], kinds[j], *stacks[j], None if prev is None else prev[j])
                for j in range(6)]

    for l in reversed(range(n_layers)):
        sv = saved[l]
        wi, wg, woa, wob, wout, wdn = gathered[l]
        ln2 = [gt_f[l], vec(ln2_g, l), vec(ln2_b, l)]
        if l + 1 == n_layers:
            dx1, dffn, d_gtf, d_g2, d_b2, loss_part = _rowwise(
                "last_ln_bwd", last_ln_bwd, seq, [(sv["x1"], 0, d), (sv["ffn"], 0, d), (target, 0, d)], ln2,
                [(d, F32), (d, BF16)], [vec_d, vec_d, vec_d, ((1, LANE), F32)])
            d_sca_next = d_sha_next = None
        else:
            dx1, dffn, d_gtf, d_g2, d_b2, d_sca_next, d_sha_next = _rowwise(
                f"res_ln2_bwd_{l}", res_ln_bwd, seq, [(sv["x1"], 0, d), (sv["ffn"], 0, d), (dxn, 0, d), (dh, 0, d)],
                ln2 + [sc_a[l + 1], sh_a[l + 1]], [(d, F32), (d, BF16)], [vec_d] * 5)
            small[l + 1]["sc_a"], small[l + 1]["sh_a"] = d_sca_next, d_sha_next
        def riding(make_job, *job_args):
            return [] if pending is None else [make_job(*job_args)]

        def unpack(res):
            return res if pending is not None else (res, [None])

        df, (got,) = unpack(_mm(f"ffn_down_dx_{l}", dffn, wdn, "nt", F32, jobs=riding(_job_pair, pending)))
        halves = None if pending is None else [
            _pair_sum(f"pair_sum_{l + 1}_{j}", pending[j], got[j]) for j in range(6)]
        g_down = _mm(f"ffn_down_dw_{l}", sv["f"], dffn, "tn", BF16, split="cols", c_idx=c_idx)
        dgu = _rowwise(f"swiglu_bwd_{l}", swiglu_bwd, seq, [(df, 0, d_ff), (sv["gu"], 0, d_ff), (sv["gu"], d_ff, d_ff)],
                       [], [(2 * d_ff, BF16)])
        dh2, (parts_in,) = unpack(_mm(f"ffn_up_dx_{l}", dgu, wg, "nt", F32,
                                      jobs=riding(lambda: _job_chips(halves[:1], kinds[:1]))))
        g_gu, (parts_gu,) = unpack(_mm(f"ffn_up_dw_{l}", sv["h2"], dgu, "tn", BF16, split="rows", c_idx=c_idx,
                                       jobs=riding(lambda: _job_chips(halves[1:2], kinds[1:2]))))
        dxp, d_o, d_gta, d_g1, d_b1, d_scf, d_shf = _rowwise(
            f"res_ln1_bwd_{l}", res_ln_bwd, seq, [(sv["xprev"], 0, d), (sv["o"], 0, d), (dx1, 0, d), (dh2, 0, d)],
            [gt_a[l], vec(ln1_g, l), vec(ln1_b, l), sc_f[l], sh_f[l]], [(d, F32), (d, BF16)], [vec_d] * 5)
        dmg = _mm(f"mix_out_dx_{l}", d_o, wout, "nt", F32)
        g_out = _mm(f"mix_out_dw_{l}", sv["mg"], d_o, "tn", BF16, split="cols", c_idx=c_idx)
        z = sv["z"]
        dya, dyb, d_gates = _rowwise(
            f"merge_bwd_{l}", merge_bwd, seq,
            [(dmg, 0, d), (sv["y_a"], 0, d), (sv["y_b"], 0, d), (z, off_gta, d), (z, off_gtb, d)], [],
            [(d, BF16), (d, BF16), (2 * d, BF16)])
        d_att = _mm(f"attn_out_dx_{l}", dya, woa, "nt", BF16)
        g_oa = _mm(f"attn_out_dw_{l}", sv["att"], dya, "tn", BF16, split="cols", c_idx=c_idx)
        d_cv = _mm(f"conv_out_dx_{l}", dyb, wob, "nt", F32)
        g_ob = _mm(f"conv_out_dw_{l}", sv["cv"], dyb, "tn", BF16, split="cols", c_idx=c_idx)
        if l > 0:
            du2, d_wdw, d_cg, d_cb = _conv_bwd_a(f"conv_bwd_a_{l}", sv["u"], sv["u2"], d_cv, vec(conv_ln_g, l),
                                                 vec(conv_ln_b, l))
            d_glu = _conv_bwd_b(f"conv_bwd_b_{l}", du2, z, off_ga, off_gb, w_dw32[l])
            dqr, dkr, dvb, d_sink = _attn_bwd(f"attn_bwd_{l}", sv["qr"], sv["kr"], sv["vb"], sink_b[l], d_att)
        else:
            early = {1: g_gu, 2: g_oa, 3: g_ob, 4: g_out, 5: g_down}
            (du2, d_wdw, d_cg, d_cb), (got_0,) = _conv_bwd_a(
                f"conv_bwd_a_{l}", sv["u"], sv["u2"], d_cv, vec(conv_ln_g, l), vec(conv_ln_b, l),
                jobs=[_job_pair(list(early.values()))])
            halves_0 = {j: _pair_sum(f"pair_sum_0_{j}", early[j], got_0[n]) for n, j in enumerate(early)}
            parts_0 = {}
            d_glu, ((parts_0[5],),) = _conv_bwd_b(f"conv_bwd_b_{l}", du2, z, off_ga, off_gb, w_dw32[l],
                                                  jobs=[_job_chips([halves_0[5]], [kinds[5]])])
            (dqr, dkr, dvb, d_sink), ((parts_0[1],),) = _attn_bwd(
                f"attn_bwd_{l}", sv["qr"], sv["kr"], sv["vb"], sink_b[l], d_att,
                jobs=[_job_chips([halves_0[1]], [kinds[1]])])
        d_qkv = _rowwise(
            f"qkv_bwd_{l}",
            lambda dq_, dk_, dv_, cs, up, dn: jnp.concatenate([rope_t(dq_, cs, up, dn), rope_t(dk_, cs, up, dn), dv_], axis=1),
            seq, [(dqr, 0, d), (dkr, 0, dkv), (dvb, 0, dkv)] + tables, [], [(d + 2 * dkv, BF16)])
        dz = jnp.concatenate([d_qkv, d_glu, d_gates], axis=1)
        dh, (parts_rest,) = unpack(_mm(f"in_proj_dx_{l}", dz, wi, "nt", F32,
                                       jobs=riding(lambda: _job_chips(halves[2:], kinds[2:]))))
        reduced = None if pending is None else [
            _sum_slots(f"sum_chips_{l + 1}_{j}", p, into_slot=c_idx) for j, p in enumerate(parts_in + parts_gu + parts_rest)]
        last_jobs = riding(lambda: _job_join(reduced))
        if l == 0:
            last_jobs = last_jobs + [_job_chips([halves_0[j] for j in (2, 3, 4)], kinds[2:5])]
        g_in = _mm(f"in_proj_dw_{l}", sv["h"], dz, "tn", BF16, split="rows", c_idx=c_idx, jobs=last_jobs)
        if last_jobs:
            g_in, job_res = g_in
            if l == 0:
                parts_0[2], parts_0[3], parts_0[4] = job_res[-1]
            if pending is not None:
                big = adamw_all(l + 1, job_res[0], big)
        dxn = dxp
        small[l] = dict(gt_a=d_gta, sh_f=d_shf, sc_f=d_scf, gt_f=d_gtf, ln1_g=d_g1, ln1_b=d_b1, ln2_g=d_g2, ln2_b=d_b2,
                        conv_ln_g=d_cg, conv_ln_b=d_cb, sink=d_sink[:, :1].reshape(1, hq), w_dw=d_wdw[:CONV_WIDTH])
        pending = [g_in, g_gu, g_oa, g_ob, g_out, g_down]

    ((got_in,),) = _run_jobs("rs_pair_0", [_job_pair(pending[:1])])
    half_in = _pair_sum("pair_sum_0_0", pending[0], got_in)
    ((parts_0[0],),) = _run_jobs("rs_chips_0", [_job_chips([half_in], kinds[:1])])
    reduced = [_sum_slots(f"sum_chips_0_{j}", parts_0[j], into_slot=c_idx) for j in range(6)]
    (full,) = _run_jobs("rs_join_0", [_job_join(reduced)])
    big = adamw_all(0, full, big)

    grad_x, d_sca0, d_sha0 = _rowwise(
        "modulate_in_bwd", lambda xv, dhv, dxv, sc: (dxv + dhv * (1.0 + sc), jnp.sum(dhv * xv, axis=0, keepdims=True),
                                                     jnp.sum(dhv, axis=0, keepdims=True)),
        seq, [(x2, 0, d), (dh, 0, d), (dxn, 0, d)], [sc_a[0]], [(d, F32)], [vec_d, vec_d])
    small[0]["sc_a"], small[0]["sh_a"] = d_sca0, d_sha0

    order = ("sh_a", "sc_a", "gt_a", "sh_f", "sc_f", "gt_f", "conv_ln_g", "conv_ln_b", "ln1_g", "ln1_b", "ln2_g", "ln2_b")
    rows = []
    for l in range(n_layers):
        rows += [small[l][k] for k in order]
        rows.append(jnp.pad(small[l]["sink"], ((0, 0), (0, d - hq))))
        rows.append(small[l]["w_dw"])
    rows.append(jnp.pad(loss_part, ((0, 0), (0, d - LANE))))
    n_small = sum(r.shape[0] for r in rows)
    pad_rows = (-n_small) % 8
    packed = jnp.concatenate(rows + [jnp.zeros((pad_rows, d), F32)], axis=0)
    everyone = _gather_small("gather_small_grads", packed, ALL_DEVICES)
    total = _sum_slots("sum_small_grads", everyone)
    per_layer = len(order) + 1 + CONV_WIDTH
    tot = total[:n_layers * per_layer].reshape(n_layers, per_layer, d)
    g_mod = tot[:, :N_MOD].reshape(n_layers, N_MOD * d)
    g_small = {k: tot[:, N_MOD + j] for j, k in enumerate(order[N_MOD:])}
    g_sink = tot[:, len(order), :hq]
    g_dw_full = tot[:, len(order) + 1:]
    cols_dw = w_dw.shape[2]
    g_dw = lax.dynamic_slice_in_dim(g_dw_full, chip * cols_dw, cols_dw, axis=2)
    loss = total[n_layers * per_layer, 0]

    d_mod_all = everyone[:, :n_layers * per_layer].reshape(8, n_layers, per_layer, d)[:, :, :N_MOD]
    d_mod_all = d_mod_all.reshape(8, n_layers, N_MOD * d)
    cols_ada = w_ada.shape[2]
    d_mod_mine = lax.dynamic_slice_in_dim(d_mod_all, chip * cols_ada, cols_ada, axis=2)
    dmod16 = jnp.concatenate([d_mod_mine, jnp.zeros_like(d_mod_mine)], axis=0)
    dmod16 = jnp.transpose(dmod16, (1, 0, 2)).astype(BF16)
    ada = _adamw_ada("adamw_ada", c16, dmod16, w_ada, m_w_ada, v_w_ada)

    def small_step(name, g, w, m, v):
        shp = w.shape
        g2, w2, m2, v2 = (a.reshape(-1, shp[-1]) for a in (g, w, m, v))
        return (g,) + tuple(a.reshape(shp) for a in _adamw_small(name, g2, w2, m2, v2))

    res = {
        "w_ada": ada,
        "b_ada": small_step("adamw_b_ada", g_mod, b_ada, m_b_ada, v_b_ada),
        "sink": small_step("adamw_sink", g_sink, sink, m_sink, v_sink),
        "w_dw": small_step("adamw_w_dw", g_dw, w_dw, m_w_dw, v_w_dw),
        "conv_ln_g": small_step("adamw_conv_ln_g", g_small["conv_ln_g"], conv_ln_g, m_conv_ln_g, v_conv_ln_g),
        "conv_ln_b": small_step("adamw_conv_ln_b", g_small["conv_ln_b"], conv_ln_b, m_conv_ln_b, v_conv_ln_b),
        "ln1_g": small_step("adamw_ln1_g", g_small["ln1_g"], ln1_g, m_ln1_g, v_ln1_g),
        "ln1_b": small_step("adamw_ln1_b", g_small["ln1_b"], ln1_b, m_ln1_b, v_ln1_b),
        "ln2_g": small_step("adamw_ln2_g", g_small["ln2_g"], ln2_g, m_ln2_g, v_ln2_g),
        "ln2_b": small_step("adamw_ln2_b", g_small["ln2_b"], ln2_b, m_ln2_b, v_ln2_b),
        "w_in": big[0], "w_gu": big[1], "w_oa": big[2], "w_ob": big[3], "w_out": big[4], "w_down": big[5],
    }
    names = ("w_ada", "b_ada", "w_in", "sink", "w_dw", "conv_ln_g", "conv_ln_b", "w_oa", "w_ob", "w_out", "ln1_g", "ln1_b",
             "w_gu", "w_down", "ln2_g", "ln2_b")
    outs = [loss, grad_x[None]]
    for field in range(4):
        outs += [res[k][field] for k in names]
    return tuple(outs)
```

```python
import functools
import math

import jax
import jax.numpy as jnp
from jax import lax
from jax.experimental import pallas as pl
from jax.experimental.pallas import tpu as pltpu

F32 = jnp.float32
BF16 = jnp.bfloat16
MESH = pl.DeviceIdType.MESH

HEAD_DIM = 128
GQA_GROUP = 4
WINDOW = 128
BLOCK = 128
BAND = 3 * BLOCK
ROPE_DIM = HEAD_DIM // 4
ROPE_THETA = 500000.0
CONV_WIDTH = 31
CONV_PAD = CONV_WIDTH // 2
CONV_HALO = 16
N_MOD = 6
LN_EPS = 1e-5
NEG_INF = -1e30
ADAM_LR = 0.001
ADAM_B1 = 0.9
ADAM_B2 = 0.999
ADAM_EPS = 1e-08
ADAM_WD = 0.01
ADAM_STEP = 10

LANE = 128
SUBLANES = 8
V7X_VMEM_LIMIT = 56 * 1024 * 1024
ROW_TILE = 256
CONV_ROWS = 32
CONV_LANES = 256

HBM_SPEC = pl.BlockSpec(memory_space=pltpu.HBM)


def _params(*sem):
    return pltpu.CompilerParams(dimension_semantics=sem, vmem_limit_bytes=V7X_VMEM_LIMIT)


def _pick(n, cands, even=False):
    for t in cands:
        if n % t == 0 and (not even or (n // t) % 2 == 0):
            return t
    raise ValueError(f"no tile for {n} in {cands}")


BLOCK_BUDGET = 10 * 1024 * 1024
MM_BLOCK_BUDGET = 40 * 1024 * 1024


def _rows_within(n_rows, bytes_per_row, cands=(256, 128, 64, 32, 16, 8)):
    fit = [t for t in cands if n_rows % t == 0]
    for t in fit:
        if t * bytes_per_row <= BLOCK_BUDGET:
            return t
    return fit[-1]


def _sigmoid(v):
    return jax.nn.sigmoid(v)


def _const_map(ndim):
    return lambda *_: (0,) * ndim


def _rowwise(name, fn, n_rows, row_ins, vec_ins, row_outs, vec_outs=()):
    per_row = sum(w * a.dtype.itemsize for a, _, w in row_ins) + sum(w * jnp.dtype(dt).itemsize for w, dt in row_outs)
    tr = _rows_within(n_rows, per_row, (ROW_TILE, 128, 64))
    in_specs, args, pieces = [], [], []
    for arr, off, width in row_ins:
        bw = math.gcd(off, width) if off else width
        assert bw % LANE == 0 and arr.shape[0] == n_rows
        pieces.append(width // bw)
        for p in range(width // bw):
            in_specs.append(pl.BlockSpec((tr, bw), functools.partial(lambda i, blk: (i, blk), blk=off // bw + p)))
            args.append(arr)
    for v in vec_ins:
        in_specs.append(pl.BlockSpec(v.shape, _const_map(v.ndim)))
        args.append(v)
    out_shape = [jax.ShapeDtypeStruct((n_rows, w), dt) for w, dt in row_outs]
    out_specs = [pl.BlockSpec((tr, w), lambda i: (i, 0)) for w, _ in row_outs]
    for shp, dt in vec_outs:
        out_shape.append(jax.ShapeDtypeStruct(shp, dt))
        out_specs.append(pl.BlockSpec(shp, _const_map(len(shp))))
    n_in, n_row_out = len(args), len(row_outs)

    def body(*refs):
        in_refs, out_refs = refs[:n_in], refs[n_in:]
        vals, k = [], 0
        for npc in pieces:
            ps = [in_refs[k + p][...] for p in range(npc)]
            k += npc
            vals.append((ps[0] if npc == 1 else jnp.concatenate(ps, axis=1)).astype(F32))
        for _ in vec_ins:
            vals.append(in_refs[k][...])
            k += 1
        outs = fn(*vals)
        if not isinstance(outs, (tuple, list)):
            outs = (outs,)
        assert len(outs) == len(out_refs)
        for j in range(n_row_out):
            out_refs[j][...] = outs[j].astype(out_refs[j].dtype)
        if vec_outs:
            @pl.when(pl.program_id(0) == 0)
            def _():
                for j in range(n_row_out, len(out_refs)):
                    out_refs[j][...] = jnp.zeros(out_refs[j].shape, out_refs[j].dtype)
            for j in range(n_row_out, len(out_refs)):
                out_refs[j][...] += outs[j].astype(out_refs[j].dtype)

    res = pl.pallas_call(
        body, name=name, grid=(n_rows // tr,), in_specs=in_specs, out_specs=out_specs, out_shape=out_shape,
        compiler_params=_params("arbitrary"),
    )(*args)
    return res[0] if len(res) == 1 else res


def _mm(name, a, b, mode, out_dtype, b_layer=None, split=None, c_idx=None, jobs=()):
    bshape = b.shape[1:] if b_layer is not None else b.shape
    if mode == "nn":
        (m, k), (k2, n) = a.shape, bshape
        dims = (((1,), (0,)), ((), ()))
    elif mode == "nt":
        (m, k), (n, k2) = a.shape, bshape
        dims = (((1,), (1,)), ((), ()))
    else:
        (k, m), (k2, n) = a.shape, bshape
        dims = (((0,), (0,)), ((), ()))
    assert k == k2, (name, a.shape, b.shape)
    tm = _pick(m, (1024, 512, 256, 128, 16), even=(split == "rows"))
    tn = _pick(n, (1024, 512, 256, 128), even=(split == "cols"))
    out_bytes = jnp.dtype(out_dtype).itemsize
    b_bytes = b.dtype.itemsize

    def blocks_fit(t):
        acc = 0 if t == k else tm * tn * 4
        return 2 * (tm * t * a.dtype.itemsize + tn * t * b_bytes + tm * tn * out_bytes) + acc <= MM_BLOCK_BUDGET

    tk = next(t for t in (4096, 2816, 2048, 1408, 1024, 704, 512, 256, 128) if k % t == 0 and (blocks_fit(t) or t == 128))
    ni, nj, nk = m // tm, n // tn, k // tk

    if mode == "nn":
        a_spec = pl.BlockSpec((tm, tk), lambda i, j, kk, *_: (i, kk))
        b_blk, b_map = (tk, tn), (lambda i, j, kk: (kk, j))
    elif mode == "nt":
        a_spec = pl.BlockSpec((tm, tk), lambda i, j, kk, *_: (i, kk))
        b_blk, b_map = (tn, tk), (lambda i, j, kk: (j, kk))
    else:
        a_spec = pl.BlockSpec((tk, tm), lambda i, j, kk, *_: (kk, i))
        b_blk, b_map = (tk, tn), (lambda i, j, kk: (kk, j))
    if b_layer is None:
        b_spec = pl.BlockSpec(b_blk, lambda i, j, kk, *_: b_map(i, j, kk))
    else:
        b_spec = pl.BlockSpec((None,) + b_blk, lambda i, j, kk, *_: (b_layer,) + b_map(i, j, kk))

    if split is None:
        out_shape = jax.ShapeDtypeStruct((m, n), out_dtype)
        o_spec = pl.BlockSpec((tm, tn), lambda i, j, kk, *_: (i, j))
    elif split == "rows":
        out_shape = jax.ShapeDtypeStruct((2, m // 2, n), out_dtype)
        o_spec = pl.BlockSpec(
            (None, tm, tn), lambda i, j, kk, c_ref: (jnp.where(i // (ni // 2) == c_ref[0], 0, 1), i % (ni // 2), j))
    else:
        out_shape = jax.ShapeDtypeStruct((2, m, n // 2), out_dtype)
        o_spec = pl.BlockSpec(
            (None, tm, tn), lambda i, j, kk, c_ref: (jnp.where(j // (nj // 2) == c_ref[0], 0, 1), i, j % (nj // 2)))

    n_job_in = sum(len(jb.ins) for jb in jobs)
    n_job_out = sum(len(jb.out_shapes) for jb in jobs)
    n_acc = 0 if nk == 1 else 1

    def body(*refs):
        if split is not None:
            refs = refs[1:]
        a_ref, b_ref = refs[:2]
        job_ins = refs[2:2 + n_job_in]
        o_ref = refs[2 + n_job_in]
        job_outs = refs[3 + n_job_in:3 + n_job_in + n_job_out]
        scratch_refs = refs[3 + n_job_in + n_job_out:]
        cut = _job_refs(jobs, job_ins, job_outs, scratch_refs[n_acc:])
        i, j, kk = pl.program_id(0), pl.program_id(1), pl.program_id(2)

        if jobs:
            @pl.when((i == 0) & (j == 0) & (kk == 0))
            def _():
                for jb, parts in zip(jobs, cut):
                    jb.start(*parts)

        part = lax.dot_general(a_ref[...].astype(BF16), b_ref[...].astype(BF16), dims, preferred_element_type=F32)
        if nk == 1:
            o_ref[...] = part.astype(o_ref.dtype)
        else:
            acc_ref = scratch_refs[0]

            @pl.when(kk == 0)
            def _():
                acc_ref[...] = part

            @pl.when(kk > 0)
            def _():
                acc_ref[...] += part

            @pl.when(kk == nk - 1)
            def _():
                o_ref[...] = acc_ref[...].astype(o_ref.dtype)

        if jobs:
            @pl.when((i == ni - 1) & (j == nj - 1) & (kk == nk - 1))
            def _():
                for jb, parts in zip(jobs, cut):
                    jb.finish(*parts)

    scratch = ([] if nk == 1 else [pltpu.VMEM((tm, tn), F32)]) + [s for jb in jobs for s in jb.sems]
    params = _params(*(["arbitrary"] * 3 if jobs else ["parallel", "parallel", "arbitrary"]))
    in_specs = [a_spec, b_spec] + [HBM_SPEC] * n_job_in
    out_specs = [o_spec] + [HBM_SPEC] * n_job_out
    out_shapes = [out_shape] + [s for jb in jobs for s in jb.out_shapes]
    operands = [a, b] + [x for jb in jobs for x in jb.ins]
    n_pre = 0 if split is None else 1
    aliases = _job_aliases(jobs, n_pre + 2, 1)
    if split is None:
        res = pl.pallas_call(
            body, name=name, grid=(ni, nj, nk), in_specs=in_specs, out_specs=out_specs, out_shape=out_shapes,
            scratch_shapes=scratch, input_output_aliases=aliases, compiler_params=params,
        )(*operands)
    else:
        grid_spec = pltpu.PrefetchScalarGridSpec(
            num_scalar_prefetch=1, grid=(ni, nj, nk), in_specs=in_specs, out_specs=out_specs, scratch_shapes=scratch)
        res = pl.pallas_call(body, name=name, grid_spec=grid_spec, out_shape=out_shapes, input_output_aliases=aliases,
                             compiler_params=params)(c_idx, *operands)
    if not jobs:
        return res[0]
    return res[0], _job_results(jobs, res[1:])


def _attn_tile(seq):
    return _pick(seq, (256, 128))


def _heads_stacked(ref, b):
    return jnp.concatenate(
        [ref[b * BLOCK:(b + 1) * BLOCK, g * HEAD_DIM:(g + 1) * HEAD_DIM] for g in range(GQA_GROUP)], axis=0)


def _attn_scores(q_ref, k_ref, v_ref, sink_ref, kvh, i, b, tq, seq):
    rows = GQA_GROUP * BLOCK
    q0 = i * tq + b * BLOCK
    k_off = pl.multiple_of(jnp.clip(q0 - BLOCK, 0, seq - BAND), BLOCK)
    kw = k_ref[pl.ds(k_off, BAND), :]
    vw = v_ref[pl.ds(k_off, BAND), :]
    q_pos = q0 + (lax.broadcasted_iota(jnp.int32, (rows, BAND), 0) & (BLOCK - 1))
    k_pos = k_off + lax.broadcasted_iota(jnp.int32, (rows, BAND), 1)
    valid = jnp.abs(k_pos - q_pos) <= WINDOW
    qs = _heads_stacked(q_ref, b)
    s = lax.dot_general(qs, kw, (((1,), (1,)), ((), ())), preferred_element_type=F32) * (HEAD_DIM ** -0.5)
    s = jnp.where(valid, s, NEG_INF)
    sink = jnp.concatenate(
        [jnp.broadcast_to(sink_ref[pl.ds(kvh * GQA_GROUP + g, 1), :][:, :1], (BLOCK, 1)) for g in range(GQA_GROUP)], axis=0)
    m = jnp.maximum(jnp.max(s, axis=-1, keepdims=True), sink)
    p = jnp.exp(s - m)
    p_sink = jnp.exp(sink - m)
    denom = jnp.sum(p, axis=-1, keepdims=True) + p_sink
    return k_off, kw, vw, qs, p / denom, p_sink / denom


def _attn_fwd(name, qr, kr, vb, sink_b, jobs=()):
    seq, dq = qr.shape
    nkv = kr.shape[1] // HEAD_DIM
    tq = _attn_tile(seq)
    gw = GQA_GROUP * HEAD_DIM

    def body(q_ref, k_ref, v_ref, sink_ref, o_ref):
        kvh, i = pl.program_id(0), pl.program_id(1)
        for b in range(tq // BLOCK):
            _, _, vw, _, pn, _ = _attn_scores(q_ref, k_ref, v_ref, sink_ref, kvh, i, b, tq, seq)
            o = jnp.dot(pn.astype(BF16), vw, preferred_element_type=F32).astype(o_ref.dtype)
            for g in range(GQA_GROUP):
                o_ref[b * BLOCK:(b + 1) * BLOCK, g * HEAD_DIM:(g + 1) * HEAD_DIM] = o[g * BLOCK:(g + 1) * BLOCK]

    (att,), job_res = _call(
        name, body, (nkv, seq // tq),
        [
            pl.BlockSpec((tq, gw), lambda h, i: (i, h)),
            pl.BlockSpec((seq, HEAD_DIM), lambda h, i: (0, h)),
            pl.BlockSpec((seq, HEAD_DIM), lambda h, i: (0, h)),
            pl.BlockSpec(sink_b.shape, lambda h, i: (0, 0)),
        ],
        [pl.BlockSpec((tq, gw), lambda h, i: (i, h))], [jax.ShapeDtypeStruct((seq, dq), BF16)], [],
        (qr, kr, vb, sink_b), ("arbitrary", "arbitrary"), jobs)
    return (att, job_res) if jobs else att


def _attn_bwd(name, qr, kr, vb, sink_b, d_att, jobs=()):
    seq, dq = qr.shape
    dkv = kr.shape[1]
    nkv = dkv // HEAD_DIM
    tq = _attn_tile(seq)
    gw = GQA_GROUP * HEAD_DIM
    tn_dims = (((0,), (0,)), ((), ()))

    def body(q_ref, k_ref, v_ref, sink_ref, do_ref, dq_ref, dk_ref, dv_ref, dsink_ref):
        kvh, i = pl.program_id(0), pl.program_id(1)

        @pl.when(i == 0)
        def _():
            dk_ref[...] = jnp.zeros(dk_ref.shape, F32)
            dv_ref[...] = jnp.zeros(dv_ref.shape, F32)

        @pl.when((i == 0) & (kvh == 0))
        def _():
            dsink_ref[...] = jnp.zeros(dsink_ref.shape, F32)

        for b in range(tq // BLOCK):
            k_off, kw, vw, qs, pn, pn_sink = _attn_scores(q_ref, k_ref, v_ref, sink_ref, kvh, i, b, tq, seq)
            dos = _heads_stacked(do_ref, b)
            dp = lax.dot_general(dos, vw, (((1,), (1,)), ((), ())), preferred_element_type=F32)
            delta = jnp.sum(pn * dp, axis=-1, keepdims=True)
            ds = (pn * (dp - delta) * (HEAD_DIM ** -0.5)).astype(BF16)
            dqs = jnp.dot(ds, kw, preferred_element_type=F32)
            sink_term = pn_sink * delta
            for g in range(GQA_GROUP):
                dq_ref[b * BLOCK:(b + 1) * BLOCK, g * HEAD_DIM:(g + 1) * HEAD_DIM] = dqs[g * BLOCK:(g + 1) * BLOCK]
                d_sink = -jnp.sum(sink_term[g * BLOCK:(g + 1) * BLOCK], axis=0, keepdims=True)
                dsink_ref[pl.ds(kvh * GQA_GROUP + g, 1), :] += jnp.broadcast_to(d_sink, (1, LANE))
            dk_ref[pl.ds(k_off, BAND), :] += lax.dot_general(ds, qs, tn_dims, preferred_element_type=F32)
            dv_ref[pl.ds(k_off, BAND), :] += lax.dot_general(pn.astype(BF16), dos, tn_dims, preferred_element_type=F32)

    res, job_res = _call(
        name, body, (nkv, seq // tq),
        [
            pl.BlockSpec((tq, gw), lambda h, i: (i, h)),
            pl.BlockSpec((seq, HEAD_DIM), lambda h, i: (0, h)),
            pl.BlockSpec((seq, HEAD_DIM), lambda h, i: (0, h)),
            pl.BlockSpec(sink_b.shape, lambda h, i: (0, 0)),
            pl.BlockSpec((tq, gw), lambda h, i: (i, h)),
        ],
        [
            pl.BlockSpec((tq, gw), lambda h, i: (i, h)),
            pl.BlockSpec((seq, HEAD_DIM), lambda h, i: (0, h)),
            pl.BlockSpec((seq, HEAD_DIM), lambda h, i: (0, h)),
            pl.BlockSpec(sink_b.shape, lambda h, i: (0, 0)),
        ],
        [
            jax.ShapeDtypeStruct((seq, dq), F32),
            jax.ShapeDtypeStruct((seq, dkv), F32),
            jax.ShapeDtypeStruct((seq, dkv), F32),
            jax.ShapeDtypeStruct(sink_b.shape, F32),
        ],
        [], (qr, kr, vb, sink_b, d_att), ("arbitrary", "arbitrary"), jobs)
    return (res, job_res) if jobs else res


def _halo_specs(tr, width, n_rows):
    per, last = tr // CONV_HALO, n_rows // CONV_HALO - 1
    return [
        pl.BlockSpec((tr, width), lambda i: (i, 0)),
        pl.BlockSpec((CONV_HALO, width), lambda i: (jnp.maximum(i * per - 1, 0), 0)),
        pl.BlockSpec((CONV_HALO, width), lambda i: (jnp.minimum((i + 1) * per, last), 0)),
    ]


def _ext_scratch(tr, width):
    return pltpu.VMEM((SUBLANES, tr + 2 * CONV_HALO, width), F32)


def _fill_ext(ext_ref, main_ref, prev_ref, next_ref, n_steps, tr):
    i = pl.program_id(0)
    ext_ref[0, 0:CONV_HALO, :] = jnp.where(i > 0, prev_ref[...], 0.0)
    ext_ref[0, CONV_HALO:CONV_HALO + tr, :] = main_ref[...]
    ext_ref[0, CONV_HALO + tr:, :] = jnp.where(i < n_steps - 1, next_ref[...], 0.0)
    rows = tr + 2 * CONV_HALO - SUBLANES
    for p in range(1, SUBLANES):
        ext_ref[p, 0:rows, :] = ext_ref[0, p:p + rows, :]


def _tap(ext_ref, r0, t, cols):
    whole, phase = divmod(1 + t, SUBLANES)
    return ext_ref[phase, r0 + whole * SUBLANES:r0 + whole * SUBLANES + CONV_ROWS, cols]


def _conv_taps(ext_ref, w_ref, out_ref, tr, width, flip):
    cw = min(CONV_LANES, width)
    for cc in range(width // cw):
        cols = slice(cc * cw, (cc + 1) * cw)
        for rc in range(tr // CONV_ROWS):
            acc = jnp.zeros((CONV_ROWS, cw), F32)
            for t in range(CONV_WIDTH):
                wt = CONV_WIDTH - 1 - t if flip else t
                acc += _tap(ext_ref, rc * CONV_ROWS, t, cols) * w_ref[wt:wt + 1, cols]
            out_ref[rc * CONV_ROWS:(rc + 1) * CONV_ROWS, cols] = acc


def _ln(v, g, b):
    mu = jnp.mean(v, axis=-1, keepdims=True)
    vc = v - mu
    var = jnp.mean(vc * vc, axis=-1, keepdims=True)
    return vc * lax.rsqrt(var + LN_EPS) * g + b


def _conv_fwd(name, u, w32, ln_g, ln_b, jobs=()):
    n_rows, width = u.shape
    tr = min(ROW_TILE, n_rows)
    n_steps = n_rows // tr

    def body(main_ref, prev_ref, next_ref, w_ref, g_ref, b_ref, u2_ref, cv_ref, ext_ref):
        _fill_ext(ext_ref, main_ref, prev_ref, next_ref, n_steps, tr)
        _conv_taps(ext_ref, w_ref, u2_ref, tr, width, flip=False)
        u3 = _ln(u2_ref[...], g_ref[...], b_ref[...])
        cv_ref[...] = (u3 * _sigmoid(u3)).astype(cv_ref.dtype)

    vec = lambda a: pl.BlockSpec(a.shape, lambda i: (0, 0))
    res, job_res = _call(
        name, body, (n_steps,), _halo_specs(tr, width, n_rows) + [vec(w32), vec(ln_g), vec(ln_b)],
        [pl.BlockSpec((tr, width), lambda i: (i, 0))] * 2,
        [jax.ShapeDtypeStruct((n_rows, width), F32), jax.ShapeDtypeStruct((n_rows, width), BF16)],
        [_ext_scratch(tr, width)], (u, u, u, w32, ln_g, ln_b), ("arbitrary",), jobs)
    return (res, job_res) if jobs else res


def _conv_bwd_a(name, u, u2, d_cv, ln_g, ln_b, jobs=()):
    n_rows, width = u.shape
    tr = min(ROW_TILE // 2, n_rows)
    n_steps = n_rows // tr

    def body(main_ref, prev_ref, next_ref, u2_ref, dcv_ref, g_ref, b_ref, du2_ref, dw_ref, dg_ref, db_ref, ext_ref):
        @pl.when(pl.program_id(0) == 0)
        def _():
            dw_ref[...] = jnp.zeros(dw_ref.shape, F32)
            dg_ref[...] = jnp.zeros(dg_ref.shape, F32)
            db_ref[...] = jnp.zeros(db_ref.shape, F32)

        _fill_ext(ext_ref, main_ref, prev_ref, next_ref, n_steps, tr)

        def swish_ln(v, g, b):
            u3 = _ln(v, g, b)
            return u3 * _sigmoid(u3)

        _, vjp = jax.vjp(swish_ln, u2_ref[...], g_ref[...], b_ref[...])
        du2, dg, db = vjp(dcv_ref[...])
        du2_ref[...] = du2
        dg_ref[...] += dg
        db_ref[...] += db
        for cc in range(width // LANE):
            cols = slice(cc * LANE, (cc + 1) * LANE)
            for t0 in range(0, CONV_WIDTH, 16):
                taps = range(t0, min(t0 + 16, CONV_WIDTH))
                accs = {t: jnp.zeros((SUBLANES, LANE), F32) for t in taps}
                for rc in range(tr // CONV_ROWS):
                    r0 = rc * CONV_ROWS
                    d_blk = du2_ref[r0:r0 + CONV_ROWS, cols]
                    for t in taps:
                        prod = d_blk * _tap(ext_ref, r0, t, cols)
                        for q in range(CONV_ROWS // SUBLANES):
                            accs[t] = accs[t] + prod[q * SUBLANES:(q + 1) * SUBLANES]
                for t in taps:
                    dw_ref[t:t + 1, cols] += jnp.sum(accs[t], axis=0, keepdims=True)

    vec = lambda a: pl.BlockSpec(a.shape, lambda i: (0, 0))
    row = pl.BlockSpec((tr, width), lambda i: (i, 0))
    res, job_res = _call(
        name, body, (n_steps,), _halo_specs(tr, width, n_rows) + [row, row, vec(ln_g), vec(ln_b)],
        [row, pl.BlockSpec((32, width), lambda i: (0, 0)), vec(ln_g), vec(ln_b)],
        [jax.ShapeDtypeStruct((n_rows, width), F32), jax.ShapeDtypeStruct((32, width), F32),
         jax.ShapeDtypeStruct(ln_g.shape, F32), jax.ShapeDtypeStruct(ln_b.shape, F32)],
        [_ext_scratch(tr, width)], (u, u, u, u2, d_cv, ln_g, ln_b), ("arbitrary",), jobs)
    return (res, job_res) if jobs else res


def _conv_bwd_b(name, du2, z, off_a, off_b, w32, jobs=()):
    n_rows, width = du2.shape
    tr = min(ROW_TILE, n_rows)
    n_steps = n_rows // tr
    bw = math.gcd(math.gcd(off_a, off_b), width)
    npc = width // bw

    def body(*refs):
        main_ref, prev_ref, next_ref = refs[:3]
        a_refs, b_refs = refs[3:3 + npc], refs[3 + npc:3 + 2 * npc]
        w_ref, out_ref, ext_ref, du_ref = refs[3 + 2 * npc:]
        _fill_ext(ext_ref, main_ref, prev_ref, next_ref, n_steps, tr)
        _conv_taps(ext_ref, w_ref, du_ref, tr, width, flip=True)
        for p in range(npc):
            cols = slice(p * bw, (p + 1) * bw)
            du = du_ref[:, cols]
            sg = _sigmoid(b_refs[p][...].astype(F32))
            out_ref[:, p * bw:(p + 1) * bw] = (du * sg).astype(out_ref.dtype)
            out_ref[:, width + p * bw:width + (p + 1) * bw] = (
                du * a_refs[p][...].astype(F32) * sg * (1.0 - sg)).astype(out_ref.dtype)

    def piece(off, p):
        return pl.BlockSpec((tr, bw), functools.partial(lambda i, blk: (i, blk), blk=off // bw + p))

    in_specs = _halo_specs(tr, width, n_rows)
    in_specs += [piece(off_a, p) for p in range(npc)] + [piece(off_b, p) for p in range(npc)]
    in_specs.append(pl.BlockSpec(w32.shape, lambda i: (0, 0)))
    (d_glu,), job_res = _call(
        name, body, (n_steps,), in_specs, [pl.BlockSpec((tr, 2 * width), lambda i: (i, 0))],
        [jax.ShapeDtypeStruct((n_rows, 2 * width), BF16)], [_ext_scratch(tr, width), pltpu.VMEM((tr, width), F32)],
        (du2, du2, du2, *([z] * (2 * npc)), w32), ("arbitrary",), jobs)
    return (d_glu, job_res) if jobs else d_glu


def _place():
    return lax.axis_index("x"), lax.axis_index("y"), lax.axis_index("c")


def _flip(v, m):
    return 1 - v if m else v


def _gather_small(name, v, masks):
    varies = [any(m[a] for m in masks) for a in range(3)]
    n = len(masks) + 1

    def slot(pos):
        idx = 0
        for a in range(3):
            if varies[a]:
                idx = idx * 2 + pos[a]
        return idx

    def body(v_ref, o_ref, send_sems, recv_sems, local_sem):
        me = _place()
        mine = pltpu.make_async_copy(v_ref, o_ref.at[slot(me)], local_sem)
        mine.start()
        peers = [tuple(_flip(me[a], m[a]) for a in range(3)) for m in masks]
        sends = [pltpu.make_async_remote_copy(v_ref, o_ref.at[slot(me)], send_sems.at[k], recv_sems.at[k],
                                              device_id=peer, device_id_type=MESH) for k, peer in enumerate(peers)]
        for cp in sends:
            cp.start()
        for k, peer in enumerate(peers):
            pltpu.make_async_remote_copy(v_ref, o_ref.at[slot(peer)], send_sems.at[k], recv_sems.at[k],
                                         device_id=peer, device_id_type=MESH).wait_recv()
        for cp in sends:
            cp.wait_send()
        mine.wait()

    return pl.pallas_call(
        body, name=name, in_specs=[HBM_SPEC], out_specs=HBM_SPEC,
        out_shape=jax.ShapeDtypeStruct((n,) + v.shape, v.dtype),
        scratch_shapes=[pltpu.SemaphoreType.DMA((n - 1,)), pltpu.SemaphoreType.DMA((n - 1,)), pltpu.SemaphoreType.DMA(())],
    )(v)


ALL_DEVICES = [(mx, my, mc) for mx in (0, 1) for my in (0, 1) for mc in (0, 1)][1:]
SAME_CORE_CHIPS = [(1, 0, 0), (0, 1, 0), (1, 1, 0)]


def _chips(x, y):
    return [(1 - x, y), (x, 1 - y), (1 - x, 1 - y)]


def _cast_into(name, w, layer, kind, chip_idx):
    _, r, cc = w.shape
    tr = _rows_within(r, cc * 6)
    steps = r // tr
    if kind == "col":
        shape, o_spec = (r, 4 * cc), pl.BlockSpec((tr, cc), lambda i, s_ref: (i, s_ref[0]))
    else:
        shape, o_spec = (4 * r, cc), pl.BlockSpec((tr, cc), lambda i, s_ref: (s_ref[0] * steps + i, 0))

    def body(s_ref, w_ref, o_ref):
        o_ref[...] = w_ref[...].astype(o_ref.dtype)

    grid_spec = pltpu.PrefetchScalarGridSpec(
        num_scalar_prefetch=1, grid=(steps,),
        in_specs=[pl.BlockSpec((None, tr, cc), lambda i, s_ref: (layer, i, 0))], out_specs=o_spec)
    return pl.pallas_call(body, name=name, grid_spec=grid_spec, out_shape=jax.ShapeDtypeStruct(shape, BF16),
                          compiler_params=_params("arbitrary"))(chip_idx, w)


class _Job:
    def __init__(self, ins, out_shapes, aliases, sems, start, finish):
        self.ins, self.out_shapes, self.aliases, self.sems = list(ins), list(out_shapes), dict(aliases), list(sems)
        self.start, self.finish = start, finish


def _job_refs(jobs, in_refs, out_refs, sem_refs):
    cut, i, o, s = [], 0, 0, 0
    for jb in jobs:
        cut.append((in_refs[i:i + len(jb.ins)], out_refs[o:o + len(jb.out_shapes)], sem_refs[s:s + len(jb.sems)]))
        i, o, s = i + len(jb.ins), o + len(jb.out_shapes), s + len(jb.sems)
    return cut


def _job_aliases(jobs, first_in, first_out):
    aliases, i, o = {}, first_in, first_out
    for jb in jobs:
        for a, b in jb.aliases.items():
            aliases[i + a] = o + b
        i, o = i + len(jb.ins), o + len(jb.out_shapes)
    return aliases


def _run_jobs(name, jobs):
    n_in = sum(len(jb.ins) for jb in jobs)
    n_out = sum(len(jb.out_shapes) for jb in jobs)

    def body(*refs):
        cut = _job_refs(jobs, refs[:n_in], refs[n_in:n_in + n_out], refs[n_in + n_out:])
        for jb, parts in zip(jobs, cut):
            jb.start(*parts)
        for jb, parts in zip(jobs, cut):
            jb.finish(*parts)

    res = pl.pallas_call(
        body, name=name, in_specs=[HBM_SPEC] * n_in, out_specs=[HBM_SPEC] * n_out,
        out_shape=[s for jb in jobs for s in jb.out_shapes], input_output_aliases=_job_aliases(jobs, 0, 0),
        scratch_shapes=[s for jb in jobs for s in jb.sems],
    )(*[a for jb in jobs for a in jb.ins])
    return _job_results(jobs, res)


def _job_results(jobs, flat):
    out, o = [], 0
    for jb in jobs:
        out.append(list(flat[o:o + len(jb.out_shapes)]))
        o += len(jb.out_shapes)
    return out


def _call(name, body, grid, in_specs, out_specs, out_shape, scratch, operands, sem, jobs=()):
    n_in, n_out, n_scr = len(in_specs), len(out_specs), len(scratch)
    n_job_in = sum(len(jb.ins) for jb in jobs)
    n_job_out = sum(len(jb.out_shapes) for jb in jobs)

    def full_body(*refs):
        ins, job_ins = refs[:n_in], refs[n_in:n_in + n_job_in]
        rest = refs[n_in + n_job_in:]
        outs, job_outs = rest[:n_out], rest[n_out:n_out + n_job_out]
        scr, sems = rest[n_out + n_job_out:n_out + n_job_out + n_scr], rest[n_out + n_job_out + n_scr:]
        cut = _job_refs(jobs, job_ins, job_outs, sems)
        ids = [pl.program_id(a) for a in range(len(grid))]
        if jobs:
            @pl.when(functools.reduce(lambda p, q: p & q, [i == 0 for i in ids]))
            def _():
                for jb, parts in zip(jobs, cut):
                    jb.start(*parts)
        body(*ins, *outs, *scr)
        if jobs:
            @pl.when(functools.reduce(lambda p, q: p & q, [i == g - 1 for i, g in zip(ids, grid)]))
            def _():
                for jb, parts in zip(jobs, cut):
                    jb.finish(*parts)

    res = pl.pallas_call(
        full_body, name=name, grid=grid, in_specs=list(in_specs) + [HBM_SPEC] * n_job_in,
        out_specs=list(out_specs) + [HBM_SPEC] * n_job_out,
        out_shape=list(out_shape) + [s for jb in jobs for s in jb.out_shapes],
        scratch_shapes=list(scratch) + [s for jb in jobs for s in jb.sems],
        input_output_aliases=_job_aliases(jobs, n_in, n_out), compiler_params=_params(*sem),
    )(*operands, *[a for jb in jobs for a in jb.ins])
    return list(res[:n_out]), _job_results(jobs, res[n_out:])


def _job_gather(fulls, shapes, kinds):
    n = len(fulls)
    for r, _ in shapes:
        assert r % 32 == 0

    def window(o_ref, j, s, h):
        r, cc = shapes[j]
        hr = r // 2
        if kinds[j] == "col":
            return o_ref.at[pl.ds(pl.multiple_of(h * hr, 16), hr), pl.ds(pl.multiple_of(s * cc, LANE), cc)]
        return o_ref.at[pl.ds(pl.multiple_of(s * r + h * hr, 16), hr), :]

    def first_copies(outs, sems):
        x, y, c = _place()
        cps = []
        for j in range(n):
            mine = window(outs[j], j, 2 * x + y, c)
            for k, chip in enumerate(_chips(x, y)):
                cps.append(pltpu.make_async_remote_copy(mine, mine, sems[0].at[3 * j + k], sems[1].at[3 * j + k],
                                                        device_id=(*chip, c), device_id_type=MESH))
        return cps

    def start(ins, outs, sems):
        for cp in first_copies(outs, sems):
            cp.start()

    def finish(ins, outs, sems):
        x, y, c = _place()
        chips = _chips(x, y)
        sibling = (x, y, 1 - c)
        passed = []
        for j in range(n):
            for k, chip in enumerate(chips):
                win = window(outs[j], j, 2 * chip[0] + chip[1], c)
                pltpu.make_async_remote_copy(win, win, sems[0].at[3 * j + k], sems[1].at[3 * j + k],
                                             device_id=(*chip, c), device_id_type=MESH).wait_recv()
                cp = pltpu.make_async_remote_copy(win, win, sems[2].at[3 * j + k], sems[3].at[3 * j + k],
                                                  device_id=sibling, device_id_type=MESH)
                cp.start()
                passed.append(cp)
        for j in range(n):
            for k, chip in enumerate(chips):
                win = window(outs[j], j, 2 * chip[0] + chip[1], 1 - c)
                pltpu.make_async_remote_copy(win, win, sems[2].at[3 * j + k], sems[3].at[3 * j + k],
                                             device_id=sibling, device_id_type=MESH).wait_recv()
        for cp in first_copies(outs, sems) + passed:
            cp.wait_send()

    return _Job(fulls, [jax.ShapeDtypeStruct(f.shape, f.dtype) for f in fulls], {j: j for j in range(n)},
                [pltpu.SemaphoreType.DMA((3 * n,)) for _ in range(4)], start, finish)


def _job_pair(grads):
    n = len(grads)

    def copies(ins, outs, sems):
        x, y, c = _place()
        return [pltpu.make_async_remote_copy(ins[j].at[1], outs[j], sems[0].at[j], sems[1].at[j],
                                             device_id=(x, y, 1 - c), device_id_type=MESH) for j in range(n)]

    def start(ins, outs, sems):
        for cp in copies(ins, outs, sems):
            cp.start()

    def finish(ins, outs, sems):
        for cp in copies(ins, outs, sems):
            cp.wait()

    return _Job(grads, [jax.ShapeDtypeStruct(g.shape[1:], g.dtype) for g in grads], {},
                [pltpu.SemaphoreType.DMA((n,)), pltpu.SemaphoreType.DMA((n,))], start, finish)


def _job_chips(halves, kinds):
    n = len(halves)
    shapes = [(h.shape[0], h.shape[1] // 4) if kinds[j] == "col" else (h.shape[0] // 4, h.shape[1])
              for j, h in enumerate(halves)]

    def part(ref, j, s):
        r, cc = shapes[j]
        if kinds[j] == "col":
            return ref.at[:, pl.ds(pl.multiple_of(s * cc, LANE), cc)]
        return ref.at[pl.ds(pl.multiple_of(s * r, 16), r), :]

    def copies(ins, outs, sems):
        x, y, c = _place()
        s_me = 2 * x + y
        local = [pltpu.make_async_copy(part(ins[j], j, s_me), outs[j].at[s_me], sems[2].at[j]) for j in range(n)]
        sends, recvs = [], []
        for j in range(n):
            for k, chip in enumerate(_chips(x, y)):
                s_peer = 2 * chip[0] + chip[1]
                sends.append(pltpu.make_async_remote_copy(part(ins[j], j, s_peer), outs[j].at[s_me],
                                                          sems[0].at[3 * j + k], sems[1].at[3 * j + k],
                                                          device_id=(*chip, c), device_id_type=MESH))
                dst = outs[j].at[s_peer]
                recvs.append(pltpu.make_async_remote_copy(dst, dst, sems[0].at[3 * j + k], sems[1].at[3 * j + k],
                                                          device_id=(*chip, c), device_id_type=MESH))
        return local, sends, recvs

    def start(ins, outs, sems):
        local, sends, _ = copies(ins, outs, sems)
        for cp in local + sends:
            cp.start()

    def finish(ins, outs, sems):
        local, sends, recvs = copies(ins, outs, sems)
        for cp in recvs:
            cp.wait_recv()
        for cp in sends:
            cp.wait_send()
        for cp in local:
            cp.wait()

    return _Job(halves, [jax.ShapeDtypeStruct((4,) + shapes[j], halves[j].dtype) for j in range(n)], {},
                [pltpu.SemaphoreType.DMA((3 * n,)), pltpu.SemaphoreType.DMA((3 * n,)), pltpu.SemaphoreType.DMA((n,))],
                start, finish)


def _job_join(pairs):
    n = len(pairs)

    def copies(outs, sems):
        x, y, c = _place()
        sends, recvs = [], []
        for j in range(n):
            sends.append(pltpu.make_async_remote_copy(outs[j].at[c], outs[j].at[c], sems[0].at[j], sems[1].at[j],
                                                      device_id=(x, y, 1 - c), device_id_type=MESH))
            theirs = outs[j].at[1 - c]
            recvs.append(pltpu.make_async_remote_copy(theirs, theirs, sems[0].at[j], sems[1].at[j],
                                                      device_id=(x, y, 1 - c), device_id_type=MESH))
        return sends, recvs

    def start(ins, outs, sems):
        for cp in copies(outs, sems)[0]:
            cp.start()

    def finish(ins, outs, sems):
        sends, recvs = copies(outs, sems)
        for cp in recvs:
            cp.wait_recv()
        for cp in sends:
            cp.wait_send()

    return _Job(pairs, [jax.ShapeDtypeStruct(p.shape, p.dtype) for p in pairs], {j: j for j in range(n)},
                [pltpu.SemaphoreType.DMA((n,)), pltpu.SemaphoreType.DMA((n,))], start, finish)


def _pair_sum(name, mine_other, got):
    _, r, cc = mine_other.shape
    tr = _rows_within(r, 3 * cc * mine_other.dtype.itemsize, (256, 128, 64, 32, 16))

    def body(a_ref, b_ref, o_ref):
        o_ref[...] = (a_ref[...].astype(F32) + b_ref[...].astype(F32)).astype(o_ref.dtype)

    return pl.pallas_call(
        body, name=name, grid=(r // tr,),
        in_specs=[pl.BlockSpec((None, tr, cc), lambda i: (0, i, 0)), pl.BlockSpec((tr, cc), lambda i: (i, 0))],
        out_specs=pl.BlockSpec((tr, cc), lambda i: (i, 0)),
        out_shape=jax.ShapeDtypeStruct((r, cc), mine_other.dtype), compiler_params=_params("parallel"),
    )(mine_other, got)


def _sum_slots(name, parts, into_slot=None):
    n, r, cc = parts.shape
    tr = _rows_within(r, cc * (n * parts.dtype.itemsize + 4), (256, 128, 64, 32, 16, 8))

    def body(*refs):
        p_ref, o_ref = refs[-2:]
        acc = p_ref[0].astype(F32)
        for s in range(1, n):
            acc = acc + p_ref[s].astype(F32)
        o_ref[...] = acc

    in_spec = pl.BlockSpec((n, tr, cc), lambda i, *_: (0, i, 0))
    if into_slot is None:
        return pl.pallas_call(
            body, name=name, grid=(r // tr,), in_specs=[in_spec], out_specs=pl.BlockSpec((tr, cc), lambda i: (i, 0)),
            out_shape=jax.ShapeDtypeStruct((r, cc), F32), compiler_params=_params("parallel"),
        )(parts)
    grid_spec = pltpu.PrefetchScalarGridSpec(
        num_scalar_prefetch=1, grid=(r // tr,), in_specs=[in_spec],
        out_specs=pl.BlockSpec((None, tr, cc), lambda i, c_ref: (c_ref[0], i, 0)))
    return pl.pallas_call(body, name=name, grid_spec=grid_spec, out_shape=jax.ShapeDtypeStruct((2, r, cc), F32),
                          compiler_params=_params("arbitrary"))(into_slot, parts)


def _adamw_math(w, g, m, v):
    m = ADAM_B1 * m + (1.0 - ADAM_B1) * g
    v = ADAM_B2 * v + (1.0 - ADAM_B2) * jnp.square(g)
    m_hat = m / (1.0 - ADAM_B1 ** ADAM_STEP)
    v_hat = v / (1.0 - ADAM_B2 ** ADAM_STEP)
    delta = -ADAM_LR * (m_hat / (jnp.sqrt(v_hat) + ADAM_EPS) + ADAM_WD * w)
    return delta, m, v


def _adamw_layer(name, layer, g_pair, kind, w, m, v, prev):
    n_layers, r, cc = w.shape
    if prev is None:
        prev = tuple(lax.empty(w.shape, F32) for _ in range(4))
    if kind == "col":
        g = g_pair.reshape(r, cc)
        tr = _rows_within(r, 8 * cc * 4)
        grid = (r // tr,)
        g_spec = pl.BlockSpec((tr, cc), lambda i: (i, 0))
        blk = pl.BlockSpec((None, tr, cc), lambda i: (layer, i, 0))
    else:
        g = g_pair
        tr = _rows_within(r, 4 * cc * 4)
        grid = (r // tr, 2)
        g_spec = pl.BlockSpec((None, tr, cc // 2), lambda i, h: (h, i, 0))
        blk = pl.BlockSpec((None, tr, cc // 2), lambda i, h: (layer, i, h))

    def body(g_ref, w_ref, m_ref, v_ref, *rest):
        og_ref, od_ref, om_ref, ov_ref = rest[4:]
        gv = g_ref[...]
        delta, m2, v2 = _adamw_math(w_ref[...], gv, m_ref[...], v_ref[...])
        og_ref[...] = gv
        od_ref[...] = delta
        om_ref[...] = m2
        ov_ref[...] = v2

    return pl.pallas_call(
        body, name=name, grid=grid,
        in_specs=[g_spec, blk, blk, blk] + [HBM_SPEC] * 4,
        out_specs=[blk] * 4, out_shape=[jax.ShapeDtypeStruct(w.shape, F32)] * 4,
        input_output_aliases={4: 0, 5: 1, 6: 2, 7: 3}, compiler_params=_params(*(["parallel"] * len(grid))),
    )(g, w, m, v, *prev)


def _adamw_small(name, g, w, m, v):
    def body(g_ref, w_ref, m_ref, v_ref, od_ref, om_ref, ov_ref):
        delta, m2, v2 = _adamw_math(w_ref[...], g_ref[...], m_ref[...], v_ref[...])
        od_ref[...] = delta
        om_ref[...] = m2
        ov_ref[...] = v2

    return pl.pallas_call(body, name=name, out_shape=[jax.ShapeDtypeStruct(w.shape, F32)] * 3)(g, w, m, v)


def _adamw_ada(name, c16, dmod16, w, m, v):
    n_layers, d, cols = w.shape
    tr = _rows_within(d, 7 * cols * 4, (256, 128))
    blk = pl.BlockSpec((None, tr, cols), lambda l, i: (l, i, 0))

    def body(c_ref, dm_ref, w_ref, m_ref, v_ref, og_ref, od_ref, om_ref, ov_ref):
        gv = lax.dot_general(c_ref[...], dm_ref[...], (((0,), (0,)), ((), ())), preferred_element_type=F32)
        delta, m2, v2 = _adamw_math(w_ref[...], gv, m_ref[...], v_ref[...])
        og_ref[...] = gv
        od_ref[...] = delta
        om_ref[...] = m2
        ov_ref[...] = v2

    return pl.pallas_call(
        body, name=name, grid=(n_layers, d // tr),
        in_specs=[pl.BlockSpec((16, tr), lambda l, i: (0, i)), pl.BlockSpec((None, 16, cols), lambda l, i: (l, 0, 0)),
                  blk, blk, blk],
        out_specs=[blk] * 4, out_shape=[jax.ShapeDtypeStruct(w.shape, F32)] * 4,
        compiler_params=_params("parallel", "parallel"),
    )(c16, dmod16, w, m, v)


def kernel(x, c, w_ada, b_ada, w_in, sink, w_dw, conv_ln_g, conv_ln_b, w_oa, w_ob, w_out, ln1_g, ln1_b, w_gu, w_down, ln2_g, ln2_b, loss_target, m_w_ada, m_b_ada, m_w_in, m_sink, m_w_dw, m_conv_ln_g, m_conv_ln_b, m_w_oa, m_w_ob, m_w_out, m_ln1_g, m_ln1_b, m_w_gu, m_w_down, m_ln2_g, m_ln2_b, v_w_ada, v_b_ada, v_w_in, v_sink, v_w_dw, v_conv_ln_g, v_conv_ln_b, v_w_oa, v_w_ob, v_w_out, v_ln1_g, v_ln1_b, v_w_gu, v_w_down, v_ln2_g, v_ln2_b):
    seq, d = x.shape[1], x.shape[2]
    n_layers = w_in.shape[0]
    d_in = 4 * w_in.shape[2]
    d_ff = 4 * w_down.shape[1]
    hq = d // HEAD_DIM
    dkv = (hq // GQA_GROUP) * HEAD_DIM
    off_k, off_v, off_ga, off_gb = d, d + dkv, d + 2 * dkv, 2 * d + 2 * dkv
    off_gta, off_gtb = 3 * d + 2 * dkv, 4 * d + 2 * dkv
    assert d_in == 5 * d + 2 * dkv and seq % ROW_TILE == 0 and seq >= BAND
    alpha = (2.0 * n_layers) ** 0.25

    xi, yi, ci = _place()
    chip = 2 * xi + yi
    batch = 4 * xi + 2 * yi + ci
    c_idx = jnp.reshape(ci, (1,)).astype(jnp.int32)
    x2 = x[0]
    target = loss_target[0]

    c_act = jax.nn.silu(c)
    c_all = _gather_small("gather_c", c_act, ALL_DEVICES).reshape(8, d)
    c16 = jnp.concatenate([c_all, jnp.zeros((8, d), F32)], axis=0).astype(BF16)
    mod_cols = [_mm(f"mod_{l}", c16, w_ada, "nn", F32, b_layer=l) for l in range(n_layers)]
    mod_all = _gather_small("gather_mod", jnp.stack(mod_cols), SAME_CORE_CHIPS)
    mod = lax.dynamic_index_in_dim(mod_all, batch, axis=2, keepdims=False)
    mod = jnp.transpose(mod, (1, 0, 2)).reshape(n_layers, N_MOD * d) + b_ada
    mod = mod.reshape(n_layers, N_MOD, 1, d)
    sh_a, sc_a, gt_a, sh_f, sc_f, gt_f = (mod[:, j] for j in range(N_MOD))

    pos = jnp.arange(seq, dtype=F32)
    inv_freq = ROPE_THETA ** (-jnp.arange(0, ROPE_DIM, 2, dtype=F32) / ROPE_DIM)
    ang = pos[:, None] * inv_freq[None, :]
    cos, sin = jnp.cos(ang), jnp.sin(ang)
    half = ROPE_DIM // 2
    rest = HEAD_DIM - ROPE_DIM
    t_cs = jnp.concatenate([cos, cos, jnp.ones((seq, rest), F32)], axis=1)
    t_up = jnp.concatenate([-sin, jnp.zeros((seq, rest + half), F32)], axis=1)
    t_dn = jnp.concatenate([jnp.zeros((seq, half), F32), sin, jnp.zeros((seq, rest), F32)], axis=1)

    def rope(t, cs, up, dn):
        w = t.shape[1]
        reps = (1, w // HEAD_DIM)
        return (t * jnp.tile(cs, reps) + pltpu.roll(t, w - half, 1) * jnp.tile(up, reps)
                + pltpu.roll(t, half, 1) * jnp.tile(dn, reps))

    def rope_t(dt, cs, up, dn):
        w = dt.shape[1]
        reps = (1, w // HEAD_DIM)
        return (dt * jnp.tile(cs, reps) + pltpu.roll(dt * jnp.tile(up, reps), half, 1)
                + pltpu.roll(dt * jnp.tile(dn, reps), w - half, 1))

    tables = [(t_cs, 0, HEAD_DIM), (t_up, 0, HEAD_DIM), (t_dn, 0, HEAD_DIM)]

    kinds = ("col", "col", "row", "row", "row", "row")
    big_weights = (w_in, w_gu, w_oa, w_ob, w_out, w_down)
    chip_idx = jnp.reshape(chip, (1,)).astype(jnp.int32)
    shard_shapes = [w.shape[1:] for w in big_weights]
    fulls = [[_cast_into(f"cast_w_{l}_{j}", w, l, kinds[j], chip_idx) for j, w in enumerate(big_weights)]
             for l in range(n_layers)]

    def gather_job(l, which):
        return _job_gather([fulls[l][j] for j in which], [shard_shapes[j] for j in which], [kinds[j] for j in which])

    ride_in_proj, ride_ffn_up, ride_attn_out, ride_conv_out, ride_mix_out, ride_ffn_down = (0,), (1,), (2,), (3,), (4,), (5,)
    ahead, ride_attn_0, ride_conv_0 = (0, 2), (3, 4, 5), (1,)
    gathered = [[None] * 6 for _ in range(n_layers)]

    def keep(layer, which, arrays):
        for j, arr in zip(which, arrays):
            gathered[layer][j] = arr

    keep(0, ahead, _run_jobs("gather_w_0", [gather_job(0, ahead)])[0])
    w_dw_all = _gather_small("gather_dw", w_dw, SAME_CORE_CHIPS)
    w_dw_full = jnp.transpose(w_dw_all, (1, 2, 0, 3)).reshape(n_layers, CONV_WIDTH, d)
    w_dw32 = jnp.pad(w_dw_full, ((0, 0), (0, 32 - CONV_WIDTH), (0, 0)))
    sink_b = jnp.broadcast_to(sink[:, :, None], (n_layers, hq, LANE))

    def vec(a, l):
        return a[l][None, :]

    def res_ln(xprev, y, gt, g, b, scn, shn):
        xn = _ln(alpha * xprev + (1.0 + gt) * y, g, b)
        return xn, xn * (1.0 + scn) + shn

    def merge(ya, yb, ga, gb):
        return _sigmoid(ga) * ya + _sigmoid(gb) * yb

    def swiglu(gate, up):
        return gate * _sigmoid(gate) * up

    h = _rowwise("modulate_in", lambda xv, sc, sh: xv * (1.0 + sc) + sh, seq, [(x2, 0, d)], [sc_a[0], sh_a[0]],
                 [(d, BF16)])
    xprev = x2
    saved = []
    for l in range(n_layers):
        nxt = l + 1 < n_layers

        def mm_carrying(name, a_, j_weight, which, out_dtype=F32):
            if not nxt:
                return _mm(name, a_, gathered[l][j_weight], "nn", out_dtype)
            res, (got,) = _mm(name, a_, gathered[l][j_weight], "nn", out_dtype, jobs=[gather_job(l + 1, which)])
            keep(l + 1, which, got)
            return res

        z = mm_carrying(f"in_proj_{l}", h, 0, ride_in_proj, BF16)
        qr, kr, vb = _rowwise(
            f"qkv_prep_{l}", lambda q, k, v, cs, up, dn: (rope(q, cs, up, dn), rope(k, cs, up, dn), v), seq,
            [(z, 0, d), (z, off_k, dkv), (z, off_v, dkv)] + tables, [], [(d, BF16), (dkv, BF16), (dkv, BF16)])
        if l == 0:
            att, (got,) = _attn_fwd(f"attn_{l}", qr, kr, vb, sink_b[l], jobs=[gather_job(0, ride_attn_0)])
            keep(0, ride_attn_0, got)
        else:
            att = _attn_fwd(f"attn_{l}", qr, kr, vb, sink_b[l])
        y_a = mm_carrying(f"attn_out_{l}", att, 2, ride_attn_out)
        u = _rowwise(f"glu_{l}", lambda a, b: a * _sigmoid(b), seq, [(z, off_ga, d), (z, off_gb, d)], [], [(d, F32)])
        if l == 0:
            (u2, cv), (got,) = _conv_fwd(f"conv_{l}", u, w_dw32[l], vec(conv_ln_g, l), vec(conv_ln_b, l),
                                         jobs=[gather_job(0, ride_conv_0)])
            keep(0, ride_conv_0, got)
        else:
            u2, cv = _conv_fwd(f"conv_{l}", u, w_dw32[l], vec(conv_ln_g, l), vec(conv_ln_b, l))
        y_b = mm_carrying(f"conv_out_{l}", cv, 3, ride_conv_out)
        mg = _rowwise(f"merge_{l}", merge, seq, [(y_a, 0, d), (y_b, 0, d), (z, off_gta, d), (z, off_gtb, d)], [],
                      [(d, BF16)])
        o = mm_carrying(f"mix_out_{l}", mg, 4, ride_mix_out)
        x1, h2 = _rowwise(f"res_ln1_{l}", res_ln, seq, [(xprev, 0, d), (o, 0, d)],
                          [gt_a[l], vec(ln1_g, l), vec(ln1_b, l), sc_f[l], sh_f[l]], [(d, F32), (d, BF16)])
        gu = mm_carrying(f"ffn_up_{l}", h2, 1, ride_ffn_up, BF16)
        f = _rowwise(f"swiglu_{l}", swiglu, seq, [(gu, 0, d_ff), (gu, d_ff, d_ff)], [], [(d_ff, BF16)])
        ffn = mm_carrying(f"ffn_down_{l}", f, 5, ride_ffn_down)
        saved.append(dict(xprev=xprev, h=h, z=z, qr=qr, kr=kr, vb=vb, att=att, u=u, u2=u2, cv=cv, y_a=y_a, y_b=y_b,
                          mg=mg, o=o, x1=x1, h2=h2, gu=gu, f=f, ffn=ffn))
        if l + 1 < n_layers:
            xprev, h = _rowwise(f"res_ln2_{l}", res_ln, seq, [(x1, 0, d), (ffn, 0, d)],
                                [gt_f[l], vec(ln2_g, l), vec(ln2_b, l), sc_a[l + 1], sh_a[l + 1]], [(d, F32), (d, BF16)])

    def res_ln_bwd(xp, y, dxn, dh, gt, g, b, scn, shn):
        _, vjp = jax.vjp(res_ln, xp, y, gt, g, b, scn, shn)
        return vjp((dxn, dh))

    def last_ln_bwd(xp, y, tgt, gt, g, b):
        def head(xp_, y_, gt_, g_, b_):
            return _ln(alpha * xp_ + (1.0 + gt_) * y_, g_, b_)
        out, vjp = jax.vjp(head, xp, y, gt, g, b)
        err = out - tgt
        loss = 0.5 * jnp.sum(jnp.sum(err * err, axis=-1, keepdims=True) / d, axis=0, keepdims=True)
        return vjp(err / d) + (jnp.broadcast_to(loss, (1, LANE)),)

    def merge_bwd(dmg, ya, yb, ga, gb):
        _, vjp = jax.vjp(merge, ya, yb, ga, gb)
        dya, dyb, dga, dgb = vjp(dmg)
        return dya, dyb, jnp.concatenate([dga, dgb], axis=1)

    def swiglu_bwd(df, gate, up):
        _, vjp = jax.vjp(swiglu, gate, up)
        return jnp.concatenate(vjp(df), axis=1)

    vec_d = ((1, d), F32)
    small = [None] * n_layers
    big = None
    loss_part = None
    dxn = dh = None
    pending = None
    stacks = ((w_in, m_w_in, v_w_in), (w_gu, m_w_gu, v_w_gu), (w_oa, m_w_oa, v_w_oa), (w_ob, m_w_ob, v_w_ob),
              (w_out, m_w_out, v_w_out), (w_down, m_w_down, v_w_down))

    def adamw_all(layer, full, prev):
        return [_adamw_layer(f"adamw_{layer}_{j}", layer, full[j], kinds[j], *stacks[j], None if prev is None else prev[j])
                for j in range(6)]

    for l in reversed(range(n_layers)):
        sv = saved[l]
        wi, wg, woa, wob, wout, wdn = gathered[l]
        ln2 = [gt_f[l], vec(ln2_g, l), vec(ln2_b, l)]
        if l + 1 == n_layers:
            dx1, dffn, d_gtf, d_g2, d_b2, loss_part = _rowwise(
                "last_ln_bwd", last_ln_bwd, seq, [(sv["x1"], 0, d), (sv["ffn"], 0, d), (target, 0, d)], ln2,
                [(d, F32), (d, BF16)], [vec_d, vec_d, vec_d, ((1, LANE), F32)])
            d_sca_next = d_sha_next = None
        else:
            dx1, dffn, d_gtf, d_g2, d_b2, d_sca_next, d_sha_next = _rowwise(
                f"res_ln2_bwd_{l}", res_ln_bwd, seq, [(sv["x1"], 0, d), (sv["ffn"], 0, d), (dxn, 0, d), (dh, 0, d)],
                ln2 + [sc_a[l + 1], sh_a[l + 1]], [(d, F32), (d, BF16)], [vec_d] * 5)
            small[l + 1]["sc_a"], small[l + 1]["sh_a"] = d_sca_next, d_sha_next
        def riding(make_job, *job_args):
            return [] if pending is None else [make_job(*job_args)]

        def unpack(res):
            return res if pending is not None else (res, [None])

        df, (got,) = unpack(_mm(f"ffn_down_dx_{l}", dffn, wdn, "nt", BF16, jobs=riding(_job_pair, pending)))
        halves = None if pending is None else [
            _pair_sum(f"pair_sum_{l + 1}_{j}", pending[j], got[j]) for j in range(6)]
        g_down = _mm(f"ffn_down_dw_{l}", sv["f"], dffn, "tn", BF16, split="cols", c_idx=c_idx)
        dgu = _rowwise(f"swiglu_bwd_{l}", swiglu_bwd, seq, [(df, 0, d_ff), (sv["gu"], 0, d_ff), (sv["gu"], d_ff, d_ff)],
                       [], [(2 * d_ff, BF16)])
        dh2, (parts_in,) = unpack(_mm(f"ffn_up_dx_{l}", dgu, wg, "nt", F32,
                                      jobs=riding(lambda: _job_chips(halves[:1], kinds[:1]))))
        g_gu, (parts_gu,) = unpack(_mm(f"ffn_up_dw_{l}", sv["h2"], dgu, "tn", BF16, split="rows", c_idx=c_idx,
                                       jobs=riding(lambda: _job_chips(halves[1:2], kinds[1:2]))))
        dxp, d_o, d_gta, d_g1, d_b1, d_scf, d_shf = _rowwise(
            f"res_ln1_bwd_{l}", res_ln_bwd, seq, [(sv["xprev"], 0, d), (sv["o"], 0, d), (dx1, 0, d), (dh2, 0, d)],
            [gt_a[l], vec(ln1_g, l), vec(ln1_b, l), sc_f[l], sh_f[l]], [(d, F32), (d, BF16)], [vec_d] * 5)
        dmg = _mm(f"mix_out_dx_{l}", d_o, wout, "nt", F32)
        g_out = _mm(f"mix_out_dw_{l}", sv["mg"], d_o, "tn", BF16, split="cols", c_idx=c_idx)
        z = sv["z"]
        dya, dyb, d_gates = _rowwise(
            f"merge_bwd_{l}", merge_bwd, seq,
            [(dmg, 0, d), (sv["y_a"], 0, d), (sv["y_b"], 0, d), (z, off_gta, d), (z, off_gtb, d)], [],
            [(d, BF16), (d, BF16), (2 * d, BF16)])
        d_att = _mm(f"attn_out_dx_{l}", dya, woa, "nt", BF16)
        g_oa = _mm(f"attn_out_dw_{l}", sv["att"], dya, "tn", BF16, split="cols", c_idx=c_idx)
        d_cv = _mm(f"conv_out_dx_{l}", dyb, wob, "nt", F32)
        g_ob = _mm(f"conv_out_dw_{l}", sv["cv"], dyb, "tn", BF16, split="cols", c_idx=c_idx)
        if l > 0:
            du2, d_wdw, d_cg, d_cb = _conv_bwd_a(f"conv_bwd_a_{l}", sv["u"], sv["u2"], d_cv, vec(conv_ln_g, l),
                                                 vec(conv_ln_b, l))
            d_glu = _conv_bwd_b(f"conv_bwd_b_{l}", du2, z, off_ga, off_gb, w_dw32[l])
            dqr, dkr, dvb, d_sink = _attn_bwd(f"attn_bwd_{l}", sv["qr"], sv["kr"], sv["vb"], sink_b[l], d_att)
        else:
            early = {1: g_gu, 2: g_oa, 3: g_ob, 4: g_out, 5: g_down}
            (du2, d_wdw, d_cg, d_cb), (got_0,) = _conv_bwd_a(
                f"conv_bwd_a_{l}", sv["u"], sv["u2"], d_cv, vec(conv_ln_g, l), vec(conv_ln_b, l),
                jobs=[_job_pair(list(early.values()))])
            halves_0 = {j: _pair_sum(f"pair_sum_0_{j}", early[j], got_0[n]) for n, j in enumerate(early)}
            parts_0 = {}
            d_glu, ((parts_0[5],),) = _conv_bwd_b(f"conv_bwd_b_{l}", du2, z, off_ga, off_gb, w_dw32[l],
                                                  jobs=[_job_chips([halves_0[5]], [kinds[5]])])
            (dqr, dkr, dvb, d_sink), ((parts_0[1],),) = _attn_bwd(
                f"attn_bwd_{l}", sv["qr"], sv["kr"], sv["vb"], sink_b[l], d_att,
                jobs=[_job_chips([halves_0[1]], [kinds[1]])])
        d_qkv = _rowwise(
            f"qkv_bwd_{l}",
            lambda dq_, dk_, dv_, cs, up, dn: jnp.concatenate([rope_t(dq_, cs, up, dn), rope_t(dk_, cs, up, dn), dv_], axis=1),
            seq, [(dqr, 0, d), (dkr, 0, dkv), (dvb, 0, dkv)] + tables, [], [(d + 2 * dkv, BF16)])
        dz = jnp.concatenate([d_qkv, d_glu, d_gates], axis=1)
        dh, (parts_rest,) = unpack(_mm(f"in_proj_dx_{l}", dz, wi, "nt", F32,
                                       jobs=riding(lambda: _job_chips(halves[2:], kinds[2:]))))
        reduced = None if pending is None else [
            _sum_slots(f"sum_chips_{l + 1}_{j}", p, into_slot=c_idx) for j, p in enumerate(parts_in + parts_gu + parts_rest)]
        last_jobs = riding(lambda: _job_join(reduced))
        if l == 0:
            last_jobs = last_jobs + [_job_chips([halves_0[j] for j in (2, 3, 4)], kinds[2:5])]
        g_in = _mm(f"in_proj_dw_{l}", sv["h"], dz, "tn", BF16, split="rows", c_idx=c_idx, jobs=last_jobs)
        if last_jobs:
            g_in, job_res = g_in
            if l == 0:
                parts_0[2], parts_0[3], parts_0[4] = job_res[-1]
            if pending is not None:
                big = adamw_all(l + 1, job_res[0], big)
        dxn = dxp
        small[l] = dict(gt_a=d_gta, sh_f=d_shf, sc_f=d_scf, gt_f=d_gtf, ln1_g=d_g1, ln1_b=d_b1, ln2_g=d_g2, ln2_b=d_b2,
                        conv_ln_g=d_cg, conv_ln_b=d_cb, sink=d_sink[:, :1].reshape(1, hq), w_dw=d_wdw[:CONV_WIDTH])
        pending = [g_in, g_gu, g_oa, g_ob, g_out, g_down]

    ((got_in,),) = _run_jobs("rs_pair_0", [_job_pair(pending[:1])])
    half_in = _pair_sum("pair_sum_0_0", pending[0], got_in)
    ((parts_0[0],),) = _run_jobs("rs_chips_0", [_job_chips([half_in], kinds[:1])])
    reduced = [_sum_slots(f"sum_chips_0_{j}", parts_0[j], into_slot=c_idx) for j in range(6)]
    (full,) = _run_jobs("rs_join_0", [_job_join(reduced)])
    big = adamw_all(0, full, big)

    grad_x, d_sca0, d_sha0 = _rowwise(
        "modulate_in_bwd", lambda xv, dhv, dxv, sc: (dxv + dhv * (1.0 + sc), jnp.sum(dhv * xv, axis=0, keepdims=True),
                                                     jnp.sum(dhv, axis=0, keepdims=True)),
        seq, [(x2, 0, d), (dh, 0, d), (dxn, 0, d)], [sc_a[0]], [(d, F32)], [vec_d, vec_d])
    small[0]["sc_a"], small[0]["sh_a"] = d_sca0, d_sha0

    order = ("sh_a", "sc_a", "gt_a", "sh_f", "sc_f", "gt_f", "conv_ln_g", "conv_ln_b", "ln1_g", "ln1_b", "ln2_g", "ln2_b")
    rows = []
    for l in range(n_layers):
        rows += [small[l][k] for k in order]
        rows.append(jnp.pad(small[l]["sink"], ((0, 0), (0, d - hq))))
        rows.append(small[l]["w_dw"])
    rows.append(jnp.pad(loss_part, ((0, 0), (0, d - LANE))))
    n_small = sum(r.shape[0] for r in rows)
    pad_rows = (-n_small) % 8
    packed = jnp.concatenate(rows + [jnp.zeros((pad_rows, d), F32)], axis=0)
    everyone = _gather_small("gather_small_grads", packed, ALL_DEVICES)
    total = _sum_slots("sum_small_grads", everyone)
    per_layer = len(order) + 1 + CONV_WIDTH
    tot = total[:n_layers * per_layer].reshape(n_layers, per_layer, d)
    g_mod = tot[:, :N_MOD].reshape(n_layers, N_MOD * d)
    g_small = {k: tot[:, N_MOD + j] for j, k in enumerate(order[N_MOD:])}
    g_sink = tot[:, len(order), :hq]
    g_dw_full = tot[:, len(order) + 1:]
    cols_dw = w_dw.shape[2]
    g_dw = lax.dynamic_slice_in_dim(g_dw_full, chip * cols_dw, cols_dw, axis=2)
    loss = total[n_layers * per_layer, 0]

    d_mod_all = everyone[:, :n_layers * per_layer].reshape(8, n_layers, per_layer, d)[:, :, :N_MOD]
    d_mod_all = d_mod_all.reshape(8, n_layers, N_MOD * d)
    cols_ada = w_ada.shape[2]
    d_mod_mine = lax.dynamic_slice_in_dim(d_mod_all, chip * cols_ada, cols_ada, axis=2)
    dmod16 = jnp.concatenate([d_mod_mine, jnp.zeros_like(d_mod_mine)], axis=0)
    dmod16 = jnp.transpose(dmod16, (1, 0, 2)).astype(BF16)
    ada = _adamw_ada("adamw_ada", c16, dmod16, w_ada, m_w_ada, v_w_ada)

    def small_step(name, g, w, m, v):
        shp = w.shape
        g2, w2, m2, v2 = (a.reshape(-1, shp[-1]) for a in (g, w, m, v))
        return (g,) + tuple(a.reshape(shp) for a in _adamw_small(name, g2, w2, m2, v2))

    res = {
        "w_ada": ada,
        "b_ada": small_step("adamw_b_ada", g_mod, b_ada, m_b_ada, v_b_ada),
        "sink": small_step("adamw_sink", g_sink, sink, m_sink, v_sink),
        "w_dw": small_step("adamw_w_dw", g_dw, w_dw, m_w_dw, v_w_dw),
        "conv_ln_g": small_step("adamw_conv_ln_g", g_small["conv_ln_g"], conv_ln_g, m_conv_ln_g, v_conv_ln_g),
        "conv_ln_b": small_step("adamw_conv_ln_b", g_small["conv_ln_b"], conv_ln_b, m_conv_ln_b, v_conv_ln_b),
        "ln1_g": small_step("adamw_ln1_g", g_small["ln1_g"], ln1_g, m_ln1_g, v_ln1_g),
        "ln1_b": small_step("adamw_ln1_b", g_small["ln1_b"], ln1_b, m_ln1_b, v_ln1_b),
        "ln2_g": small_step("adamw_ln2_g", g_small["ln2_g"], ln2_g, m_ln2_g, v_ln2_g),
        "ln2_b": small_step("adamw_ln2_b", g_small["ln2_b"], ln2_b, m_ln2_b, v_ln2_b),
        "w_in": big[0], "w_gu": big[1], "w_oa": big[2], "w_ob": big[3], "w_out": big[4], "w_down": big[5],
    }
    names = ("w_ada", "b_ada", "w_in", "sink", "w_dw", "conv_ln_g", "conv_ln_b", "w_oa", "w_ob", "w_out", "ln1_g", "ln1_b",
             "w_gu", "w_down", "ln2_g", "ln2_b")
    outs = [loss, grad_x[None]]
    for field in range(4):
        outs += [res[k][field] for k in names]
    return tuple(outs)
```

```python
import functools
import math

import jax
import jax.numpy as jnp
from jax import lax
from jax.experimental import pallas as pl
from jax.experimental.pallas import tpu as pltpu

F32 = jnp.float32
BF16 = jnp.bfloat16
MESH = pl.DeviceIdType.MESH

HEAD_DIM = 128
GQA_GROUP = 4
WINDOW = 128
BLOCK = 128
BAND = 3 * BLOCK
ROPE_DIM = HEAD_DIM // 4
ROPE_THETA = 500000.0
CONV_WIDTH = 31
CONV_PAD = CONV_WIDTH // 2
CONV_HALO = 16
N_MOD = 6
LN_EPS = 1e-5
NEG_INF = -1e30
ADAM_LR = 0.001
ADAM_B1 = 0.9
ADAM_B2 = 0.999
ADAM_EPS = 1e-08
ADAM_WD = 0.01
ADAM_STEP = 10

LANE = 128
SUBLANES = 8
V7X_VMEM_LIMIT = 56 * 1024 * 1024
ROW_TILE = 256
CONV_ROWS = 32
CONV_LANES = 256

HBM_SPEC = pl.BlockSpec(memory_space=pltpu.HBM)


def _params(*sem):
    return pltpu.CompilerParams(dimension_semantics=sem, vmem_limit_bytes=V7X_VMEM_LIMIT)


def _pick(n, cands, even=False):
    for t in cands:
        if n % t == 0 and (not even or (n // t) % 2 == 0):
            return t
    raise ValueError(f"no tile for {n} in {cands}")


BLOCK_BUDGET = 10 * 1024 * 1024
MM_BLOCK_BUDGET = 40 * 1024 * 1024


def _rows_within(n_rows, bytes_per_row, cands=(256, 128, 64, 32, 16, 8)):
    fit = [t for t in cands if n_rows % t == 0]
    for t in fit:
        if t * bytes_per_row <= BLOCK_BUDGET:
            return t
    return fit[-1]


def _sigmoid(v):
    return jax.nn.sigmoid(v)


def _const_map(ndim):
    return lambda *_: (0,) * ndim


def _rowwise(name, fn, n_rows, row_ins, vec_ins, row_outs, vec_outs=()):
    per_row = sum(w * a.dtype.itemsize for a, _, w in row_ins) + sum(w * jnp.dtype(dt).itemsize for w, dt in row_outs)
    tr = _rows_within(n_rows, per_row, (ROW_TILE, 128, 64))
    in_specs, args, pieces = [], [], []
    for arr, off, width in row_ins:
        bw = math.gcd(off, width) if off else width
        assert bw % LANE == 0 and arr.shape[0] == n_rows
        pieces.append(width // bw)
        for p in range(width // bw):
            in_specs.append(pl.BlockSpec((tr, bw), functools.partial(lambda i, blk: (i, blk), blk=off // bw + p)))
            args.append(arr)
    for v in vec_ins:
        in_specs.append(pl.BlockSpec(v.shape, _const_map(v.ndim)))
        args.append(v)
    out_shape = [jax.ShapeDtypeStruct((n_rows, w), dt) for w, dt in row_outs]
    out_specs = [pl.BlockSpec((tr, w), lambda i: (i, 0)) for w, _ in row_outs]
    for shp, dt in vec_outs:
        out_shape.append(jax.ShapeDtypeStruct(shp, dt))
        out_specs.append(pl.BlockSpec(shp, _const_map(len(shp))))
    n_in, n_row_out = len(args), len(row_outs)

    def body(*refs):
        in_refs, out_refs = refs[:n_in], refs[n_in:]
        vals, k = [], 0
        for npc in pieces:
            ps = [in_refs[k + p][...] for p in range(npc)]
            k += npc
            vals.append((ps[0] if npc == 1 else jnp.concatenate(ps, axis=1)).astype(F32))
        for _ in vec_ins:
            vals.append(in_refs[k][...])
            k += 1
        outs = fn(*vals)
        if not isinstance(outs, (tuple, list)):
            outs = (outs,)
        assert len(outs) == len(out_refs)
        for j in range(n_row_out):
            out_refs[j][...] = outs[j].astype(out_refs[j].dtype)
        if vec_outs:
            @pl.when(pl.program_id(0) == 0)
            def _():
                for j in range(n_row_out, len(out_refs)):
                    out_refs[j][...] = jnp.zeros(out_refs[j].shape, out_refs[j].dtype)
            for j in range(n_row_out, len(out_refs)):
                out_refs[j][...] += outs[j].astype(out_refs[j].dtype)

    res = pl.pallas_call(
        body, name=name, grid=(n_rows // tr,), in_specs=in_specs, out_specs=out_specs, out_shape=out_shape,
        compiler_params=_params("arbitrary"),
    )(*args)
    return res[0] if len(res) == 1 else res


def _mm(name, a, b, mode, out_dtype, b_layer=None, split=None, c_idx=None, jobs=()):
    bshape = b.shape[1:] if b_layer is not None else b.shape
    if mode == "nn":
        (m, k), (k2, n) = a.shape, bshape
        dims = (((1,), (0,)), ((), ()))
    elif mode == "nt":
        (m, k), (n, k2) = a.shape, bshape
        dims = (((1,), (1,)), ((), ()))
    else:
        (k, m), (k2, n) = a.shape, bshape
        dims = (((0,), (0,)), ((), ()))
    assert k == k2, (name, a.shape, b.shape)
    tm = _pick(m, (1024, 512, 256, 128, 16), even=(split == "rows"))
    tn = _pick(n, (1024, 512, 256, 128), even=(split == "cols"))
    out_bytes = jnp.dtype(out_dtype).itemsize
    b_bytes = b.dtype.itemsize

    def blocks_fit(t):
        acc = 0 if t == k else tm * tn * 4
        return 2 * (tm * t * a.dtype.itemsize + tn * t * b_bytes + tm * tn * out_bytes) + acc <= MM_BLOCK_BUDGET

    tk = next(t for t in (4096, 2816, 2048, 1408, 1024, 704, 512, 256, 128) if k % t == 0 and (blocks_fit(t) or t == 128))
    ni, nj, nk = m // tm, n // tn, k // tk

    if mode == "nn":
        a_spec = pl.BlockSpec((tm, tk), lambda i, j, kk, *_: (i, kk))
        b_blk, b_map = (tk, tn), (lambda i, j, kk: (kk, j))
    elif mode == "nt":
        a_spec = pl.BlockSpec((tm, tk), lambda i, j, kk, *_: (i, kk))
        b_blk, b_map = (tn, tk), (lambda i, j, kk: (j, kk))
    else:
        a_spec = pl.BlockSpec((tk, tm), lambda i, j, kk, *_: (kk, i))
        b_blk, b_map = (tk, tn), (lambda i, j, kk: (kk, j))
    if b_layer is None:
        b_spec = pl.BlockSpec(b_blk, lambda i, j, kk, *_: b_map(i, j, kk))
    else:
        b_spec = pl.BlockSpec((None,) + b_blk, lambda i, j, kk, *_: (b_layer,) + b_map(i, j, kk))

    if split is None:
        out_shape = jax.ShapeDtypeStruct((m, n), out_dtype)
        o_spec = pl.BlockSpec((tm, tn), lambda i, j, kk, *_: (i, j))
    elif split == "rows":
        out_shape = jax.ShapeDtypeStruct((2, m // 2, n), out_dtype)
        o_spec = pl.BlockSpec(
            (None, tm, tn), lambda i, j, kk, c_ref: (jnp.where(i // (ni // 2) == c_ref[0], 0, 1), i % (ni // 2), j))
    else:
        out_shape = jax.ShapeDtypeStruct((2, m, n // 2), out_dtype)
        o_spec = pl.BlockSpec(
            (None, tm, tn), lambda i, j, kk, c_ref: (jnp.where(j // (nj // 2) == c_ref[0], 0, 1), i, j % (nj // 2)))

    n_job_in = sum(len(jb.ins) for jb in jobs)
    n_job_out = sum(len(jb.out_shapes) for jb in jobs)
    n_acc = 0 if nk == 1 else 1

    def body(*refs):
        if split is not None:
            refs = refs[1:]
        a_ref, b_ref = refs[:2]
        job_ins = refs[2:2 + n_job_in]
        o_ref = refs[2 + n_job_in]
        job_outs = refs[3 + n_job_in:3 + n_job_in + n_job_out]
        scratch_refs = refs[3 + n_job_in + n_job_out:]
        cut = _job_refs(jobs, job_ins, job_outs, scratch_refs[n_acc:])
        i, j, kk = pl.program_id(0), pl.program_id(1), pl.program_id(2)

        if jobs:
            @pl.when((i == 0) & (j == 0) & (kk == 0))
            def _():
                for jb, parts in zip(jobs, cut):
                    jb.start(*parts)

        part = lax.dot_general(a_ref[...].astype(BF16), b_ref[...].astype(BF16), dims, preferred_element_type=F32)
        if nk == 1:
            o_ref[...] = part.astype(o_ref.dtype)
        else:
            acc_ref = scratch_refs[0]

            @pl.when(kk == 0)
            def _():
                acc_ref[...] = part

            @pl.when(kk > 0)
            def _():
                acc_ref[...] += part

            @pl.when(kk == nk - 1)
            def _():
                o_ref[...] = acc_ref[...].astype(o_ref.dtype)

        if jobs:
            @pl.when((i == ni - 1) & (j == nj - 1) & (kk == nk - 1))
            def _():
                for jb, parts in zip(jobs, cut):
                    jb.finish(*parts)

    scratch = ([] if nk == 1 else [pltpu.VMEM((tm, tn), F32)]) + [s for jb in jobs for s in jb.sems]
    params = _params(*(["arbitrary"] * 3 if jobs else ["parallel", "parallel", "arbitrary"]))
    in_specs = [a_spec, b_spec] + [HBM_SPEC] * n_job_in
    out_specs = [o_spec] + [HBM_SPEC] * n_job_out
    out_shapes = [out_shape] + [s for jb in jobs for s in jb.out_shapes]
    operands = [a, b] + [x for jb in jobs for x in jb.ins]
    n_pre = 0 if split is None else 1
    aliases = _job_aliases(jobs, n_pre + 2, 1)
    if split is None:
        res = pl.pallas_call(
            body, name=name, grid=(ni, nj, nk), in_specs=in_specs, out_specs=out_specs, out_shape=out_shapes,
            scratch_shapes=scratch, input_output_aliases=aliases, compiler_params=params,
        )(*operands)
    else:
        grid_spec = pltpu.PrefetchScalarGridSpec(
            num_scalar_prefetch=1, grid=(ni, nj, nk), in_specs=in_specs, out_specs=out_specs, scratch_shapes=scratch)
        res = pl.pallas_call(body, name=name, grid_spec=grid_spec, out_shape=out_shapes, input_output_aliases=aliases,
                             compiler_params=params)(c_idx, *operands)
    if not jobs:
        return res[0]
    return res[0], _job_results(jobs, res[1:])


def _attn_tile(seq):
    return _pick(seq, (256, 128))


def _heads_stacked(ref, b):
    return jnp.concatenate(
        [ref[b * BLOCK:(b + 1) * BLOCK, g * HEAD_DIM:(g + 1) * HEAD_DIM] for g in range(GQA_GROUP)], axis=0)


def _attn_scores(q_ref, k_ref, v_ref, sink_ref, kvh, i, b, tq, seq):
    rows = GQA_GROUP * BLOCK
    q0 = i * tq + b * BLOCK
    k_off = pl.multiple_of(jnp.clip(q0 - BLOCK, 0, seq - BAND), BLOCK)
    kw = k_ref[pl.ds(k_off, BAND), :]
    vw = v_ref[pl.ds(k_off, BAND), :]
    q_pos = q0 + (lax.broadcasted_iota(jnp.int32, (rows, BAND), 0) & (BLOCK - 1))
    k_pos = k_off + lax.broadcasted_iota(jnp.int32, (rows, BAND), 1)
    valid = jnp.abs(k_pos - q_pos) <= WINDOW
    qs = _heads_stacked(q_ref, b)
    s = lax.dot_general(qs, kw, (((1,), (1,)), ((), ())), preferred_element_type=F32) * (HEAD_DIM ** -0.5)
    s = jnp.where(valid, s, NEG_INF)
    sink = jnp.concatenate(
        [jnp.broadcast_to(sink_ref[pl.ds(kvh * GQA_GROUP + g, 1), :][:, :1], (BLOCK, 1)) for g in range(GQA_GROUP)], axis=0)
    m = jnp.maximum(jnp.max(s, axis=-1, keepdims=True), sink)
    p = jnp.exp(s - m)
    p_sink = jnp.exp(sink - m)
    denom = jnp.sum(p, axis=-1, keepdims=True) + p_sink
    return k_off, kw, vw, qs, p / denom, p_sink / denom


def _attn_fwd(name, qr, kr, vb, sink_b, jobs=()):
    seq, dq = qr.shape
    nkv = kr.shape[1] // HEAD_DIM
    tq = _attn_tile(seq)
    gw = GQA_GROUP * HEAD_DIM

    def body(q_ref, k_ref, v_ref, sink_ref, o_ref):
        kvh, i = pl.program_id(0), pl.program_id(1)
        for b in range(tq // BLOCK):
            _, _, vw, _, pn, _ = _attn_scores(q_ref, k_ref, v_ref, sink_ref, kvh, i, b, tq, seq)
            o = jnp.dot(pn.astype(BF16), vw, preferred_element_type=F32).astype(o_ref.dtype)
            for g in range(GQA_GROUP):
                o_ref[b * BLOCK:(b + 1) * BLOCK, g * HEAD_DIM:(g + 1) * HEAD_DIM] = o[g * BLOCK:(g + 1) * BLOCK]

    (att,), job_res = _call(
        name, body, (nkv, seq // tq),
        [
            pl.BlockSpec((tq, gw), lambda h, i: (i, h)),
            pl.BlockSpec((seq, HEAD_DIM), lambda h, i: (0, h)),
            pl.BlockSpec((seq, HEAD_DIM), lambda h, i: (0, h)),
            pl.BlockSpec(sink_b.shape, lambda h, i: (0, 0)),
        ],
        [pl.BlockSpec((tq, gw), lambda h, i: (i, h))], [jax.ShapeDtypeStruct((seq, dq), BF16)], [],
        (qr, kr, vb, sink_b), ("arbitrary", "arbitrary"), jobs)
    return (att, job_res) if jobs else att


def _attn_bwd(name, qr, kr, vb, sink_b, d_att, jobs=()):
    seq, dq = qr.shape
    dkv = kr.shape[1]
    nkv = dkv // HEAD_DIM
    tq = _attn_tile(seq)
    gw = GQA_GROUP * HEAD_DIM
    tn_dims = (((0,), (0,)), ((), ()))

    def body(q_ref, k_ref, v_ref, sink_ref, do_ref, dq_ref, dk_ref, dv_ref, dsink_ref):
        kvh, i = pl.program_id(0), pl.program_id(1)

        @pl.when(i == 0)
        def _():
            dk_ref[...] = jnp.zeros(dk_ref.shape, F32)
            dv_ref[...] = jnp.zeros(dv_ref.shape, F32)

        @pl.when((i == 0) & (kvh == 0))
        def _():
            dsink_ref[...] = jnp.zeros(dsink_ref.shape, F32)

        for b in range(tq // BLOCK):
            k_off, kw, vw, qs, pn, pn_sink = _attn_scores(q_ref, k_ref, v_ref, sink_ref, kvh, i, b, tq, seq)
            dos = _heads_stacked(do_ref, b)
            dp = lax.dot_general(dos, vw, (((1,), (1,)), ((), ())), preferred_element_type=F32)
            delta = jnp.sum(pn * dp, axis=-1, keepdims=True)
            ds = (pn * (dp - delta) * (HEAD_DIM ** -0.5)).astype(BF16)
            dqs = jnp.dot(ds, kw, preferred_element_type=F32)
            sink_term = pn_sink * delta
            for g in range(GQA_GROUP):
                dq_ref[b * BLOCK:(b + 1) * BLOCK, g * HEAD_DIM:(g + 1) * HEAD_DIM] = dqs[g * BLOCK:(g + 1) * BLOCK]
                d_sink = -jnp.sum(sink_term[g * BLOCK:(g + 1) * BLOCK], axis=0, keepdims=True)
                dsink_ref[pl.ds(kvh * GQA_GROUP + g, 1), :] += jnp.broadcast_to(d_sink, (1, LANE))
            dk_ref[pl.ds(k_off, BAND), :] += lax.dot_general(ds, qs, tn_dims, preferred_element_type=F32)
            dv_ref[pl.ds(k_off, BAND), :] += lax.dot_general(pn.astype(BF16), dos, tn_dims, preferred_element_type=F32)

    res, job_res = _call(
        name, body, (nkv, seq // tq),
        [
            pl.BlockSpec((tq, gw), lambda h, i: (i, h)),
            pl.BlockSpec((seq, HEAD_DIM), lambda h, i: (0, h)),
            pl.BlockSpec((seq, HEAD_DIM), lambda h, i: (0, h)),
            pl.BlockSpec(sink_b.shape, lambda h, i: (0, 0)),
            pl.BlockSpec((tq, gw), lambda h, i: (i, h)),
        ],
        [
            pl.BlockSpec((tq, gw), lambda h, i: (i, h)),
            pl.BlockSpec((seq, HEAD_DIM), lambda h, i: (0, h)),
            pl.BlockSpec((seq, HEAD_DIM), lambda h, i: (0, h)),
            pl.BlockSpec(sink_b.shape, lambda h, i: (0, 0)),
        ],
        [
            jax.ShapeDtypeStruct((seq, dq), F32),
            jax.ShapeDtypeStruct((seq, dkv), F32),
            jax.ShapeDtypeStruct((seq, dkv), F32),
            jax.ShapeDtypeStruct(sink_b.shape, F32),
        ],
        [], (qr, kr, vb, sink_b, d_att), ("arbitrary", "arbitrary"), jobs)
    return (res, job_res) if jobs else res


def _halo_specs(tr, width, n_rows):
    per, last = tr // CONV_HALO, n_rows // CONV_HALO - 1
    return [
        pl.BlockSpec((tr, width), lambda i: (i, 0)),
        pl.BlockSpec((CONV_HALO, width), lambda i: (jnp.maximum(i * per - 1, 0), 0)),
        pl.BlockSpec((CONV_HALO, width), lambda i: (jnp.minimum((i + 1) * per, last), 0)),
    ]


def _ext_scratch(tr, width):
    return pltpu.VMEM((SUBLANES, tr + 2 * CONV_HALO, width), F32)


def _fill_ext(ext_ref, main_ref, prev_ref, next_ref, n_steps, tr):
    i = pl.program_id(0)
    ext_ref[0, 0:CONV_HALO, :] = jnp.where(i > 0, prev_ref[...], 0.0)
    ext_ref[0, CONV_HALO:CONV_HALO + tr, :] = main_ref[...]
    ext_ref[0, CONV_HALO + tr:, :] = jnp.where(i < n_steps - 1, next_ref[...], 0.0)
    rows = tr + 2 * CONV_HALO - SUBLANES
    for p in range(1, SUBLANES):
        ext_ref[p, 0:rows, :] = ext_ref[0, p:p + rows, :]


def _tap(ext_ref, r0, t, cols):
    whole, phase = divmod(1 + t, SUBLANES)
    return ext_ref[phase, r0 + whole * SUBLANES:r0 + whole * SUBLANES + CONV_ROWS, cols]


def _conv_taps(ext_ref, w_ref, out_ref, tr, width, flip):
    cw = min(CONV_LANES, width)
    for cc in range(width // cw):
        cols = slice(cc * cw, (cc + 1) * cw)
        for rc in range(tr // CONV_ROWS):
            acc = jnp.zeros((CONV_ROWS, cw), F32)
            for t in range(CONV_WIDTH):
                wt = CONV_WIDTH - 1 - t if flip else t
                acc += _tap(ext_ref, rc * CONV_ROWS, t, cols) * w_ref[wt:wt + 1, cols]
            out_ref[rc * CONV_ROWS:(rc + 1) * CONV_ROWS, cols] = acc


def _ln(v, g, b):
    mu = jnp.mean(v, axis=-1, keepdims=True)
    vc = v - mu
    var = jnp.mean(vc * vc, axis=-1, keepdims=True)
    return vc * lax.rsqrt(var + LN_EPS) * g + b


def _conv_fwd(name, u, w32, ln_g, ln_b, jobs=()):
    n_rows, width = u.shape
    tr = min(ROW_TILE, n_rows)
    n_steps = n_rows // tr

    def body(main_ref, prev_ref, next_ref, w_ref, g_ref, b_ref, u2_ref, cv_ref, ext_ref):
        _fill_ext(ext_ref, main_ref, prev_ref, next_ref, n_steps, tr)
        _conv_taps(ext_ref, w_ref, u2_ref, tr, width, flip=False)
        u3 = _ln(u2_ref[...], g_ref[...], b_ref[...])
        cv_ref[...] = (u3 * _sigmoid(u3)).astype(cv_ref.dtype)

    vec = lambda a: pl.BlockSpec(a.shape, lambda i: (0, 0))
    res, job_res = _call(
        name, body, (n_steps,), _halo_specs(tr, width, n_rows) + [vec(w32), vec(ln_g), vec(ln_b)],
        [pl.BlockSpec((tr, width), lambda i: (i, 0))] * 2,
        [jax.ShapeDtypeStruct((n_rows, width), F32), jax.ShapeDtypeStruct((n_rows, width), BF16)],
        [_ext_scratch(tr, width)], (u, u, u, w32, ln_g, ln_b), ("arbitrary",), jobs)
    return (res, job_res) if jobs else res


def _conv_bwd_a(name, u, u2, d_cv, ln_g, ln_b, jobs=()):
    n_rows, width = u.shape
    tr = min(ROW_TILE // 2, n_rows)
    n_steps = n_rows // tr

    def body(main_ref, prev_ref, next_ref, u2_ref, dcv_ref, g_ref, b_ref, du2_ref, dw_ref, dg_ref, db_ref, ext_ref):
        @pl.when(pl.program_id(0) == 0)
        def _():
            dw_ref[...] = jnp.zeros(dw_ref.shape, F32)
            dg_ref[...] = jnp.zeros(dg_ref.shape, F32)
            db_ref[...] = jnp.zeros(db_ref.shape, F32)

        _fill_ext(ext_ref, main_ref, prev_ref, next_ref, n_steps, tr)

        def swish_ln(v, g, b):
            u3 = _ln(v, g, b)
            return u3 * _sigmoid(u3)

        _, vjp = jax.vjp(swish_ln, u2_ref[...], g_ref[...], b_ref[...])
        du2, dg, db = vjp(dcv_ref[...].astype(F32))
        du2_ref[...] = du2
        dg_ref[...] += dg
        db_ref[...] += db
        for cc in range(width // LANE):
            cols = slice(cc * LANE, (cc + 1) * LANE)
            for t0 in range(0, CONV_WIDTH, 16):
                taps = range(t0, min(t0 + 16, CONV_WIDTH))
                accs = {t: jnp.zeros((SUBLANES, LANE), F32) for t in taps}
                for rc in range(tr // CONV_ROWS):
                    r0 = rc * CONV_ROWS
                    d_blk = du2_ref[r0:r0 + CONV_ROWS, cols]
                    for t in taps:
                        prod = d_blk * _tap(ext_ref, r0, t, cols)
                        for q in range(CONV_ROWS // SUBLANES):
                            accs[t] = accs[t] + prod[q * SUBLANES:(q + 1) * SUBLANES]
                for t in taps:
                    dw_ref[t:t + 1, cols] += jnp.sum(accs[t], axis=0, keepdims=True)

    vec = lambda a: pl.BlockSpec(a.shape, lambda i: (0, 0))
    row = pl.BlockSpec((tr, width), lambda i: (i, 0))
    res, job_res = _call(
        name, body, (n_steps,), _halo_specs(tr, width, n_rows) + [row, row, vec(ln_g), vec(ln_b)],
        [row, pl.BlockSpec((32, width), lambda i: (0, 0)), vec(ln_g), vec(ln_b)],
        [jax.ShapeDtypeStruct((n_rows, width), F32), jax.ShapeDtypeStruct((32, width), F32),
         jax.ShapeDtypeStruct(ln_g.shape, F32), jax.ShapeDtypeStruct(ln_b.shape, F32)],
        [_ext_scratch(tr, width)], (u, u, u, u2, d_cv, ln_g, ln_b), ("arbitrary",), jobs)
    return (res, job_res) if jobs else res


def _conv_bwd_b(name, du2, z, off_a, off_b, w32, jobs=()):
    n_rows, width = du2.shape
    tr = min(ROW_TILE, n_rows)
    n_steps = n_rows // tr
    bw = math.gcd(math.gcd(off_a, off_b), width)
    npc = width // bw

    def body(*refs):
        main_ref, prev_ref, next_ref = refs[:3]
        a_refs, b_refs = refs[3:3 + npc], refs[3 + npc:3 + 2 * npc]
        w_ref, out_ref, ext_ref, du_ref = refs[3 + 2 * npc:]
        _fill_ext(ext_ref, main_ref, prev_ref, next_ref, n_steps, tr)
        _conv_taps(ext_ref, w_ref, du_ref, tr, width, flip=True)
        for p in range(npc):
            cols = slice(p * bw, (p + 1) * bw)
            du = du_ref[:, cols]
            sg = _sigmoid(b_refs[p][...].astype(F32))
            out_ref[:, p * bw:(p + 1) * bw] = (du * sg).astype(out_ref.dtype)
            out_ref[:, width + p * bw:width + (p + 1) * bw] = (
                du * a_refs[p][...].astype(F32) * sg * (1.0 - sg)).astype(out_ref.dtype)

    def piece(off, p):
        return pl.BlockSpec((tr, bw), functools.partial(lambda i, blk: (i, blk), blk=off // bw + p))

    in_specs = _halo_specs(tr, width, n_rows)
    in_specs += [piece(off_a, p) for p in range(npc)] + [piece(off_b, p) for p in range(npc)]
    in_specs.append(pl.BlockSpec(w32.shape, lambda i: (0, 0)))
    (d_glu,), job_res = _call(
        name, body, (n_steps,), in_specs, [pl.BlockSpec((tr, 2 * width), lambda i: (i, 0))],
        [jax.ShapeDtypeStruct((n_rows, 2 * width), BF16)], [_ext_scratch(tr, width), pltpu.VMEM((tr, width), F32)],
        (du2, du2, du2, *([z] * (2 * npc)), w32), ("arbitrary",), jobs)
    return (d_glu, job_res) if jobs else d_glu


def _place():
    return lax.axis_index("x"), lax.axis_index("y"), lax.axis_index("c")


def _flip(v, m):
    return 1 - v if m else v


def _gather_small(name, v, masks):
    varies = [any(m[a] for m in masks) for a in range(3)]
    n = len(masks) + 1

    def slot(pos):
        idx = 0
        for a in range(3):
            if varies[a]:
                idx = idx * 2 + pos[a]
        return idx

    def body(v_ref, o_ref, send_sems, recv_sems, local_sem):
        me = _place()
        mine = pltpu.make_async_copy(v_ref, o_ref.at[slot(me)], local_sem)
        mine.start()
        peers = [tuple(_flip(me[a], m[a]) for a in range(3)) for m in masks]
        sends = [pltpu.make_async_remote_copy(v_ref, o_ref.at[slot(me)], send_sems.at[k], recv_sems.at[k],
                                              device_id=peer, device_id_type=MESH) for k, peer in enumerate(peers)]
        for cp in sends:
            cp.start()
        for k, peer in enumerate(peers):
            pltpu.make_async_remote_copy(v_ref, o_ref.at[slot(peer)], send_sems.at[k], recv_sems.at[k],
                                         device_id=peer, device_id_type=MESH).wait_recv()
        for cp in sends:
            cp.wait_send()
        mine.wait()

    return pl.pallas_call(
        body, name=name, in_specs=[HBM_SPEC], out_specs=HBM_SPEC,
        out_shape=jax.ShapeDtypeStruct((n,) + v.shape, v.dtype),
        scratch_shapes=[pltpu.SemaphoreType.DMA((n - 1,)), pltpu.SemaphoreType.DMA((n - 1,)), pltpu.SemaphoreType.DMA(())],
    )(v)


ALL_DEVICES = [(mx, my, mc) for mx in (0, 1) for my in (0, 1) for mc in (0, 1)][1:]
SAME_CORE_CHIPS = [(1, 0, 0), (0, 1, 0), (1, 1, 0)]


def _chips(x, y):
    return [(1 - x, y), (x, 1 - y), (1 - x, 1 - y)]


def _cast_into(name, w, layer, kind, chip_idx):
    _, r, cc = w.shape
    tr = _rows_within(r, cc * 6)
    steps = r // tr
    if kind == "col":
        shape, o_spec = (r, 4 * cc), pl.BlockSpec((tr, cc), lambda i, s_ref: (i, s_ref[0]))
    else:
        shape, o_spec = (4 * r, cc), pl.BlockSpec((tr, cc), lambda i, s_ref: (s_ref[0] * steps + i, 0))

    def body(s_ref, w_ref, o_ref):
        o_ref[...] = w_ref[...].astype(o_ref.dtype)

    grid_spec = pltpu.PrefetchScalarGridSpec(
        num_scalar_prefetch=1, grid=(steps,),
        in_specs=[pl.BlockSpec((None, tr, cc), lambda i, s_ref: (layer, i, 0))], out_specs=o_spec)
    return pl.pallas_call(body, name=name, grid_spec=grid_spec, out_shape=jax.ShapeDtypeStruct(shape, BF16),
                          compiler_params=_params("arbitrary"))(chip_idx, w)


class _Job:
    def __init__(self, ins, out_shapes, aliases, sems, start, finish):
        self.ins, self.out_shapes, self.aliases, self.sems = list(ins), list(out_shapes), dict(aliases), list(sems)
        self.start, self.finish = start, finish


def _job_refs(jobs, in_refs, out_refs, sem_refs):
    cut, i, o, s = [], 0, 0, 0
    for jb in jobs:
        cut.append((in_refs[i:i + len(jb.ins)], out_refs[o:o + len(jb.out_shapes)], sem_refs[s:s + len(jb.sems)]))
        i, o, s = i + len(jb.ins), o + len(jb.out_shapes), s + len(jb.sems)
    return cut


def _job_aliases(jobs, first_in, first_out):
    aliases, i, o = {}, first_in, first_out
    for jb in jobs:
        for a, b in jb.aliases.items():
            aliases[i + a] = o + b
        i, o = i + len(jb.ins), o + len(jb.out_shapes)
    return aliases


def _run_jobs(name, jobs):
    n_in = sum(len(jb.ins) for jb in jobs)
    n_out = sum(len(jb.out_shapes) for jb in jobs)

    def body(*refs):
        cut = _job_refs(jobs, refs[:n_in], refs[n_in:n_in + n_out], refs[n_in + n_out:])
        for jb, parts in zip(jobs, cut):
            jb.start(*parts)
        for jb, parts in zip(jobs, cut):
            jb.finish(*parts)

    res = pl.pallas_call(
        body, name=name, in_specs=[HBM_SPEC] * n_in, out_specs=[HBM_SPEC] * n_out,
        out_shape=[s for jb in jobs for s in jb.out_shapes], input_output_aliases=_job_aliases(jobs, 0, 0),
        scratch_shapes=[s for jb in jobs for s in jb.sems],
    )(*[a for jb in jobs for a in jb.ins])
    return _job_results(jobs, res)


def _job_results(jobs, flat):
    out, o = [], 0
    for jb in jobs:
        out.append(list(flat[o:o + len(jb.out_shapes)]))
        o += len(jb.out_shapes)
    return out


def _call(name, body, grid, in_specs, out_specs, out_shape, scratch, operands, sem, jobs=()):
    n_in, n_out, n_scr = len(in_specs), len(out_specs), len(scratch)
    n_job_in = sum(len(jb.ins) for jb in jobs)
    n_job_out = sum(len(jb.out_shapes) for jb in jobs)

    def full_body(*refs):
        ins, job_ins = refs[:n_in], refs[n_in:n_in + n_job_in]
        rest = refs[n_in + n_job_in:]
        outs, job_outs = rest[:n_out], rest[n_out:n_out + n_job_out]
        scr, sems = rest[n_out + n_job_out:n_out + n_job_out + n_scr], rest[n_out + n_job_out + n_scr:]
        cut = _job_refs(jobs, job_ins, job_outs, sems)
        ids = [pl.program_id(a) for a in range(len(grid))]
        if jobs:
            @pl.when(functools.reduce(lambda p, q: p & q, [i == 0 for i in ids]))
            def _():
                for jb, parts in zip(jobs, cut):
                    jb.start(*parts)
        body(*ins, *outs, *scr)
        if jobs:
            @pl.when(functools.reduce(lambda p, q: p & q, [i == g - 1 for i, g in zip(ids, grid)]))
            def _():
                for jb, parts in zip(jobs, cut):
                    jb.finish(*parts)

    res = pl.pallas_call(
        full_body, name=name, grid=grid, in_specs=list(in_specs) + [HBM_SPEC] * n_job_in,
        out_specs=list(out_specs) + [HBM_SPEC] * n_job_out,
        out_shape=list(out_shape) + [s for jb in jobs for s in jb.out_shapes],
        scratch_shapes=list(scratch) + [s for jb in jobs for s in jb.sems],
        input_output_aliases=_job_aliases(jobs, n_in, n_out), compiler_params=_params(*sem),
    )(*operands, *[a for jb in jobs for a in jb.ins])
    return list(res[:n_out]), _job_results(jobs, res[n_out:])


def _job_gather(fulls, shapes, kinds):
    n = len(fulls)
    for r, _ in shapes:
        assert r % 32 == 0

    def window(o_ref, j, s, h):
        r, cc = shapes[j]
        hr = r // 2
        if kinds[j] == "col":
            return o_ref.at[pl.ds(pl.multiple_of(h * hr, 16), hr), pl.ds(pl.multiple_of(s * cc, LANE), cc)]
        return o_ref.at[pl.ds(pl.multiple_of(s * r + h * hr, 16), hr), :]

    def first_copies(outs, sems):
        x, y, c = _place()
        cps = []
        for j in range(n):
            mine = window(outs[j], j, 2 * x + y, c)
            for k, chip in enumerate(_chips(x, y)):
                cps.append(pltpu.make_async_remote_copy(mine, mine, sems[0].at[3 * j + k], sems[1].at[3 * j + k],
                                                        device_id=(*chip, c), device_id_type=MESH))
        return cps

    def start(ins, outs, sems):
        for cp in first_copies(outs, sems):
            cp.start()

    def finish(ins, outs, sems):
        x, y, c = _place()
        chips = _chips(x, y)
        sibling = (x, y, 1 - c)
        passed = []
        for j in range(n):
            for k, chip in enumerate(chips):
                win = window(outs[j], j, 2 * chip[0] + chip[1], c)
                pltpu.make_async_remote_copy(win, win, sems[0].at[3 * j + k], sems[1].at[3 * j + k],
                                             device_id=(*chip, c), device_id_type=MESH).wait_recv()
                cp = pltpu.make_async_remote_copy(win, win, sems[2].at[3 * j + k], sems[3].at[3 * j + k],
                                                  device_id=sibling, device_id_type=MESH)
                cp.start()
                passed.append(cp)
        for j in range(n):
            for k, chip in enumerate(chips):
                win = window(outs[j], j, 2 * chip[0] + chip[1], 1 - c)
                pltpu.make_async_remote_copy(win, win, sems[2].at[3 * j + k], sems[3].at[3 * j + k],
                                             device_id=sibling, device_id_type=MESH).wait_recv()
        for cp in first_copies(outs, sems) + passed:
            cp.wait_send()

    return _Job(fulls, [jax.ShapeDtypeStruct(f.shape, f.dtype) for f in fulls], {j: j for j in range(n)},
                [pltpu.SemaphoreType.DMA((3 * n,)) for _ in range(4)], start, finish)


def _job_pair(grads):
    n = len(grads)

    def copies(ins, outs, sems):
        x, y, c = _place()
        return [pltpu.make_async_remote_copy(ins[j].at[1], outs[j], sems[0].at[j], sems[1].at[j],
                                             device_id=(x, y, 1 - c), device_id_type=MESH) for j in range(n)]

    def start(ins, outs, sems):
        for cp in copies(ins, outs, sems):
            cp.start()

    def finish(ins, outs, sems):
        for cp in copies(ins, outs, sems):
            cp.wait()

    return _Job(grads, [jax.ShapeDtypeStruct(g.shape[1:], g.dtype) for g in grads], {},
                [pltpu.SemaphoreType.DMA((n,)), pltpu.SemaphoreType.DMA((n,))], start, finish)


def _job_chips(halves, kinds):
    n = len(halves)
    shapes = [(h.shape[0], h.shape[1] // 4) if kinds[j] == "col" else (h.shape[0] // 4, h.shape[1])
              for j, h in enumerate(halves)]

    def part(ref, j, s):
        r, cc = shapes[j]
        if kinds[j] == "col":
            return ref.at[:, pl.ds(pl.multiple_of(s * cc, LANE), cc)]
        return ref.at[pl.ds(pl.multiple_of(s * r, 16), r), :]

    def copies(ins, outs, sems):
        x, y, c = _place()
        s_me = 2 * x + y
        local = [pltpu.make_async_copy(part(ins[j], j, s_me), outs[j].at[s_me], sems[2].at[j]) for j in range(n)]
        sends, recvs = [], []
        for j in range(n):
            for k, chip in enumerate(_chips(x, y)):
                s_peer = 2 * chip[0] + chip[1]
                sends.append(pltpu.make_async_remote_copy(part(ins[j], j, s_peer), outs[j].at[s_me],
                                                          sems[0].at[3 * j + k], sems[1].at[3 * j + k],
                                                          device_id=(*chip, c), device_id_type=MESH))
                dst = outs[j].at[s_peer]
                recvs.append(pltpu.make_async_remote_copy(dst, dst, sems[0].at[3 * j + k], sems[1].at[3 * j + k],
                                                          device_id=(*chip, c), device_id_type=MESH))
        return local, sends, recvs

    def start(ins, outs, sems):
        local, sends, _ = copies(ins, outs, sems)
        for cp in local + sends:
            cp.start()

    def finish(ins, outs, sems):
        local, sends, recvs = copies(ins, outs, sems)
        for cp in recvs:
            cp.wait_recv()
        for cp in sends:
            cp.wait_send()
        for cp in local:
            cp.wait()

    return _Job(halves, [jax.ShapeDtypeStruct((4,) + shapes[j], halves[j].dtype) for j in range(n)], {},
                [pltpu.SemaphoreType.DMA((3 * n,)), pltpu.SemaphoreType.DMA((3 * n,)), pltpu.SemaphoreType.DMA((n,))],
                start, finish)


def _job_join(pairs):
    n = len(pairs)

    def copies(outs, sems):
        x, y, c = _place()
        sends, recvs = [], []
        for j in range(n):
            sends.append(pltpu.make_async_remote_copy(outs[j].at[c], outs[j].at[c], sems[0].at[j], sems[1].at[j],
                                                      device_id=(x, y, 1 - c), device_id_type=MESH))
            theirs = outs[j].at[1 - c]
            recvs.append(pltpu.make_async_remote_copy(theirs, theirs, sems[0].at[j], sems[1].at[j],
                                                      device_id=(x, y, 1 - c), device_id_type=MESH))
        return sends, recvs

    def start(ins, outs, sems):
        for cp in copies(outs, sems)[0]:
            cp.start()

    def finish(ins, outs, sems):
        sends, recvs = copies(outs, sems)
        for cp in recvs:
            cp.wait_recv()
        for cp in sends:
            cp.wait_send()

    return _Job(pairs, [jax.ShapeDtypeStruct(p.shape, p.dtype) for p in pairs], {j: j for j in range(n)},
                [pltpu.SemaphoreType.DMA((n,)), pltpu.SemaphoreType.DMA((n,))], start, finish)


def _pair_sum(name, mine_other, got):
    _, r, cc = mine_other.shape
    tr = _rows_within(r, 3 * cc * mine_other.dtype.itemsize, (256, 128, 64, 32, 16))

    def body(a_ref, b_ref, o_ref):
        o_ref[...] = (a_ref[...].astype(F32) + b_ref[...].astype(F32)).astype(o_ref.dtype)

    return pl.pallas_call(
        body, name=name, grid=(r // tr,),
        in_specs=[pl.BlockSpec((None, tr, cc), lambda i: (0, i, 0)), pl.BlockSpec((tr, cc), lambda i: (i, 0))],
        out_specs=pl.BlockSpec((tr, cc), lambda i: (i, 0)),
        out_shape=jax.ShapeDtypeStruct((r, cc), mine_other.dtype), compiler_params=_params("parallel"),
    )(mine_other, got)


def _sum_slots(name, parts, into_slot=None):
    n, r, cc = parts.shape
    tr = _rows_within(r, cc * (n * parts.dtype.itemsize + 4), (256, 128, 64, 32, 16, 8))

    def body(*refs):
        p_ref, o_ref = refs[-2:]
        acc = p_ref[0].astype(F32)
        for s in range(1, n):
            acc = acc + p_ref[s].astype(F32)
        o_ref[...] = acc

    in_spec = pl.BlockSpec((n, tr, cc), lambda i, *_: (0, i, 0))
    if into_slot is None:
        return pl.pallas_call(
            body, name=name, grid=(r // tr,), in_specs=[in_spec], out_specs=pl.BlockSpec((tr, cc), lambda i: (i, 0)),
            out_shape=jax.ShapeDtypeStruct((r, cc), F32), compiler_params=_params("parallel"),
        )(parts)
    grid_spec = pltpu.PrefetchScalarGridSpec(
        num_scalar_prefetch=1, grid=(r // tr,), in_specs=[in_spec],
        out_specs=pl.BlockSpec((None, tr, cc), lambda i, c_ref: (c_ref[0], i, 0)))
    return pl.pallas_call(body, name=name, grid_spec=grid_spec, out_shape=jax.ShapeDtypeStruct((2, r, cc), F32),
                          compiler_params=_params("arbitrary"))(into_slot, parts)


def _adamw_math(w, g, m, v):
    m = ADAM_B1 * m + (1.0 - ADAM_B1) * g
    v = ADAM_B2 * v + (1.0 - ADAM_B2) * jnp.square(g)
    m_hat = m / (1.0 - ADAM_B1 ** ADAM_STEP)
    v_hat = v / (1.0 - ADAM_B2 ** ADAM_STEP)
    delta = -ADAM_LR * (m_hat / (jnp.sqrt(v_hat) + ADAM_EPS) + ADAM_WD * w)
    return delta, m, v


def _adamw_layer(name, layer, g_pair, kind, w, m, v, prev):
    n_layers, r, cc = w.shape
    if prev is None:
        prev = tuple(lax.empty(w.shape, F32) for _ in range(4))
    if kind == "col":
        g = g_pair.reshape(r, cc)
        tr = _rows_within(r, 8 * cc * 4)
        grid = (r // tr,)
        g_spec = pl.BlockSpec((tr, cc), lambda i: (i, 0))
        blk = pl.BlockSpec((None, tr, cc), lambda i: (layer, i, 0))
    else:
        g = g_pair
        tr = _rows_within(r, 4 * cc * 4)
        grid = (r // tr, 2)
        g_spec = pl.BlockSpec((None, tr, cc // 2), lambda i, h: (h, i, 0))
        blk = pl.BlockSpec((None, tr, cc // 2), lambda i, h: (layer, i, h))

    def body(g_ref, w_ref, m_ref, v_ref, *rest):
        og_ref, od_ref, om_ref, ov_ref = rest[4:]
        gv = g_ref[...]
        delta, m2, v2 = _adamw_math(w_ref[...], gv, m_ref[...], v_ref[...])
        og_ref[...] = gv
        od_ref[...] = delta
        om_ref[...] = m2
        ov_ref[...] = v2

    return pl.pallas_call(
        body, name=name, grid=grid,
        in_specs=[g_spec, blk, blk, blk] + [HBM_SPEC] * 4,
        out_specs=[blk] * 4, out_shape=[jax.ShapeDtypeStruct(w.shape, F32)] * 4,
        input_output_aliases={4: 0, 5: 1, 6: 2, 7: 3}, compiler_params=_params(*(["parallel"] * len(grid))),
    )(g, w, m, v, *prev)


def _adamw_small(name, g, w, m, v):
    def body(g_ref, w_ref, m_ref, v_ref, od_ref, om_ref, ov_ref):
        delta, m2, v2 = _adamw_math(w_ref[...], g_ref[...], m_ref[...], v_ref[...])
        od_ref[...] = delta
        om_ref[...] = m2
        ov_ref[...] = v2

    return pl.pallas_call(body, name=name, out_shape=[jax.ShapeDtypeStruct(w.shape, F32)] * 3)(g, w, m, v)


def _adamw_ada(name, c16, dmod16, w, m, v, jobs=()):
    n_layers, d, cols = w.shape
    tr = _rows_within(d, 7 * cols * 4, (256, 128))
    blk = pl.BlockSpec((None, tr, cols), lambda l, i: (l, i, 0))

    def body(c_ref, dm_ref, w_ref, m_ref, v_ref, og_ref, od_ref, om_ref, ov_ref):
        gv = lax.dot_general(c_ref[...], dm_ref[...], (((0,), (0,)), ((), ())), preferred_element_type=F32)
        delta, m2, v2 = _adamw_math(w_ref[...], gv, m_ref[...], v_ref[...])
        og_ref[...] = gv
        od_ref[...] = delta
        om_ref[...] = m2
        ov_ref[...] = v2

    return _call(
        name, body, (n_layers, d // tr),
        [pl.BlockSpec((16, tr), lambda l, i: (0, i)), pl.BlockSpec((None, 16, cols), lambda l, i: (l, 0, 0)),
         blk, blk, blk],
        [blk] * 4, [jax.ShapeDtypeStruct(w.shape, F32)] * 4, [], (c16, dmod16, w, m, v),
        ("arbitrary", "arbitrary"), jobs)


def kernel(x, c, w_ada, b_ada, w_in, sink, w_dw, conv_ln_g, conv_ln_b, w_oa, w_ob, w_out, ln1_g, ln1_b, w_gu, w_down, ln2_g, ln2_b, loss_target, m_w_ada, m_b_ada, m_w_in, m_sink, m_w_dw, m_conv_ln_g, m_conv_ln_b, m_w_oa, m_w_ob, m_w_out, m_ln1_g, m_ln1_b, m_w_gu, m_w_down, m_ln2_g, m_ln2_b, v_w_ada, v_b_ada, v_w_in, v_sink, v_w_dw, v_conv_ln_g, v_conv_ln_b, v_w_oa, v_w_ob, v_w_out, v_ln1_g, v_ln1_b, v_w_gu, v_w_down, v_ln2_g, v_ln2_b):
    seq, d = x.shape[1], x.shape[2]
    n_layers = w_in.shape[0]
    d_in = 4 * w_in.shape[2]
    d_ff = 4 * w_down.shape[1]
    hq = d // HEAD_DIM
    dkv = (hq // GQA_GROUP) * HEAD_DIM
    off_k, off_v, off_ga, off_gb = d, d + dkv, d + 2 * dkv, 2 * d + 2 * dkv
    off_gta, off_gtb = 3 * d + 2 * dkv, 4 * d + 2 * dkv
    assert d_in == 5 * d + 2 * dkv and seq % ROW_TILE == 0 and seq >= BAND
    alpha = (2.0 * n_layers) ** 0.25

    xi, yi, ci = _place()
    chip = 2 * xi + yi
    batch = 4 * xi + 2 * yi + ci
    c_idx = jnp.reshape(ci, (1,)).astype(jnp.int32)
    x2 = x[0]
    target = loss_target[0]

    c_act = jax.nn.silu(c)
    c_all = _gather_small("gather_c", c_act, ALL_DEVICES).reshape(8, d)
    c16 = jnp.concatenate([c_all, jnp.zeros((8, d), F32)], axis=0).astype(BF16)
    mod_cols = [_mm(f"mod_{l}", c16, w_ada, "nn", F32, b_layer=l) for l in range(n_layers)]
    mod_all = _gather_small("gather_mod", jnp.stack(mod_cols), SAME_CORE_CHIPS)
    mod = lax.dynamic_index_in_dim(mod_all, batch, axis=2, keepdims=False)
    mod = jnp.transpose(mod, (1, 0, 2)).reshape(n_layers, N_MOD * d) + b_ada
    mod = mod.reshape(n_layers, N_MOD, 1, d)
    sh_a, sc_a, gt_a, sh_f, sc_f, gt_f = (mod[:, j] for j in range(N_MOD))

    pos = jnp.arange(seq, dtype=F32)
    inv_freq = ROPE_THETA ** (-jnp.arange(0, ROPE_DIM, 2, dtype=F32) / ROPE_DIM)
    ang = pos[:, None] * inv_freq[None, :]
    cos, sin = jnp.cos(ang), jnp.sin(ang)
    half = ROPE_DIM // 2
    rest = HEAD_DIM - ROPE_DIM
    t_cs = jnp.concatenate([cos, cos, jnp.ones((seq, rest), F32)], axis=1)
    t_up = jnp.concatenate([-sin, jnp.zeros((seq, rest + half), F32)], axis=1)
    t_dn = jnp.concatenate([jnp.zeros((seq, half), F32), sin, jnp.zeros((seq, rest), F32)], axis=1)

    def rope(t, cs, up, dn):
        w = t.shape[1]
        reps = (1, w // HEAD_DIM)
        return (t * jnp.tile(cs, reps) + pltpu.roll(t, w - half, 1) * jnp.tile(up, reps)
                + pltpu.roll(t, half, 1) * jnp.tile(dn, reps))

    def rope_t(dt, cs, up, dn):
        w = dt.shape[1]
        reps = (1, w // HEAD_DIM)
        return (dt * jnp.tile(cs, reps) + pltpu.roll(dt * jnp.tile(up, reps), half, 1)
                + pltpu.roll(dt * jnp.tile(dn, reps), w - half, 1))

    tables = [(t_cs, 0, HEAD_DIM), (t_up, 0, HEAD_DIM), (t_dn, 0, HEAD_DIM)]

    kinds = ("col", "col", "row", "row", "row", "row")
    big_weights = (w_in, w_gu, w_oa, w_ob, w_out, w_down)
    chip_idx = jnp.reshape(chip, (1,)).astype(jnp.int32)
    shard_shapes = [w.shape[1:] for w in big_weights]
    fulls = [[_cast_into(f"cast_w_{l}_{j}", w, l, kinds[j], chip_idx) for j, w in enumerate(big_weights)]
             for l in range(n_layers)]

    def gather_job(l, which):
        return _job_gather([fulls[l][j] for j in which], [shard_shapes[j] for j in which], [kinds[j] for j in which])

    ride_in_proj, ride_ffn_up, ride_attn_out, ride_conv_out, ride_mix_out, ride_ffn_down = (0,), (1,), (2,), (3,), (4,), (5,)
    ahead, ride_attn_0, ride_conv_0 = (0, 2), (3, 4, 5), (1,)
    gathered = [[None] * 6 for _ in range(n_layers)]

    def keep(layer, which, arrays):
        for j, arr in zip(which, arrays):
            gathered[layer][j] = arr

    keep(0, ahead, _run_jobs("gather_w_0", [gather_job(0, ahead)])[0])
    w_dw_all = _gather_small("gather_dw", w_dw, SAME_CORE_CHIPS)
    w_dw_full = jnp.transpose(w_dw_all, (1, 2, 0, 3)).reshape(n_layers, CONV_WIDTH, d)
    w_dw32 = jnp.pad(w_dw_full, ((0, 0), (0, 32 - CONV_WIDTH), (0, 0)))
    sink_b = jnp.broadcast_to(sink[:, :, None], (n_layers, hq, LANE))

    def vec(a, l):
        return a[l][None, :]

    def res_ln(xprev, y, gt, g, b, scn, shn):
        xn = _ln(alpha * xprev + (1.0 + gt) * y, g, b)
        return xn, xn * (1.0 + scn) + shn

    def merge(ya, yb, ga, gb):
        return _sigmoid(ga) * ya + _sigmoid(gb) * yb

    def swiglu(gate, up):
        return gate * _sigmoid(gate) * up

    h = _rowwise("modulate_in", lambda xv, sc, sh: xv * (1.0 + sc) + sh, seq, [(x2, 0, d)], [sc_a[0], sh_a[0]],
                 [(d, BF16)])
    xprev = x2
    saved = []
    for l in range(n_layers):
        nxt = l + 1 < n_layers

        def mm_carrying(name, a_, j_weight, which, out_dtype=F32):
            if not nxt:
                return _mm(name, a_, gathered[l][j_weight], "nn", out_dtype)
            res, (got,) = _mm(name, a_, gathered[l][j_weight], "nn", out_dtype, jobs=[gather_job(l + 1, which)])
            keep(l + 1, which, got)
            return res

        z = mm_carrying(f"in_proj_{l}", h, 0, ride_in_proj, BF16)
        qr, kr, vb = _rowwise(
            f"qkv_prep_{l}", lambda q, k, v, cs, up, dn: (rope(q, cs, up, dn), rope(k, cs, up, dn), v), seq,
            [(z, 0, d), (z, off_k, dkv), (z, off_v, dkv)] + tables, [], [(d, BF16), (dkv, BF16), (dkv, BF16)])
        if l == 0:
            att, (got,) = _attn_fwd(f"attn_{l}", qr, kr, vb, sink_b[l], jobs=[gather_job(0, ride_attn_0)])
            keep(0, ride_attn_0, got)
        else:
            att = _attn_fwd(f"attn_{l}", qr, kr, vb, sink_b[l])
        y_a = mm_carrying(f"attn_out_{l}", att, 2, ride_attn_out, BF16)
        u = _rowwise(f"glu_{l}", lambda a, b: a * _sigmoid(b), seq, [(z, off_ga, d), (z, off_gb, d)], [], [(d, F32)])
        if l == 0:
            (u2, cv), (got,) = _conv_fwd(f"conv_{l}", u, w_dw32[l], vec(conv_ln_g, l), vec(conv_ln_b, l),
                                         jobs=[gather_job(0, ride_conv_0)])
            keep(0, ride_conv_0, got)
        else:
            u2, cv = _conv_fwd(f"conv_{l}", u, w_dw32[l], vec(conv_ln_g, l), vec(conv_ln_b, l))
        y_b = mm_carrying(f"conv_out_{l}", cv, 3, ride_conv_out, BF16)
        mg = _rowwise(f"merge_{l}", merge, seq, [(y_a, 0, d), (y_b, 0, d), (z, off_gta, d), (z, off_gtb, d)], [],
                      [(d, BF16)])
        o = mm_carrying(f"mix_out_{l}", mg, 4, ride_mix_out)
        x1, h2 = _rowwise(f"res_ln1_{l}", res_ln, seq, [(xprev, 0, d), (o, 0, d)],
                          [gt_a[l], vec(ln1_g, l), vec(ln1_b, l), sc_f[l], sh_f[l]], [(d, F32), (d, BF16)])
        gu = mm_carrying(f"ffn_up_{l}", h2, 1, ride_ffn_up, BF16)
        f = _rowwise(f"swiglu_{l}", swiglu, seq, [(gu, 0, d_ff), (gu, d_ff, d_ff)], [], [(d_ff, BF16)])
        ffn = mm_carrying(f"ffn_down_{l}", f, 5, ride_ffn_down)
        saved.append(dict(xprev=xprev, h=h, z=z, qr=qr, kr=kr, vb=vb, att=att, u=u, u2=u2, cv=cv, y_a=y_a, y_b=y_b,
                          mg=mg, o=o, x1=x1, h2=h2, gu=gu, f=f, ffn=ffn))
        if l + 1 < n_layers:
            xprev, h = _rowwise(f"res_ln2_{l}", res_ln, seq, [(x1, 0, d), (ffn, 0, d)],
                                [gt_f[l], vec(ln2_g, l), vec(ln2_b, l), sc_a[l + 1], sh_a[l + 1]], [(d, F32), (d, BF16)])

    def res_ln_bwd(xp, y, dxn, dh, gt, g, b, scn, shn):
        _, vjp = jax.vjp(res_ln, xp, y, gt, g, b, scn, shn)
        return vjp((dxn, dh))

    def last_ln_bwd(xp, y, tgt, gt, g, b):
        def head(xp_, y_, gt_, g_, b_):
            return _ln(alpha * xp_ + (1.0 + gt_) * y_, g_, b_)
        out, vjp = jax.vjp(head, xp, y, gt, g, b)
        err = out - tgt
        loss = 0.5 * jnp.sum(jnp.sum(err * err, axis=-1, keepdims=True) / d, axis=0, keepdims=True)
        return vjp(err / d) + (jnp.broadcast_to(loss, (1, LANE)),)

    def merge_bwd(dmg, ya, yb, ga, gb):
        _, vjp = jax.vjp(merge, ya, yb, ga, gb)
        dya, dyb, dga, dgb = vjp(dmg)
        return dya, dyb, jnp.concatenate([dga, dgb], axis=1)

    def swiglu_bwd(df, gate, up):
        _, vjp = jax.vjp(swiglu, gate, up)
        return jnp.concatenate(vjp(df), axis=1)

    vec_d = ((1, d), F32)
    small = [None] * n_layers
    big = None
    loss_part = None
    dxn = dh = None
    pending = None
    stacks = ((w_in, m_w_in, v_w_in), (w_gu, m_w_gu, v_w_gu), (w_oa, m_w_oa, v_w_oa), (w_ob, m_w_ob, v_w_ob),
              (w_out, m_w_out, v_w_out), (w_down, m_w_down, v_w_down))

    def adamw_all(layer, full, prev):
        return [_adamw_layer(f"adamw_{layer}_{j}", layer, full[j], kinds[j], *stacks[j], None if prev is None else prev[j])
                for j in range(6)]

    for l in reversed(range(n_layers)):
        sv = saved[l]
        wi, wg, woa, wob, wout, wdn = gathered[l]
        ln2 = [gt_f[l], vec(ln2_g, l), vec(ln2_b, l)]
        if l + 1 == n_layers:
            dx1, dffn, d_gtf, d_g2, d_b2, loss_part = _rowwise(
                "last_ln_bwd", last_ln_bwd, seq, [(sv["x1"], 0, d), (sv["ffn"], 0, d), (target, 0, d)], ln2,
                [(d, F32), (d, BF16)], [vec_d, vec_d, vec_d, ((1, LANE), F32)])
            d_sca_next = d_sha_next = None
        else:
            dx1, dffn, d_gtf, d_g2, d_b2, d_sca_next, d_sha_next = _rowwise(
                f"res_ln2_bwd_{l}", res_ln_bwd, seq, [(sv["x1"], 0, d), (sv["ffn"], 0, d), (dxn, 0, d), (dh, 0, d)],
                ln2 + [sc_a[l + 1], sh_a[l + 1]], [(d, F32), (d, BF16)], [vec_d] * 5)
            small[l + 1]["sc_a"], small[l + 1]["sh_a"] = d_sca_next, d_sha_next
        def riding(make_job, *job_args):
            return [] if pending is None else [make_job(*job_args)]

        def unpack(res):
            return res if pending is not None else (res, [None])

        df, (got,) = unpack(_mm(f"ffn_down_dx_{l}", dffn, wdn, "nt", BF16, jobs=riding(_job_pair, pending)))
        halves = None if pending is None else [
            _pair_sum(f"pair_sum_{l + 1}_{j}", pending[j], got[j]) for j in range(6)]
        g_down = _mm(f"ffn_down_dw_{l}", sv["f"], dffn, "tn", BF16, split="cols", c_idx=c_idx)
        dgu = _rowwise(f"swiglu_bwd_{l}", swiglu_bwd, seq, [(df, 0, d_ff), (sv["gu"], 0, d_ff), (sv["gu"], d_ff, d_ff)],
                       [], [(2 * d_ff, BF16)])
        dh2, (parts_in,) = unpack(_mm(f"ffn_up_dx_{l}", dgu, wg, "nt", F32,
                                      jobs=riding(lambda: _job_chips(halves[:1], kinds[:1]))))
        g_gu, (parts_gu,) = unpack(_mm(f"ffn_up_dw_{l}", sv["h2"], dgu, "tn", BF16, split="rows", c_idx=c_idx,
                                       jobs=riding(lambda: _job_chips(halves[1:2], kinds[1:2]))))
        dxp, d_o, d_gta, d_g1, d_b1, d_scf, d_shf = _rowwise(
            f"res_ln1_bwd_{l}", res_ln_bwd, seq, [(sv["xprev"], 0, d), (sv["o"], 0, d), (dx1, 0, d), (dh2, 0, d)],
            [gt_a[l], vec(ln1_g, l), vec(ln1_b, l), sc_f[l], sh_f[l]], [(d, F32), (d, BF16)], [vec_d] * 5)
        dmg = _mm(f"mix_out_dx_{l}", d_o, wout, "nt", BF16)
        g_out = _mm(f"mix_out_dw_{l}", sv["mg"], d_o, "tn", BF16, split="cols", c_idx=c_idx)
        z = sv["z"]
        dya, dyb, d_gates = _rowwise(
            f"merge_bwd_{l}", merge_bwd, seq,
            [(dmg, 0, d), (sv["y_a"], 0, d), (sv["y_b"], 0, d), (z, off_gta, d), (z, off_gtb, d)], [],
            [(d, BF16), (d, BF16), (2 * d, BF16)])
        d_att = _mm(f"attn_out_dx_{l}", dya, woa, "nt", BF16)
        g_oa = _mm(f"attn_out_dw_{l}", sv["att"], dya, "tn", BF16, split="cols", c_idx=c_idx)
        d_cv = _mm(f"conv_out_dx_{l}", dyb, wob, "nt", BF16)
        g_ob = _mm(f"conv_out_dw_{l}", sv["cv"], dyb, "tn", BF16, split="cols", c_idx=c_idx)
        if l > 0:
            du2, d_wdw, d_cg, d_cb = _conv_bwd_a(f"conv_bwd_a_{l}", sv["u"], sv["u2"], d_cv, vec(conv_ln_g, l),
                                                 vec(conv_ln_b, l))
            d_glu = _conv_bwd_b(f"conv_bwd_b_{l}", du2, z, off_ga, off_gb, w_dw32[l])
            dqr, dkr, dvb, d_sink = _attn_bwd(f"attn_bwd_{l}", sv["qr"], sv["kr"], sv["vb"], sink_b[l], d_att)
        else:
            early = {1: g_gu, 2: g_oa, 3: g_ob, 4: g_out, 5: g_down}
            (du2, d_wdw, d_cg, d_cb), (got_0,) = _conv_bwd_a(
                f"conv_bwd_a_{l}", sv["u"], sv["u2"], d_cv, vec(conv_ln_g, l), vec(conv_ln_b, l),
                jobs=[_job_pair(list(early.values()))])
            halves_0 = {j: _pair_sum(f"pair_sum_0_{j}", early[j], got_0[n]) for n, j in enumerate(early)}
            parts_0 = {}
            d_glu, ((parts_0[5],),) = _conv_bwd_b(f"conv_bwd_b_{l}", du2, z, off_ga, off_gb, w_dw32[l],
                                                  jobs=[_job_chips([halves_0[5]], [kinds[5]])])
            (dqr, dkr, dvb, d_sink), ((parts_0[1],),) = _attn_bwd(
                f"attn_bwd_{l}", sv["qr"], sv["kr"], sv["vb"], sink_b[l], d_att,
                jobs=[_job_chips([halves_0[1]], [kinds[1]])])
        d_qkv = _rowwise(
            f"qkv_bwd_{l}",
            lambda dq_, dk_, dv_, cs, up, dn: jnp.concatenate([rope_t(dq_, cs, up, dn), rope_t(dk_, cs, up, dn), dv_], axis=1),
            seq, [(dqr, 0, d), (dkr, 0, dkv), (dvb, 0, dkv)] + tables, [], [(d + 2 * dkv, BF16)])
        dz = jnp.concatenate([d_qkv, d_glu, d_gates], axis=1)
        dh, (parts_rest,) = unpack(_mm(f"in_proj_dx_{l}", dz, wi, "nt", F32,
                                       jobs=riding(lambda: _job_chips(halves[2:], kinds[2:]))))
        reduced = None if pending is None else [
            _sum_slots(f"sum_chips_{l + 1}_{j}", p, into_slot=c_idx) for j, p in enumerate(parts_in + parts_gu + parts_rest)]
        last_jobs = riding(lambda: _job_join(reduced))
        if l == 0:
            last_jobs = last_jobs + [_job_chips([halves_0[j] for j in (2, 3, 4)], kinds[2:5])]
        g_in = _mm(f"in_proj_dw_{l}", sv["h"], dz, "tn", BF16, split="rows", c_idx=c_idx, jobs=last_jobs)
        if last_jobs:
            g_in, job_res = g_in
            if l == 0:
                parts_0[2], parts_0[3], parts_0[4] = job_res[-1]
            if pending is not None:
                big = adamw_all(l + 1, job_res[0], big)
        dxn = dxp
        small[l] = dict(gt_a=d_gta, sh_f=d_shf, sc_f=d_scf, gt_f=d_gtf, ln1_g=d_g1, ln1_b=d_b1, ln2_g=d_g2, ln2_b=d_b2,
                        conv_ln_g=d_cg, conv_ln_b=d_cb, sink=d_sink[:, :1].reshape(1, hq), w_dw=d_wdw[:CONV_WIDTH])
        pending = [g_in, g_gu, g_oa, g_ob, g_out, g_down]

    ((got_in,),) = _run_jobs("rs_pair_0", [_job_pair(pending[:1])])
    half_in = _pair_sum("pair_sum_0_0", pending[0], got_in)

    grad_x, d_sca0, d_sha0 = _rowwise(
        "modulate_in_bwd", lambda xv, dhv, dxv, sc: (dxv + dhv * (1.0 + sc), jnp.sum(dhv * xv, axis=0, keepdims=True),
                                                     jnp.sum(dhv, axis=0, keepdims=True)),
        seq, [(x2, 0, d), (dh, 0, d), (dxn, 0, d)], [sc_a[0]], [(d, F32)], [vec_d, vec_d])
    small[0]["sc_a"], small[0]["sh_a"] = d_sca0, d_sha0

    order = ("sh_a", "sc_a", "gt_a", "sh_f", "sc_f", "gt_f", "conv_ln_g", "conv_ln_b", "ln1_g", "ln1_b", "ln2_g", "ln2_b")
    rows = []
    for l in range(n_layers):
        rows += [small[l][k] for k in order]
        rows.append(jnp.pad(small[l]["sink"], ((0, 0), (0, d - hq))))
        rows.append(small[l]["w_dw"])
    rows.append(jnp.pad(loss_part, ((0, 0), (0, d - LANE))))
    n_small = sum(r.shape[0] for r in rows)
    pad_rows = (-n_small) % 8
    packed = jnp.concatenate(rows + [jnp.zeros((pad_rows, d), F32)], axis=0)
    everyone = _gather_small("gather_small_grads", packed, ALL_DEVICES)
    total = _sum_slots("sum_small_grads", everyone)
    per_layer = len(order) + 1 + CONV_WIDTH
    tot = total[:n_layers * per_layer].reshape(n_layers, per_layer, d)
    g_mod = tot[:, :N_MOD].reshape(n_layers, N_MOD * d)
    g_small = {k: tot[:, N_MOD + j] for j, k in enumerate(order[N_MOD:])}
    g_sink = tot[:, len(order), :hq]
    g_dw_full = tot[:, len(order) + 1:]
    cols_dw = w_dw.shape[2]
    g_dw = lax.dynamic_slice_in_dim(g_dw_full, chip * cols_dw, cols_dw, axis=2)
    loss = total[n_layers * per_layer, 0]

    d_mod_all = everyone[:, :n_layers * per_layer].reshape(8, n_layers, per_layer, d)[:, :, :N_MOD]
    d_mod_all = d_mod_all.reshape(8, n_layers, N_MOD * d)
    cols_ada = w_ada.shape[2]
    d_mod_mine = lax.dynamic_slice_in_dim(d_mod_all, chip * cols_ada, cols_ada, axis=2)
    dmod16 = jnp.concatenate([d_mod_mine, jnp.zeros_like(d_mod_mine)], axis=0)
    dmod16 = jnp.transpose(dmod16, (1, 0, 2)).astype(BF16)
    ada, ((parts_0[0],),) = _adamw_ada("adamw_ada", c16, dmod16, w_ada, m_w_ada, v_w_ada,
                                       jobs=[_job_chips([half_in], kinds[:1])])
    reduced = [_sum_slots(f"sum_chips_0_{j}", parts_0[j], into_slot=c_idx) for j in range(6)]
    (full,) = _run_jobs("rs_join_0", [_job_join(reduced)])
    big = adamw_all(0, full, big)

    def small_step(name, g, w, m, v):
        shp = w.shape
        g2, w2, m2, v2 = (a.reshape(-1, shp[-1]) for a in (g, w, m, v))
        return (g,) + tuple(a.reshape(shp) for a in _adamw_small(name, g2, w2, m2, v2))

    res = {
        "w_ada": ada,
        "b_ada": small_step("adamw_b_ada", g_mod, b_ada, m_b_ada, v_b_ada),
        "sink": small_step("adamw_sink", g_sink, sink, m_sink, v_sink),
        "w_dw": small_step("adamw_w_dw", g_dw, w_dw, m_w_dw, v_w_dw),
        "conv_ln_g": small_step("adamw_conv_ln_g", g_small["conv_ln_g"], conv_ln_g, m_conv_ln_g, v_conv_ln_g),
        "conv_ln_b": small_step("adamw_conv_ln_b", g_small["conv_ln_b"], conv_ln_b, m_conv_ln_b, v_conv_ln_b),
        "ln1_g": small_step("adamw_ln1_g", g_small["ln1_g"], ln1_g, m_ln1_g, v_ln1_g),
        "ln1_b": small_step("adamw_ln1_b", g_small["ln1_b"], ln1_b, m_ln1_b, v_ln1_b),
        "ln2_g": small_step("adamw_ln2_g", g_small["ln2_g"], ln2_g, m_ln2_g, v_ln2_g),
        "ln2_b": small_step("adamw_ln2_b", g_small["ln2_b"], ln2_b, m_ln2_b, v_ln2_b),
        "w_in": big[0], "w_gu": big[1], "w_oa": big[2], "w_ob": big[3], "w_out": big[4], "w_down": big[5],
    }
    names = ("w_ada", "b_ada", "w_in", "sink", "w_dw", "conv_ln_g", "conv_ln_b", "w_oa", "w_ob", "w_out", "ln1_g", "ln1_b",
             "w_gu", "w_down", "ln2_g", "ln2_b")
    outs = [loss, grad_x[None]]
    for field in range(4):
        outs += [res[k][field] for k in names]
    return tuple(outs)
```

```python
import functools
import math

import jax
import jax.numpy as jnp
from jax import lax
from jax.experimental import pallas as pl
from jax.experimental.pallas import tpu as pltpu

F32 = jnp.float32
BF16 = jnp.bfloat16
MESH = pl.DeviceIdType.MESH

HEAD_DIM = 128
GQA_GROUP = 4
WINDOW = 128
BLOCK = 128
BAND = 3 * BLOCK
ROPE_DIM = HEAD_DIM // 4
ROPE_THETA = 500000.0
CONV_WIDTH = 31
CONV_PAD = CONV_WIDTH // 2
CONV_HALO = 16
N_MOD = 6
LN_EPS = 1e-5
NEG_INF = -1e30
ADAM_LR = 0.001
ADAM_B1 = 0.9
ADAM_B2 = 0.999
ADAM_EPS = 1e-08
ADAM_WD = 0.01
ADAM_STEP = 10

LANE = 128
SUBLANES = 8
V7X_VMEM_LIMIT = 56 * 1024 * 1024
ROW_TILE = 256
CONV_ROWS = 32
CONV_LANES = 256

HBM_SPEC = pl.BlockSpec(memory_space=pltpu.HBM)


def _params(*sem):
    return pltpu.CompilerParams(dimension_semantics=sem, vmem_limit_bytes=V7X_VMEM_LIMIT)


def _pick(n, cands, even=False):
    for t in cands:
        if n % t == 0 and (not even or (n // t) % 2 == 0):
            return t
    raise ValueError(f"no tile for {n} in {cands}")


BLOCK_BUDGET = 10 * 1024 * 1024
MM_BLOCK_BUDGET = 40 * 1024 * 1024


def _rows_within(n_rows, bytes_per_row, cands=(256, 128, 64, 32, 16, 8)):
    fit = [t for t in cands if n_rows % t == 0]
    for t in fit:
        if t * bytes_per_row <= BLOCK_BUDGET:
            return t
    return fit[-1]


def _sigmoid(v):
    return jax.nn.sigmoid(v)


def _const_map(ndim):
    return lambda *_: (0,) * ndim


def _rowwise(name, fn, n_rows, row_ins, vec_ins, row_outs, vec_outs=()):
    per_row = sum(w * a.dtype.itemsize for a, _, w in row_ins) + sum(w * jnp.dtype(dt).itemsize for w, dt in row_outs)
    tr = _rows_within(n_rows, per_row, (ROW_TILE, 128, 64))
    in_specs, args, pieces = [], [], []
    for arr, off, width in row_ins:
        bw = math.gcd(off, width) if off else width
        assert bw % LANE == 0 and arr.shape[0] == n_rows
        pieces.append(width // bw)
        for p in range(width // bw):
            in_specs.append(pl.BlockSpec((tr, bw), functools.partial(lambda i, blk: (i, blk), blk=off // bw + p)))
            args.append(arr)
    for v in vec_ins:
        in_specs.append(pl.BlockSpec(v.shape, _const_map(v.ndim)))
        args.append(v)
    out_shape = [jax.ShapeDtypeStruct((n_rows, w), dt) for w, dt in row_outs]
    out_specs = [pl.BlockSpec((tr, w), lambda i: (i, 0)) for w, _ in row_outs]
    for shp, dt in vec_outs:
        out_shape.append(jax.ShapeDtypeStruct(shp, dt))
        out_specs.append(pl.BlockSpec(shp, _const_map(len(shp))))
    n_in, n_row_out = len(args), len(row_outs)

    def body(*refs):
        in_refs, out_refs = refs[:n_in], refs[n_in:]
        vals, k = [], 0
        for npc in pieces:
            ps = [in_refs[k + p][...] for p in range(npc)]
            k += npc
            vals.append((ps[0] if npc == 1 else jnp.concatenate(ps, axis=1)).astype(F32))
        for _ in vec_ins:
            vals.append(in_refs[k][...])
            k += 1
        outs = fn(*vals)
        if not isinstance(outs, (tuple, list)):
            outs = (outs,)
        assert len(outs) == len(out_refs)
        for j in range(n_row_out):
            out_refs[j][...] = outs[j].astype(out_refs[j].dtype)
        if vec_outs:
            @pl.when(pl.program_id(0) == 0)
            def _():
                for j in range(n_row_out, len(out_refs)):
                    out_refs[j][...] = jnp.zeros(out_refs[j].shape, out_refs[j].dtype)
            for j in range(n_row_out, len(out_refs)):
                out_refs[j][...] += outs[j].astype(out_refs[j].dtype)

    res = pl.pallas_call(
        body, name=name, grid=(n_rows // tr,), in_specs=in_specs, out_specs=out_specs, out_shape=out_shape,
        compiler_params=_params("arbitrary"),
    )(*args)
    return res[0] if len(res) == 1 else res


def _mm(name, a, b, mode, out_dtype, b_layer=None, split=None, c_idx=None, jobs=()):
    bshape = b.shape[1:] if b_layer is not None else b.shape
    if mode == "nn":
        (m, k), (k2, n) = a.shape, bshape
        dims = (((1,), (0,)), ((), ()))
    elif mode == "nt":
        (m, k), (n, k2) = a.shape, bshape
        dims = (((1,), (1,)), ((), ()))
    else:
        (k, m), (k2, n) = a.shape, bshape
        dims = (((0,), (0,)), ((), ()))
    assert k == k2, (name, a.shape, b.shape)
    tm = _pick(m, (1024, 512, 256, 128, 16), even=(split == "rows"))
    tn = _pick(n, (1024, 512, 256, 128), even=(split == "cols"))
    out_bytes = jnp.dtype(out_dtype).itemsize
    b_bytes = b.dtype.itemsize

    def blocks_fit(t):
        acc = 0 if t == k else tm * tn * 4
        return 2 * (tm * t * a.dtype.itemsize + tn * t * b_bytes + tm * tn * out_bytes) + acc <= MM_BLOCK_BUDGET

    tk = next(t for t in (4096, 2816, 2048, 1408, 1024, 704, 512, 256, 128) if k % t == 0 and (blocks_fit(t) or t == 128))
    ni, nj, nk = m // tm, n // tn, k // tk

    if mode == "nn":
        a_spec = pl.BlockSpec((tm, tk), lambda i, j, kk, *_: (i, kk))
        b_blk, b_map = (tk, tn), (lambda i, j, kk: (kk, j))
    elif mode == "nt":
        a_spec = pl.BlockSpec((tm, tk), lambda i, j, kk, *_: (i, kk))
        b_blk, b_map = (tn, tk), (lambda i, j, kk: (j, kk))
    else:
        a_spec = pl.BlockSpec((tk, tm), lambda i, j, kk, *_: (kk, i))
        b_blk, b_map = (tk, tn), (lambda i, j, kk: (kk, j))
    if b_layer is None:
        b_spec = pl.BlockSpec(b_blk, lambda i, j, kk, *_: b_map(i, j, kk))
    else:
        b_spec = pl.BlockSpec((None,) + b_blk, lambda i, j, kk, *_: (b_layer,) + b_map(i, j, kk))

    if split is None:
        out_shape = jax.ShapeDtypeStruct((m, n), out_dtype)
        o_spec = pl.BlockSpec((tm, tn), lambda i, j, kk, *_: (i, j))
    elif split == "rows":
        out_shape = jax.ShapeDtypeStruct((2, m // 2, n), out_dtype)
        o_spec = pl.BlockSpec(
            (None, tm, tn), lambda i, j, kk, c_ref: (jnp.where(i // (ni // 2) == c_ref[0], 0, 1), i % (ni // 2), j))
    else:
        out_shape = jax.ShapeDtypeStruct((2, m, n // 2), out_dtype)
        o_spec = pl.BlockSpec(
            (None, tm, tn), lambda i, j, kk, c_ref: (jnp.where(j // (nj // 2) == c_ref[0], 0, 1), i, j % (nj // 2)))

    n_job_in = sum(len(jb.ins) for jb in jobs)
    n_job_out = sum(len(jb.out_shapes) for jb in jobs)
    n_acc = 0 if nk == 1 else 1

    def body(*refs):
        if split is not None:
            refs = refs[1:]
        a_ref, b_ref = refs[:2]
        job_ins = refs[2:2 + n_job_in]
        o_ref = refs[2 + n_job_in]
        job_outs = refs[3 + n_job_in:3 + n_job_in + n_job_out]
        scratch_refs = refs[3 + n_job_in + n_job_out:]
        cut = _job_refs(jobs, job_ins, job_outs, scratch_refs[n_acc:])
        i, j, kk = pl.program_id(0), pl.program_id(1), pl.program_id(2)

        if jobs:
            @pl.when((i == 0) & (j == 0) & (kk == 0))
            def _():
                for jb, parts in zip(jobs, cut):
                    jb.start(*parts)

        part = lax.dot_general(a_ref[...].astype(BF16), b_ref[...].astype(BF16), dims, preferred_element_type=F32)
        if nk == 1:
            o_ref[...] = part.astype(o_ref.dtype)
        else:
            acc_ref = scratch_refs[0]

            @pl.when(kk == 0)
            def _():
                acc_ref[...] = part

            @pl.when(kk > 0)
            def _():
                acc_ref[...] += part

            @pl.when(kk == nk - 1)
            def _():
                o_ref[...] = acc_ref[...].astype(o_ref.dtype)

        if jobs:
            @pl.when((i == ni - 1) & (j == nj - 1) & (kk == nk - 1))
            def _():
                for jb, parts in zip(jobs, cut):
                    jb.finish(*parts)

    scratch = ([] if nk == 1 else [pltpu.VMEM((tm, tn), F32)]) + [s for jb in jobs for s in jb.sems]
    params = _params(*(["arbitrary"] * 3 if jobs else ["parallel", "parallel", "arbitrary"]))
    in_specs = [a_spec, b_spec] + [HBM_SPEC] * n_job_in
    out_specs = [o_spec] + [HBM_SPEC] * n_job_out
    out_shapes = [out_shape] + [s for jb in jobs for s in jb.out_shapes]
    operands = [a, b] + [x for jb in jobs for x in jb.ins]
    n_pre = 0 if split is None else 1
    aliases = _job_aliases(jobs, n_pre + 2, 1)
    if split is None:
        res = pl.pallas_call(
            body, name=name, grid=(ni, nj, nk), in_specs=in_specs, out_specs=out_specs, out_shape=out_shapes,
            scratch_shapes=scratch, input_output_aliases=aliases, compiler_params=params,
        )(*operands)
    else:
        grid_spec = pltpu.PrefetchScalarGridSpec(
            num_scalar_prefetch=1, grid=(ni, nj, nk), in_specs=in_specs, out_specs=out_specs, scratch_shapes=scratch)
        res = pl.pallas_call(body, name=name, grid_spec=grid_spec, out_shape=out_shapes, input_output_aliases=aliases,
                             compiler_params=params)(c_idx, *operands)
    if not jobs:
        return res[0]
    return res[0], _job_results(jobs, res[1:])


def _attn_tile(seq):
    return _pick(seq, (256, 128))


def _heads_stacked(ref, b):
    return jnp.concatenate(
        [ref[b * BLOCK:(b + 1) * BLOCK, g * HEAD_DIM:(g + 1) * HEAD_DIM] for g in range(GQA_GROUP)], axis=0)


def _attn_scores(q_ref, k_ref, v_ref, sink_ref, kvh, i, b, tq, seq):
    rows = GQA_GROUP * BLOCK
    q0 = i * tq + b * BLOCK
    k_off = pl.multiple_of(jnp.clip(q0 - BLOCK, 0, seq - BAND), BLOCK)
    kw = k_ref[pl.ds(k_off, BAND), :]
    vw = v_ref[pl.ds(k_off, BAND), :]
    q_pos = q0 + (lax.broadcasted_iota(jnp.int32, (rows, BAND), 0) & (BLOCK - 1))
    k_pos = k_off + lax.broadcasted_iota(jnp.int32, (rows, BAND), 1)
    valid = jnp.abs(k_pos - q_pos) <= WINDOW
    qs = _heads_stacked(q_ref, b)
    s = lax.dot_general(qs, kw, (((1,), (1,)), ((), ())), preferred_element_type=F32) * (HEAD_DIM ** -0.5)
    s = jnp.where(valid, s, NEG_INF)
    sink = jnp.concatenate(
        [jnp.broadcast_to(sink_ref[pl.ds(kvh * GQA_GROUP + g, 1), :][:, :1], (BLOCK, 1)) for g in range(GQA_GROUP)], axis=0)
    m = jnp.maximum(jnp.max(s, axis=-1, keepdims=True), sink)
    p = jnp.exp(s - m)
    p_sink = jnp.exp(sink - m)
    denom = jnp.sum(p, axis=-1, keepdims=True) + p_sink
    return k_off, kw, vw, qs, p / denom, p_sink / denom


def _attn_fwd(name, qr, kr, vb, sink_b, jobs=()):
    seq, dq = qr.shape
    nkv = kr.shape[1] // HEAD_DIM
    tq = _attn_tile(seq)
    gw = GQA_GROUP * HEAD_DIM

    def body(q_ref, k_ref, v_ref, sink_ref, o_ref):
        kvh, i = pl.program_id(0), pl.program_id(1)
        for b in range(tq // BLOCK):
            _, _, vw, _, pn, _ = _attn_scores(q_ref, k_ref, v_ref, sink_ref, kvh, i, b, tq, seq)
            o = jnp.dot(pn.astype(BF16), vw, preferred_element_type=F32).astype(o_ref.dtype)
            for g in range(GQA_GROUP):
                o_ref[b * BLOCK:(b + 1) * BLOCK, g * HEAD_DIM:(g + 1) * HEAD_DIM] = o[g * BLOCK:(g + 1) * BLOCK]

    (att,), job_res = _call(
        name, body, (nkv, seq // tq),
        [
            pl.BlockSpec((tq, gw), lambda h, i: (i, h)),
            pl.BlockSpec((seq, HEAD_DIM), lambda h, i: (0, h)),
            pl.BlockSpec((seq, HEAD_DIM), lambda h, i: (0, h)),
            pl.BlockSpec(sink_b.shape, lambda h, i: (0, 0)),
        ],
        [pl.BlockSpec((tq, gw), lambda h, i: (i, h))], [jax.ShapeDtypeStruct((seq, dq), BF16)], [],
        (qr, kr, vb, sink_b), ("arbitrary", "arbitrary"), jobs)
    return (att, job_res) if jobs else att


def _attn_bwd(name, qr, kr, vb, sink_b, d_att, jobs=()):
    seq, dq = qr.shape
    dkv = kr.shape[1]
    nkv = dkv // HEAD_DIM
    tq = _attn_tile(seq)
    gw = GQA_GROUP * HEAD_DIM
    tn_dims = (((0,), (0,)), ((), ()))

    def body(q_ref, k_ref, v_ref, sink_ref, do_ref, dq_ref, dk_ref, dv_ref, dsink_ref):
        kvh, i = pl.program_id(0), pl.program_id(1)

        @pl.when(i == 0)
        def _():
            dk_ref[...] = jnp.zeros(dk_ref.shape, F32)
            dv_ref[...] = jnp.zeros(dv_ref.shape, F32)

        @pl.when((i == 0) & (kvh == 0))
        def _():
            dsink_ref[...] = jnp.zeros(dsink_ref.shape, F32)

        for b in range(tq // BLOCK):
            k_off, kw, vw, qs, pn, pn_sink = _attn_scores(q_ref, k_ref, v_ref, sink_ref, kvh, i, b, tq, seq)
            dos = _heads_stacked(do_ref, b)
            dp = lax.dot_general(dos, vw, (((1,), (1,)), ((), ())), preferred_element_type=F32)
            delta = jnp.sum(pn * dp, axis=-1, keepdims=True)
            ds = (pn * (dp - delta) * (HEAD_DIM ** -0.5)).astype(BF16)
            dqs = jnp.dot(ds, kw, preferred_element_type=F32)
            sink_term = pn_sink * delta
            for g in range(GQA_GROUP):
                dq_ref[b * BLOCK:(b + 1) * BLOCK, g * HEAD_DIM:(g + 1) * HEAD_DIM] = dqs[g * BLOCK:(g + 1) * BLOCK]
                d_sink = -jnp.sum(sink_term[g * BLOCK:(g + 1) * BLOCK], axis=0, keepdims=True)
                dsink_ref[pl.ds(kvh * GQA_GROUP + g, 1), :] += jnp.broadcast_to(d_sink, (1, LANE))
            dk_ref[pl.ds(k_off, BAND), :] += lax.dot_general(ds, qs, tn_dims, preferred_element_type=F32)
            dv_ref[pl.ds(k_off, BAND), :] += lax.dot_general(pn.astype(BF16), dos, tn_dims, preferred_element_type=F32)

    res, job_res = _call(
        name, body, (nkv, seq // tq),
        [
            pl.BlockSpec((tq, gw), lambda h, i: (i, h)),
            pl.BlockSpec((seq, HEAD_DIM), lambda h, i: (0, h)),
            pl.BlockSpec((seq, HEAD_DIM), lambda h, i: (0, h)),
            pl.BlockSpec(sink_b.shape, lambda h, i: (0, 0)),
            pl.BlockSpec((tq, gw), lambda h, i: (i, h)),
        ],
        [
            pl.BlockSpec((tq, gw), lambda h, i: (i, h)),
            pl.BlockSpec((seq, HEAD_DIM), lambda h, i: (0, h)),
            pl.BlockSpec((seq, HEAD_DIM), lambda h, i: (0, h)),
            pl.BlockSpec(sink_b.shape, lambda h, i: (0, 0)),
        ],
        [
            jax.ShapeDtypeStruct((seq, dq), F32),
            jax.ShapeDtypeStruct((seq, dkv), F32),
            jax.ShapeDtypeStruct((seq, dkv), F32),
            jax.ShapeDtypeStruct(sink_b.shape, F32),
        ],
        [], (qr, kr, vb, sink_b, d_att), ("arbitrary", "arbitrary"), jobs)
    return (res, job_res) if jobs else res


def _halo_specs(tr, width, n_rows):
    per, last = tr // CONV_HALO, n_rows // CONV_HALO - 1
    return [
        pl.BlockSpec((tr, width), lambda i: (i, 0)),
        pl.BlockSpec((CONV_HALO, width), lambda i: (jnp.maximum(i * per - 1, 0), 0)),
        pl.BlockSpec((CONV_HALO, width), lambda i: (jnp.minimum((i + 1) * per, last), 0)),
    ]


def _ext_scratch(tr, width):
    return pltpu.VMEM((SUBLANES, tr + 2 * CONV_HALO, width), F32)


def _fill_ext(ext_ref, main_ref, prev_ref, next_ref, n_steps, tr):
    i = pl.program_id(0)
    ext_ref[0, 0:CONV_HALO, :] = jnp.where(i > 0, prev_ref[...], 0.0)
    ext_ref[0, CONV_HALO:CONV_HALO + tr, :] = main_ref[...]
    ext_ref[0, CONV_HALO + tr:, :] = jnp.where(i < n_steps - 1, next_ref[...], 0.0)
    rows = tr + 2 * CONV_HALO - SUBLANES
    for p in range(1, SUBLANES):
        ext_ref[p, 0:rows, :] = ext_ref[0, p:p + rows, :]


def _tap(ext_ref, r0, t, cols):
    whole, phase = divmod(1 + t, SUBLANES)
    return ext_ref[phase, r0 + whole * SUBLANES:r0 + whole * SUBLANES + CONV_ROWS, cols]


def _conv_taps(ext_ref, w_ref, out_ref, tr, width, flip):
    cw = min(CONV_LANES, width)
    for cc in range(width // cw):
        cols = slice(cc * cw, (cc + 1) * cw)
        for rc in range(tr // CONV_ROWS):
            acc = jnp.zeros((CONV_ROWS, cw), F32)
            for t in range(CONV_WIDTH):
                wt = CONV_WIDTH - 1 - t if flip else t
                acc += _tap(ext_ref, rc * CONV_ROWS, t, cols) * w_ref[wt:wt + 1, cols]
            out_ref[rc * CONV_ROWS:(rc + 1) * CONV_ROWS, cols] = acc


def _ln(v, g, b):
    mu = jnp.mean(v, axis=-1, keepdims=True)
    vc = v - mu
    var = jnp.mean(vc * vc, axis=-1, keepdims=True)
    return vc * lax.rsqrt(var + LN_EPS) * g + b


def _conv_fwd(name, u, w32, ln_g, ln_b, jobs=()):
    n_rows, width = u.shape
    tr = min(ROW_TILE, n_rows)
    n_steps = n_rows // tr

    def body(main_ref, prev_ref, next_ref, w_ref, g_ref, b_ref, u2_ref, cv_ref, ext_ref):
        _fill_ext(ext_ref, main_ref, prev_ref, next_ref, n_steps, tr)
        _conv_taps(ext_ref, w_ref, u2_ref, tr, width, flip=False)
        u3 = _ln(u2_ref[...], g_ref[...], b_ref[...])
        cv_ref[...] = (u3 * _sigmoid(u3)).astype(cv_ref.dtype)

    vec = lambda a: pl.BlockSpec(a.shape, lambda i: (0, 0))
    res, job_res = _call(
        name, body, (n_steps,), _halo_specs(tr, width, n_rows) + [vec(w32), vec(ln_g), vec(ln_b)],
        [pl.BlockSpec((tr, width), lambda i: (i, 0))] * 2,
        [jax.ShapeDtypeStruct((n_rows, width), F32), jax.ShapeDtypeStruct((n_rows, width), BF16)],
        [_ext_scratch(tr, width)], (u, u, u, w32, ln_g, ln_b), ("arbitrary",), jobs)
    return (res, job_res) if jobs else res


def _conv_bwd_a(name, u, u2, d_cv, ln_g, ln_b, jobs=()):
    n_rows, width = u.shape
    tr = min(ROW_TILE // 2, n_rows)
    n_steps = n_rows // tr

    def body(main_ref, prev_ref, next_ref, u2_ref, dcv_ref, g_ref, b_ref, du2_ref, dw_ref, dg_ref, db_ref, ext_ref):
        @pl.when(pl.program_id(0) == 0)
        def _():
            dw_ref[...] = jnp.zeros(dw_ref.shape, F32)
            dg_ref[...] = jnp.zeros(dg_ref.shape, F32)
            db_ref[...] = jnp.zeros(db_ref.shape, F32)

        _fill_ext(ext_ref, main_ref, prev_ref, next_ref, n_steps, tr)

        def swish_ln(v, g, b):
            u3 = _ln(v, g, b)
            return u3 * _sigmoid(u3)

        _, vjp = jax.vjp(swish_ln, u2_ref[...], g_ref[...], b_ref[...])
        du2, dg, db = vjp(dcv_ref[...].astype(F32))
        du2_ref[...] = du2
        dg_ref[...] += dg
        db_ref[...] += db
        for cc in range(width // LANE):
            cols = slice(cc * LANE, (cc + 1) * LANE)
            for t0 in range(0, CONV_WIDTH, 16):
                taps = range(t0, min(t0 + 16, CONV_WIDTH))
                accs = {t: jnp.zeros((SUBLANES, LANE), F32) for t in taps}
                for rc in range(tr // CONV_ROWS):
                    r0 = rc * CONV_ROWS
                    d_blk = du2_ref[r0:r0 + CONV_ROWS, cols]
                    for t in taps:
                        prod = d_blk * _tap(ext_ref, r0, t, cols)
                        for q in range(CONV_ROWS // SUBLANES):
                            accs[t] = accs[t] + prod[q * SUBLANES:(q + 1) * SUBLANES]
                for t in taps:
                    dw_ref[t:t + 1, cols] += jnp.sum(accs[t], axis=0, keepdims=True)

    vec = lambda a: pl.BlockSpec(a.shape, lambda i: (0, 0))
    row = pl.BlockSpec((tr, width), lambda i: (i, 0))
    res, job_res = _call(
        name, body, (n_steps,), _halo_specs(tr, width, n_rows) + [row, row, vec(ln_g), vec(ln_b)],
        [row, pl.BlockSpec((32, width), lambda i: (0, 0)), vec(ln_g), vec(ln_b)],
        [jax.ShapeDtypeStruct((n_rows, width), F32), jax.ShapeDtypeStruct((32, width), F32),
         jax.ShapeDtypeStruct(ln_g.shape, F32), jax.ShapeDtypeStruct(ln_b.shape, F32)],
        [_ext_scratch(tr, width)], (u, u, u, u2, d_cv, ln_g, ln_b), ("arbitrary",), jobs)
    return (res, job_res) if jobs else res


def _conv_bwd_b(name, du2, z, off_a, off_b, w32, jobs=()):
    n_rows, width = du2.shape
    tr = min(ROW_TILE, n_rows)
    n_steps = n_rows // tr
    bw = math.gcd(math.gcd(off_a, off_b), width)
    npc = width // bw

    def body(*refs):
        main_ref, prev_ref, next_ref = refs[:3]
        a_refs, b_refs = refs[3:3 + npc], refs[3 + npc:3 + 2 * npc]
        w_ref, out_ref, ext_ref, du_ref = refs[3 + 2 * npc:]
        _fill_ext(ext_ref, main_ref, prev_ref, next_ref, n_steps, tr)
        _conv_taps(ext_ref, w_ref, du_ref, tr, width, flip=True)
        for p in range(npc):
            cols = slice(p * bw, (p + 1) * bw)
            du = du_ref[:, cols]
            sg = _sigmoid(b_refs[p][...].astype(F32))
            out_ref[:, p * bw:(p + 1) * bw] = (du * sg).astype(out_ref.dtype)
            out_ref[:, width + p * bw:width + (p + 1) * bw] = (
                du * a_refs[p][...].astype(F32) * sg * (1.0 - sg)).astype(out_ref.dtype)

    def piece(off, p):
        return pl.BlockSpec((tr, bw), functools.partial(lambda i, blk: (i, blk), blk=off // bw + p))

    in_specs = _halo_specs(tr, width, n_rows)
    in_specs += [piece(off_a, p) for p in range(npc)] + [piece(off_b, p) for p in range(npc)]
    in_specs.append(pl.BlockSpec(w32.shape, lambda i: (0, 0)))
    (d_glu,), job_res = _call(
        name, body, (n_steps,), in_specs, [pl.BlockSpec((tr, 2 * width), lambda i: (i, 0))],
        [jax.ShapeDtypeStruct((n_rows, 2 * width), BF16)], [_ext_scratch(tr, width), pltpu.VMEM((tr, width), F32)],
        (du2, du2, du2, *([z] * (2 * npc)), w32), ("arbitrary",), jobs)
    return (d_glu, job_res) if jobs else d_glu


def _place():
    return lax.axis_index("x"), lax.axis_index("y"), lax.axis_index("c")


def _flip(v, m):
    return 1 - v if m else v


def _gather_small(name, v, masks):
    varies = [any(m[a] for m in masks) for a in range(3)]
    n = len(masks) + 1

    def slot(pos):
        idx = 0
        for a in range(3):
            if varies[a]:
                idx = idx * 2 + pos[a]
        return idx

    def body(v_ref, o_ref, send_sems, recv_sems, local_sem):
        me = _place()
        mine = pltpu.make_async_copy(v_ref, o_ref.at[slot(me)], local_sem)
        mine.start()
        peers = [tuple(_flip(me[a], m[a]) for a in range(3)) for m in masks]
        sends = [pltpu.make_async_remote_copy(v_ref, o_ref.at[slot(me)], send_sems.at[k], recv_sems.at[k],
                                              device_id=peer, device_id_type=MESH) for k, peer in enumerate(peers)]
        for cp in sends:
            cp.start()
        for k, peer in enumerate(peers):
            pltpu.make_async_remote_copy(v_ref, o_ref.at[slot(peer)], send_sems.at[k], recv_sems.at[k],
                                         device_id=peer, device_id_type=MESH).wait_recv()
        for cp in sends:
            cp.wait_send()
        mine.wait()

    return pl.pallas_call(
        body, name=name, in_specs=[HBM_SPEC], out_specs=HBM_SPEC,
        out_shape=jax.ShapeDtypeStruct((n,) + v.shape, v.dtype),
        scratch_shapes=[pltpu.SemaphoreType.DMA((n - 1,)), pltpu.SemaphoreType.DMA((n - 1,)), pltpu.SemaphoreType.DMA(())],
    )(v)


ALL_DEVICES = [(mx, my, mc) for mx in (0, 1) for my in (0, 1) for mc in (0, 1)][1:]
SAME_CORE_CHIPS = [(1, 0, 0), (0, 1, 0), (1, 1, 0)]


def _chips(x, y):
    return [(1 - x, y), (x, 1 - y), (1 - x, 1 - y)]


def _cast_into(name, w, layer, kind, chip_idx):
    _, r, cc = w.shape
    tr = _rows_within(r, cc * 6)
    steps = r // tr
    if kind == "col":
        shape, o_spec = (r, 4 * cc), pl.BlockSpec((tr, cc), lambda i, s_ref: (i, s_ref[0]))
    else:
        shape, o_spec = (4 * r, cc), pl.BlockSpec((tr, cc), lambda i, s_ref: (s_ref[0] * steps + i, 0))

    def body(s_ref, w_ref, o_ref):
        o_ref[...] = w_ref[...].astype(o_ref.dtype)

    grid_spec = pltpu.PrefetchScalarGridSpec(
        num_scalar_prefetch=1, grid=(steps,),
        in_specs=[pl.BlockSpec((None, tr, cc), lambda i, s_ref: (layer, i, 0))], out_specs=o_spec)
    return pl.pallas_call(body, name=name, grid_spec=grid_spec, out_shape=jax.ShapeDtypeStruct(shape, BF16),
                          compiler_params=_params("arbitrary"))(chip_idx, w)


class _Job:
    def __init__(self, ins, out_shapes, aliases, sems, start, finish):
        self.ins, self.out_shapes, self.aliases, self.sems = list(ins), list(out_shapes), dict(aliases), list(sems)
        self.start, self.finish = start, finish


def _job_refs(jobs, in_refs, out_refs, sem_refs):
    cut, i, o, s = [], 0, 0, 0
    for jb in jobs:
        cut.append((in_refs[i:i + len(jb.ins)], out_refs[o:o + len(jb.out_shapes)], sem_refs[s:s + len(jb.sems)]))
        i, o, s = i + len(jb.ins), o + len(jb.out_shapes), s + len(jb.sems)
    return cut


def _job_aliases(jobs, first_in, first_out):
    aliases, i, o = {}, first_in, first_out
    for jb in jobs:
        for a, b in jb.aliases.items():
            aliases[i + a] = o + b
        i, o = i + len(jb.ins), o + len(jb.out_shapes)
    return aliases


def _run_jobs(name, jobs):
    n_in = sum(len(jb.ins) for jb in jobs)
    n_out = sum(len(jb.out_shapes) for jb in jobs)

    def body(*refs):
        cut = _job_refs(jobs, refs[:n_in], refs[n_in:n_in + n_out], refs[n_in + n_out:])
        for jb, parts in zip(jobs, cut):
            jb.start(*parts)
        for jb, parts in zip(jobs, cut):
            jb.finish(*parts)

    res = pl.pallas_call(
        body, name=name, in_specs=[HBM_SPEC] * n_in, out_specs=[HBM_SPEC] * n_out,
        out_shape=[s for jb in jobs for s in jb.out_shapes], input_output_aliases=_job_aliases(jobs, 0, 0),
        scratch_shapes=[s for jb in jobs for s in jb.sems],
    )(*[a for jb in jobs for a in jb.ins])
    return _job_results(jobs, res)


def _job_results(jobs, flat):
    out, o = [], 0
    for jb in jobs:
        out.append(list(flat[o:o + len(jb.out_shapes)]))
        o += len(jb.out_shapes)
    return out


def _call(name, body, grid, in_specs, out_specs, out_shape, scratch, operands, sem, jobs=()):
    n_in, n_out, n_scr = len(in_specs), len(out_specs), len(scratch)
    n_job_in = sum(len(jb.ins) for jb in jobs)
    n_job_out = sum(len(jb.out_shapes) for jb in jobs)

    def full_body(*refs):
        ins, job_ins = refs[:n_in], refs[n_in:n_in + n_job_in]
        rest = refs[n_in + n_job_in:]
        outs, job_outs = rest[:n_out], rest[n_out:n_out + n_job_out]
        scr, sems = rest[n_out + n_job_out:n_out + n_job_out + n_scr], rest[n_out + n_job_out + n_scr:]
        cut = _job_refs(jobs, job_ins, job_outs, sems)
        ids = [pl.program_id(a) for a in range(len(grid))]
        if jobs:
            @pl.when(functools.reduce(lambda p, q: p & q, [i == 0 for i in ids]))
            def _():
                for jb, parts in zip(jobs, cut):
                    jb.start(*parts)
        body(*ins, *outs, *scr)
        if jobs:
            @pl.when(functools.reduce(lambda p, q: p & q, [i == g - 1 for i, g in zip(ids, grid)]))
            def _():
                for jb, parts in zip(jobs, cut):
                    jb.finish(*parts)

    res = pl.pallas_call(
        full_body, name=name, grid=grid, in_specs=list(in_specs) + [HBM_SPEC] * n_job_in,
        out_specs=list(out_specs) + [HBM_SPEC] * n_job_out,
        out_shape=list(out_shape) + [s for jb in jobs for s in jb.out_shapes],
        scratch_shapes=list(scratch) + [s for jb in jobs for s in jb.sems],
        input_output_aliases=_job_aliases(jobs, n_in, n_out), compiler_params=_params(*sem),
    )(*operands, *[a for jb in jobs for a in jb.ins])
    return list(res[:n_out]), _job_results(jobs, res[n_out:])


def _job_gather(fulls, shapes, kinds):
    n = len(fulls)
    for r, _ in shapes:
        assert r % 32 == 0

    def window(o_ref, j, s, h):
        r, cc = shapes[j]
        hr = r // 2
        if kinds[j] == "col":
            return o_ref.at[pl.ds(pl.multiple_of(h * hr, 16), hr), pl.ds(pl.multiple_of(s * cc, LANE), cc)]
        return o_ref.at[pl.ds(pl.multiple_of(s * r + h * hr, 16), hr), :]

    def first_copies(outs, sems):
        x, y, c = _place()
        cps = []
        for j in range(n):
            mine = window(outs[j], j, 2 * x + y, c)
            for k, chip in enumerate(_chips(x, y)):
                cps.append(pltpu.make_async_remote_copy(mine, mine, sems[0].at[3 * j + k], sems[1].at[3 * j + k],
                                                        device_id=(*chip, c), device_id_type=MESH))
        return cps

    def start(ins, outs, sems):
        for cp in first_copies(outs, sems):
            cp.start()

    def finish(ins, outs, sems):
        x, y, c = _place()
        chips = _chips(x, y)
        sibling = (x, y, 1 - c)
        passed = []
        for j in range(n):
            for k, chip in enumerate(chips):
                win = window(outs[j], j, 2 * chip[0] + chip[1], c)
                pltpu.make_async_remote_copy(win, win, sems[0].at[3 * j + k], sems[1].at[3 * j + k],
                                             device_id=(*chip, c), device_id_type=MESH).wait_recv()
                cp = pltpu.make_async_remote_copy(win, win, sems[2].at[3 * j + k], sems[3].at[3 * j + k],
                                                  device_id=sibling, device_id_type=MESH)
                cp.start()
                passed.append(cp)
        for j in range(n):
            for k, chip in enumerate(chips):
                win = window(outs[j], j, 2 * chip[0] + chip[1], 1 - c)
                pltpu.make_async_remote_copy(win, win, sems[2].at[3 * j + k], sems[3].at[3 * j + k],
                                             device_id=sibling, device_id_type=MESH).wait_recv()
        for cp in first_copies(outs, sems) + passed:
            cp.wait_send()

    return _Job(fulls, [jax.ShapeDtypeStruct(f.shape, f.dtype) for f in fulls], {j: j for j in range(n)},
                [pltpu.SemaphoreType.DMA((3 * n,)) for _ in range(4)], start, finish)


def _job_pair(grads):
    n = len(grads)

    def copies(ins, outs, sems):
        x, y, c = _place()
        return [pltpu.make_async_remote_copy(ins[j].at[1], outs[j], sems[0].at[j], sems[1].at[j],
                                             device_id=(x, y, 1 - c), device_id_type=MESH) for j in range(n)]

    def start(ins, outs, sems):
        for cp in copies(ins, outs, sems):
            cp.start()

    def finish(ins, outs, sems):
        for cp in copies(ins, outs, sems):
            cp.wait()

    return _Job(grads, [jax.ShapeDtypeStruct(g.shape[1:], g.dtype) for g in grads], {},
                [pltpu.SemaphoreType.DMA((n,)), pltpu.SemaphoreType.DMA((n,))], start, finish)


def _job_chips(halves, kinds):
    n = len(halves)
    shapes = [(h.shape[0], h.shape[1] // 4) if kinds[j] == "col" else (h.shape[0] // 4, h.shape[1])
              for j, h in enumerate(halves)]

    def part(ref, j, s):
        r, cc = shapes[j]
        if kinds[j] == "col":
            return ref.at[:, pl.ds(pl.multiple_of(s * cc, LANE), cc)]
        return ref.at[pl.ds(pl.multiple_of(s * r, 16), r), :]

    def copies(ins, outs, sems):
        x, y, c = _place()
        s_me = 2 * x + y
        local = [pltpu.make_async_copy(part(ins[j], j, s_me), outs[j].at[s_me], sems[2].at[j]) for j in range(n)]
        sends, recvs = [], []
        for j in range(n):
            for k, chip in enumerate(_chips(x, y)):
                s_peer = 2 * chip[0] + chip[1]
                sends.append(pltpu.make_async_remote_copy(part(ins[j], j, s_peer), outs[j].at[s_me],
                                                          sems[0].at[3 * j + k], sems[1].at[3 * j + k],
                                                          device_id=(*chip, c), device_id_type=MESH))
                dst = outs[j].at[s_peer]
                recvs.append(pltpu.make_async_remote_copy(dst, dst, sems[0].at[3 * j + k], sems[1].at[3 * j + k],
                                                          device_id=(*chip, c), device_id_type=MESH))
        return local, sends, recvs

    def start(ins, outs, sems):
        local, sends, _ = copies(ins, outs, sems)
        for cp in local + sends:
            cp.start()

    def finish(ins, outs, sems):
        local, sends, recvs = copies(ins, outs, sems)
        for cp in recvs:
            cp.wait_recv()
        for cp in sends:
            cp.wait_send()
        for cp in local:
            cp.wait()

    return _Job(halves, [jax.ShapeDtypeStruct((4,) + shapes[j], halves[j].dtype) for j in range(n)], {},
                [pltpu.SemaphoreType.DMA((3 * n,)), pltpu.SemaphoreType.DMA((3 * n,)), pltpu.SemaphoreType.DMA((n,))],
                start, finish)


def _job_join(pairs):
    n = len(pairs)

    def copies(outs, sems):
        x, y, c = _place()
        sends, recvs = [], []
        for j in range(n):
            sends.append(pltpu.make_async_remote_copy(outs[j].at[c], outs[j].at[c], sems[0].at[j], sems[1].at[j],
                                                      device_id=(x, y, 1 - c), device_id_type=MESH))
            theirs = outs[j].at[1 - c]
            recvs.append(pltpu.make_async_remote_copy(theirs, theirs, sems[0].at[j], sems[1].at[j],
                                                      device_id=(x, y, 1 - c), device_id_type=MESH))
        return sends, recvs

    def start(ins, outs, sems):
        for cp in copies(outs, sems)[0]:
            cp.start()

    def finish(ins, outs, sems):
        sends, recvs = copies(outs, sems)
        for cp in recvs:
            cp.wait_recv()
        for cp in sends:
            cp.wait_send()

    return _Job(pairs, [jax.ShapeDtypeStruct(p.shape, p.dtype) for p in pairs], {j: j for j in range(n)},
                [pltpu.SemaphoreType.DMA((n,)), pltpu.SemaphoreType.DMA((n,))], start, finish)


def _pair_sum(name, mine_other, got):
    _, r, cc = mine_other.shape
    tr = _rows_within(r, 3 * cc * mine_other.dtype.itemsize, (256, 128, 64, 32, 16))

    def body(a_ref, b_ref, o_ref):
        o_ref[...] = (a_ref[...].astype(F32) + b_ref[...].astype(F32)).astype(o_ref.dtype)

    return pl.pallas_call(
        body, name=name, grid=(r // tr,),
        in_specs=[pl.BlockSpec((None, tr, cc), lambda i: (0, i, 0)), pl.BlockSpec((tr, cc), lambda i: (i, 0))],
        out_specs=pl.BlockSpec((tr, cc), lambda i: (i, 0)),
        out_shape=jax.ShapeDtypeStruct((r, cc), mine_other.dtype), compiler_params=_params("parallel"),
    )(mine_other, got)


def _sum_slots(name, parts, into_slot=None):
    n, r, cc = parts.shape
    tr = _rows_within(r, cc * (n * parts.dtype.itemsize + 4), (256, 128, 64, 32, 16, 8))

    def body(*refs):
        p_ref, o_ref = refs[-2:]
        acc = p_ref[0].astype(F32)
        for s in range(1, n):
            acc = acc + p_ref[s].astype(F32)
        o_ref[...] = acc

    in_spec = pl.BlockSpec((n, tr, cc), lambda i, *_: (0, i, 0))
    if into_slot is None:
        return pl.pallas_call(
            body, name=name, grid=(r // tr,), in_specs=[in_spec], out_specs=pl.BlockSpec((tr, cc), lambda i: (i, 0)),
            out_shape=jax.ShapeDtypeStruct((r, cc), F32), compiler_params=_params("parallel"),
        )(parts)
    grid_spec = pltpu.PrefetchScalarGridSpec(
        num_scalar_prefetch=1, grid=(r // tr,), in_specs=[in_spec],
        out_specs=pl.BlockSpec((None, tr, cc), lambda i, c_ref: (c_ref[0], i, 0)))
    return pl.pallas_call(body, name=name, grid_spec=grid_spec, out_shape=jax.ShapeDtypeStruct((2, r, cc), F32),
                          compiler_params=_params("arbitrary"))(into_slot, parts)


def _adamw_math(w, g, m, v):
    m = ADAM_B1 * m + (1.0 - ADAM_B1) * g
    v = ADAM_B2 * v + (1.0 - ADAM_B2) * jnp.square(g)
    m_hat = m / (1.0 - ADAM_B1 ** ADAM_STEP)
    v_hat = v / (1.0 - ADAM_B2 ** ADAM_STEP)
    delta = -ADAM_LR * (m_hat / (jnp.sqrt(v_hat) + ADAM_EPS) + ADAM_WD * w)
    return delta, m, v


def _adamw_layer(name, layer, g_pair, kind, w, m, v, prev):
    n_layers, r, cc = w.shape
    if prev is None:
        prev = tuple(lax.empty(w.shape, F32) for _ in range(4))
    if kind == "col":
        g = g_pair.reshape(r, cc)
        tr = _rows_within(r, 8 * cc * 4)
        grid = (r // tr,)
        g_spec = pl.BlockSpec((tr, cc), lambda i: (i, 0))
        blk = pl.BlockSpec((None, tr, cc), lambda i: (layer, i, 0))
    else:
        g = g_pair
        tr = _rows_within(r, 4 * cc * 4)
        grid = (r // tr, 2)
        g_spec = pl.BlockSpec((None, tr, cc // 2), lambda i, h: (h, i, 0))
        blk = pl.BlockSpec((None, tr, cc // 2), lambda i, h: (layer, i, h))

    def body(g_ref, w_ref, m_ref, v_ref, *rest):
        og_ref, od_ref, om_ref, ov_ref = rest[4:]
        gv = g_ref[...]
        delta, m2, v2 = _adamw_math(w_ref[...], gv, m_ref[...], v_ref[...])
        og_ref[...] = gv
        od_ref[...] = delta
        om_ref[...] = m2
        ov_ref[...] = v2

    return pl.pallas_call(
        body, name=name, grid=grid,
        in_specs=[g_spec, blk, blk, blk] + [HBM_SPEC] * 4,
        out_specs=[blk] * 4, out_shape=[jax.ShapeDtypeStruct(w.shape, F32)] * 4,
        input_output_aliases={4: 0, 5: 1, 6: 2, 7: 3}, compiler_params=_params(*(["parallel"] * len(grid))),
    )(g, w, m, v, *prev)


def _adamw_small(name, g, w, m, v):
    def body(g_ref, w_ref, m_ref, v_ref, od_ref, om_ref, ov_ref):
        delta, m2, v2 = _adamw_math(w_ref[...], g_ref[...], m_ref[...], v_ref[...])
        od_ref[...] = delta
        om_ref[...] = m2
        ov_ref[...] = v2

    return pl.pallas_call(body, name=name, out_shape=[jax.ShapeDtypeStruct(w.shape, F32)] * 3)(g, w, m, v)


def _adamw_ada(name, c16, dmod16, w, m, v, jobs=()):
    n_layers, d, cols = w.shape
    tr = _rows_within(d, 7 * cols * 4, (256, 128))
    blk = pl.BlockSpec((None, tr, cols), lambda l, i: (l, i, 0))

    def body(c_ref, dm_ref, w_ref, m_ref, v_ref, og_ref, od_ref, om_ref, ov_ref):
        gv = lax.dot_general(c_ref[...], dm_ref[...], (((0,), (0,)), ((), ())), preferred_element_type=F32)
        delta, m2, v2 = _adamw_math(w_ref[...], gv, m_ref[...], v_ref[...])
        og_ref[...] = gv
        od_ref[...] = delta
        om_ref[...] = m2
        ov_ref[...] = v2

    return _call(
        name, body, (n_layers, d // tr),
        [pl.BlockSpec((16, tr), lambda l, i: (0, i)), pl.BlockSpec((None, 16, cols), lambda l, i: (l, 0, 0)),
         blk, blk, blk],
        [blk] * 4, [jax.ShapeDtypeStruct(w.shape, F32)] * 4, [], (c16, dmod16, w, m, v),
        ("arbitrary", "arbitrary"), jobs)


def kernel(x, c, w_ada, b_ada, w_in, sink, w_dw, conv_ln_g, conv_ln_b, w_oa, w_ob, w_out, ln1_g, ln1_b, w_gu, w_down, ln2_g, ln2_b, loss_target, m_w_ada, m_b_ada, m_w_in, m_sink, m_w_dw, m_conv_ln_g, m_conv_ln_b, m_w_oa, m_w_ob, m_w_out, m_ln1_g, m_ln1_b, m_w_gu, m_w_down, m_ln2_g, m_ln2_b, v_w_ada, v_b_ada, v_w_in, v_sink, v_w_dw, v_conv_ln_g, v_conv_ln_b, v_w_oa, v_w_ob, v_w_out, v_ln1_g, v_ln1_b, v_w_gu, v_w_down, v_ln2_g, v_ln2_b):
    seq, d = x.shape[1], x.shape[2]
    n_layers = w_in.shape[0]
    d_in = 4 * w_in.shape[2]
    d_ff = 4 * w_down.shape[1]
    hq = d // HEAD_DIM
    dkv = (hq // GQA_GROUP) * HEAD_DIM
    off_k, off_v, off_ga, off_gb = d, d + dkv, d + 2 * dkv, 2 * d + 2 * dkv
    off_gta, off_gtb = 3 * d + 2 * dkv, 4 * d + 2 * dkv
    assert d_in == 5 * d + 2 * dkv and seq % ROW_TILE == 0 and seq >= BAND
    alpha = (2.0 * n_layers) ** 0.25

    xi, yi, ci = _place()
    chip = 2 * xi + yi
    batch = 4 * xi + 2 * yi + ci
    c_idx = jnp.reshape(ci, (1,)).astype(jnp.int32)
    x2 = x[0]
    target = loss_target[0]

    c_act = jax.nn.silu(c)
    c_all = _gather_small("gather_c", c_act, ALL_DEVICES).reshape(8, d)
    c16 = jnp.concatenate([c_all, jnp.zeros((8, d), F32)], axis=0).astype(BF16)
    mod_cols = [_mm(f"mod_{l}", c16, w_ada, "nn", F32, b_layer=l) for l in range(n_layers)]
    mod_all = _gather_small("gather_mod", jnp.stack(mod_cols), SAME_CORE_CHIPS)
    mod = lax.dynamic_index_in_dim(mod_all, batch, axis=2, keepdims=False)
    mod = jnp.transpose(mod, (1, 0, 2)).reshape(n_layers, N_MOD * d) + b_ada
    mod = mod.reshape(n_layers, N_MOD, 1, d)
    sh_a, sc_a, gt_a, sh_f, sc_f, gt_f = (mod[:, j] for j in range(N_MOD))

    pos = jnp.arange(seq, dtype=F32)
    inv_freq = ROPE_THETA ** (-jnp.arange(0, ROPE_DIM, 2, dtype=F32) / ROPE_DIM)
    ang = pos[:, None] * inv_freq[None, :]
    cos, sin = jnp.cos(ang), jnp.sin(ang)
    half = ROPE_DIM // 2
    rest = HEAD_DIM - ROPE_DIM
    t_cs = jnp.concatenate([cos, cos, jnp.ones((seq, rest), F32)], axis=1)
    t_up = jnp.concatenate([-sin, jnp.zeros((seq, rest + half), F32)], axis=1)
    t_dn = jnp.concatenate([jnp.zeros((seq, half), F32), sin, jnp.zeros((seq, rest), F32)], axis=1)

    def rope(t, cs, up, dn):
        w = t.shape[1]
        reps = (1, w // HEAD_DIM)
        return (t * jnp.tile(cs, reps) + pltpu.roll(t, w - half, 1) * jnp.tile(up, reps)
                + pltpu.roll(t, half, 1) * jnp.tile(dn, reps))

    def rope_t(dt, cs, up, dn):
        w = dt.shape[1]
        reps = (1, w // HEAD_DIM)
        return (dt * jnp.tile(cs, reps) + pltpu.roll(dt * jnp.tile(up, reps), half, 1)
                + pltpu.roll(dt * jnp.tile(dn, reps), w - half, 1))

    tables = [(t_cs, 0, HEAD_DIM), (t_up, 0, HEAD_DIM), (t_dn, 0, HEAD_DIM)]

    kinds = ("col", "col", "row", "row", "row", "row")
    big_weights = (w_in, w_gu, w_oa, w_ob, w_out, w_down)
    chip_idx = jnp.reshape(chip, (1,)).astype(jnp.int32)
    shard_shapes = [w.shape[1:] for w in big_weights]
    fulls = [[_cast_into(f"cast_w_{l}_{j}", w, l, kinds[j], chip_idx) for j, w in enumerate(big_weights)]
             for l in range(n_layers)]

    def gather_job(l, which):
        return _job_gather([fulls[l][j] for j in which], [shard_shapes[j] for j in which], [kinds[j] for j in which])

    ride_in_proj, ride_ffn_up, ride_attn_out, ride_conv_out, ride_mix_out, ride_ffn_down = (0,), (1,), (2,), (3,), (4,), (5,)
    ahead, ride_attn_0, ride_conv_0 = (0, 2), (3, 4, 5), (1,)
    gathered = [[None] * 6 for _ in range(n_layers)]

    def keep(layer, which, arrays):
        for j, arr in zip(which, arrays):
            gathered[layer][j] = arr

    keep(0, ahead, _run_jobs("gather_w_0", [gather_job(0, ahead)])[0])
    w_dw_all = _gather_small("gather_dw", w_dw, SAME_CORE_CHIPS)
    w_dw_full = jnp.transpose(w_dw_all, (1, 2, 0, 3)).reshape(n_layers, CONV_WIDTH, d)
    w_dw32 = jnp.pad(w_dw_full, ((0, 0), (0, 32 - CONV_WIDTH), (0, 0)))
    sink_b = jnp.broadcast_to(sink[:, :, None], (n_layers, hq, LANE))

    def vec(a, l):
        return a[l][None, :]

    def res_ln(xprev, y, gt, g, b, scn, shn):
        xn = _ln(alpha * xprev + (1.0 + gt) * y, g, b)
        return xn, xn * (1.0 + scn) + shn

    def merge(ya, yb, ga, gb):
        return _sigmoid(ga) * ya + _sigmoid(gb) * yb

    def swiglu(gate, up):
        return gate * _sigmoid(gate) * up

    h = _rowwise("modulate_in", lambda xv, sc, sh: xv * (1.0 + sc) + sh, seq, [(x2, 0, d)], [sc_a[0], sh_a[0]],
                 [(d, BF16)])
    xprev = x2
    saved = []
    for l in range(n_layers):
        nxt = l + 1 < n_layers

        def mm_carrying(name, a_, j_weight, which, out_dtype=F32):
            if not nxt:
                return _mm(name, a_, gathered[l][j_weight], "nn", out_dtype)
            res, (got,) = _mm(name, a_, gathered[l][j_weight], "nn", out_dtype, jobs=[gather_job(l + 1, which)])
            keep(l + 1, which, got)
            return res

        z = mm_carrying(f"in_proj_{l}", h, 0, ride_in_proj, BF16)
        qr, kr, vb = _rowwise(
            f"qkv_prep_{l}", lambda q, k, v, cs, up, dn: (rope(q, cs, up, dn), rope(k, cs, up, dn), v), seq,
            [(z, 0, d), (z, off_k, dkv), (z, off_v, dkv)] + tables, [], [(d, BF16), (dkv, BF16), (dkv, BF16)])
        if l == 0:
            att, (got,) = _attn_fwd(f"attn_{l}", qr, kr, vb, sink_b[l], jobs=[gather_job(0, ride_attn_0)])
            keep(0, ride_attn_0, got)
        else:
            att = _attn_fwd(f"attn_{l}", qr, kr, vb, sink_b[l])
        y_a = mm_carrying(f"attn_out_{l}", att, 2, ride_attn_out, BF16)
        u = _rowwise(f"glu_{l}", lambda a, b: a * _sigmoid(b), seq, [(z, off_ga, d), (z, off_gb, d)], [], [(d, F32)])
        if l == 0:
            (u2, cv), (got,) = _conv_fwd(f"conv_{l}", u, w_dw32[l], vec(conv_ln_g, l), vec(conv_ln_b, l),
                                         jobs=[gather_job(0, ride_conv_0)])
            keep(0, ride_conv_0, got)
        else:
            u2, cv = _conv_fwd(f"conv_{l}", u, w_dw32[l], vec(conv_ln_g, l), vec(conv_ln_b, l))
        y_b = mm_carrying(f"conv_out_{l}", cv, 3, ride_conv_out, BF16)
        mg = _rowwise(f"merge_{l}", merge, seq, [(y_a, 0, d), (y_b, 0, d), (z, off_gta, d), (z, off_gtb, d)], [],
                      [(d, BF16)])
        o = mm_carrying(f"mix_out_{l}", mg, 4, ride_mix_out, BF16)
        x1, h2 = _rowwise(f"res_ln1_{l}", res_ln, seq, [(xprev, 0, d), (o, 0, d)],
                          [gt_a[l], vec(ln1_g, l), vec(ln1_b, l), sc_f[l], sh_f[l]], [(d, F32), (d, BF16)])
        gu = mm_carrying(f"ffn_up_{l}", h2, 1, ride_ffn_up, BF16)
        f = _rowwise(f"swiglu_{l}", swiglu, seq, [(gu, 0, d_ff), (gu, d_ff, d_ff)], [], [(d_ff, BF16)])
        ffn = mm_carrying(f"ffn_down_{l}", f, 5, ride_ffn_down, BF16)
        saved.append(dict(xprev=xprev, h=h, z=z, qr=qr, kr=kr, vb=vb, att=att, u=u, u2=u2, cv=cv, y_a=y_a, y_b=y_b,
                          mg=mg, o=o, x1=x1, h2=h2, gu=gu, f=f, ffn=ffn))
        if l + 1 < n_layers:
            xprev, h = _rowwise(f"res_ln2_{l}", res_ln, seq, [(x1, 0, d), (ffn, 0, d)],
                                [gt_f[l], vec(ln2_g, l), vec(ln2_b, l), sc_a[l + 1], sh_a[l + 1]], [(d, F32), (d, BF16)])

    def res_ln_bwd(xp, y, dxn, dh, gt, g, b, scn, shn):
        _, vjp = jax.vjp(res_ln, xp, y, gt, g, b, scn, shn)
        return vjp((dxn, dh))

    def last_ln_bwd(xp, y, tgt, gt, g, b):
        def head(xp_, y_, gt_, g_, b_):
            return _ln(alpha * xp_ + (1.0 + gt_) * y_, g_, b_)
        out, vjp = jax.vjp(head, xp, y, gt, g, b)
        err = out - tgt
        loss = 0.5 * jnp.sum(jnp.sum(err * err, axis=-1, keepdims=True) / d, axis=0, keepdims=True)
        return vjp(err / d) + (jnp.broadcast_to(loss, (1, LANE)),)

    def merge_bwd(dmg, ya, yb, ga, gb):
        _, vjp = jax.vjp(merge, ya, yb, ga, gb)
        dya, dyb, dga, dgb = vjp(dmg)
        return dya, dyb, jnp.concatenate([dga, dgb], axis=1)

    def swiglu_bwd(df, gate, up):
        _, vjp = jax.vjp(swiglu, gate, up)
        return jnp.concatenate(vjp(df), axis=1)

    vec_d = ((1, d), F32)
    small = [None] * n_layers
    big = None
    loss_part = None
    dxn = dh = None
    pending = None
    stacks = ((w_in, m_w_in, v_w_in), (w_gu, m_w_gu, v_w_gu), (w_oa, m_w_oa, v_w_oa), (w_ob, m_w_ob, v_w_ob),
              (w_out, m_w_out, v_w_out), (w_down, m_w_down, v_w_down))

    def adamw_all(layer, full, prev):
        return [_adamw_layer(f"adamw_{layer}_{j}", layer, full[j], kinds[j], *stacks[j], None if prev is None else prev[j])
                for j in range(6)]

    for l in reversed(range(n_layers)):
        sv = saved[l]
        wi, wg, woa, wob, wout, wdn = gathered[l]
        ln2 = [gt_f[l], vec(ln2_g, l), vec(ln2_b, l)]
        if l + 1 == n_layers:
            dx1, dffn, d_gtf, d_g2, d_b2, loss_part = _rowwise(
                "last_ln_bwd", last_ln_bwd, seq, [(sv["x1"], 0, d), (sv["ffn"], 0, d), (target, 0, d)], ln2,
                [(d, F32), (d, BF16)], [vec_d, vec_d, vec_d, ((1, LANE), F32)])
            d_sca_next = d_sha_next = None
        else:
            dx1, dffn, d_gtf, d_g2, d_b2, d_sca_next, d_sha_next = _rowwise(
                f"res_ln2_bwd_{l}", res_ln_bwd, seq, [(sv["x1"], 0, d), (sv["ffn"], 0, d), (dxn, 0, d), (dh, 0, d)],
                ln2 + [sc_a[l + 1], sh_a[l + 1]], [(d, F32), (d, BF16)], [vec_d] * 5)
            small[l + 1]["sc_a"], small[l + 1]["sh_a"] = d_sca_next, d_sha_next
        def riding(make_job, *job_args):
            return [] if pending is None else [make_job(*job_args)]

        def unpack(res):
            return res if pending is not None else (res, [None])

        df, (got,) = unpack(_mm(f"ffn_down_dx_{l}", dffn, wdn, "nt", BF16, jobs=riding(_job_pair, pending)))
        halves = None if pending is None else [
            _pair_sum(f"pair_sum_{l + 1}_{j}", pending[j], got[j]) for j in range(6)]
        g_down = _mm(f"ffn_down_dw_{l}", sv["f"], dffn, "tn", BF16, split="cols", c_idx=c_idx)
        dgu = _rowwise(f"swiglu_bwd_{l}", swiglu_bwd, seq, [(df, 0, d_ff), (sv["gu"], 0, d_ff), (sv["gu"], d_ff, d_ff)],
                       [], [(2 * d_ff, BF16)])
        dh2, (parts_in,) = unpack(_mm(f"ffn_up_dx_{l}", dgu, wg, "nt", BF16,
                                      jobs=riding(lambda: _job_chips(halves[:1], kinds[:1]))))
        g_gu, (parts_gu,) = unpack(_mm(f"ffn_up_dw_{l}", sv["h2"], dgu, "tn", BF16, split="rows", c_idx=c_idx,
                                       jobs=riding(lambda: _job_chips(halves[1:2], kinds[1:2]))))
        dxp, d_o, d_gta, d_g1, d_b1, d_scf, d_shf = _rowwise(
            f"res_ln1_bwd_{l}", res_ln_bwd, seq, [(sv["xprev"], 0, d), (sv["o"], 0, d), (dx1, 0, d), (dh2, 0, d)],
            [gt_a[l], vec(ln1_g, l), vec(ln1_b, l), sc_f[l], sh_f[l]], [(d, F32), (d, BF16)], [vec_d] * 5)
        dmg = _mm(f"mix_out_dx_{l}", d_o, wout, "nt", BF16)
        g_out = _mm(f"mix_out_dw_{l}", sv["mg"], d_o, "tn", BF16, split="cols", c_idx=c_idx)
        z = sv["z"]
        dya, dyb, d_gates = _rowwise(
            f"merge_bwd_{l}", merge_bwd, seq,
            [(dmg, 0, d), (sv["y_a"], 0, d), (sv["y_b"], 0, d), (z, off_gta, d), (z, off_gtb, d)], [],
            [(d, BF16), (d, BF16), (2 * d, BF16)])
        d_att = _mm(f"attn_out_dx_{l}", dya, woa, "nt", BF16)
        g_oa = _mm(f"attn_out_dw_{l}", sv["att"], dya, "tn", BF16, split="cols", c_idx=c_idx)
        d_cv = _mm(f"conv_out_dx_{l}", dyb, wob, "nt", BF16)
        g_ob = _mm(f"conv_out_dw_{l}", sv["cv"], dyb, "tn", BF16, split="cols", c_idx=c_idx)
        if l > 0:
            du2, d_wdw, d_cg, d_cb = _conv_bwd_a(f"conv_bwd_a_{l}", sv["u"], sv["u2"], d_cv, vec(conv_ln_g, l),
                                                 vec(conv_ln_b, l))
            d_glu = _conv_bwd_b(f"conv_bwd_b_{l}", du2, z, off_ga, off_gb, w_dw32[l])
            dqr, dkr, dvb, d_sink = _attn_bwd(f"attn_bwd_{l}", sv["qr"], sv["kr"], sv["vb"], sink_b[l], d_att)
        else:
            early = {1: g_gu, 2: g_oa, 3: g_ob, 4: g_out, 5: g_down}
            (du2, d_wdw, d_cg, d_cb), (got_0,) = _conv_bwd_a(
                f"conv_bwd_a_{l}", sv["u"], sv["u2"], d_cv, vec(conv_ln_g, l), vec(conv_ln_b, l),
                jobs=[_job_pair(list(early.values()))])
            halves_0 = {j: _pair_sum(f"pair_sum_0_{j}", early[j], got_0[n]) for n, j in enumerate(early)}
            parts_0 = {}
            d_glu, ((parts_0[5],),) = _conv_bwd_b(f"conv_bwd_b_{l}", du2, z, off_ga, off_gb, w_dw32[l],
                                                  jobs=[_job_chips([halves_0[5]], [kinds[5]])])
            (dqr, dkr, dvb, d_sink), ((parts_0[1],),) = _attn_bwd(
                f"attn_bwd_{l}", sv["qr"], sv["kr"], sv["vb"], sink_b[l], d_att,
                jobs=[_job_chips([halves_0[1]], [kinds[1]])])
        d_qkv = _rowwise(
            f"qkv_bwd_{l}",
            lambda dq_, dk_, dv_, cs, up, dn: jnp.concatenate([rope_t(dq_, cs, up, dn), rope_t(dk_, cs, up, dn), dv_], axis=1),
            seq, [(dqr, 0, d), (dkr, 0, dkv), (dvb, 0, dkv)] + tables, [], [(d + 2 * dkv, BF16)])
        dz = jnp.concatenate([d_qkv, d_glu, d_gates], axis=1)
        dh, (parts_rest,) = unpack(_mm(f"in_proj_dx_{l}", dz, wi, "nt", BF16,
                                       jobs=riding(lambda: _job_chips(halves[2:], kinds[2:]))))
        reduced = None if pending is None else [
            _sum_slots(f"sum_chips_{l + 1}_{j}", p, into_slot=c_idx) for j, p in enumerate(parts_in + parts_gu + parts_rest)]
        last_jobs = riding(lambda: _job_join(reduced))
        if l == 0:
            last_jobs = last_jobs + [_job_chips([halves_0[j] for j in (2, 3, 4)], kinds[2:5])]
        g_in = _mm(f"in_proj_dw_{l}", sv["h"], dz, "tn", BF16, split="rows", c_idx=c_idx, jobs=last_jobs)
        if last_jobs:
            g_in, job_res = g_in
            if l == 0:
                parts_0[2], parts_0[3], parts_0[4] = job_res[-1]
            if pending is not None:
                big = adamw_all(l + 1, job_res[0], big)
        dxn = dxp
        small[l] = dict(gt_a=d_gta, sh_f=d_shf, sc_f=d_scf, gt_f=d_gtf, ln1_g=d_g1, ln1_b=d_b1, ln2_g=d_g2, ln2_b=d_b2,
                        conv_ln_g=d_cg, conv_ln_b=d_cb, sink=d_sink[:, :1].reshape(1, hq), w_dw=d_wdw[:CONV_WIDTH])
        pending = [g_in, g_gu, g_oa, g_ob, g_out, g_down]

    ((got_in,),) = _run_jobs("rs_pair_0", [_job_pair(pending[:1])])
    half_in = _pair_sum("pair_sum_0_0", pending[0], got_in)

    grad_x, d_sca0, d_sha0 = _rowwise(
        "modulate_in_bwd", lambda xv, dhv, dxv, sc: (dxv + dhv * (1.0 + sc), jnp.sum(dhv * xv, axis=0, keepdims=True),
                                                     jnp.sum(dhv, axis=0, keepdims=True)),
        seq, [(x2, 0, d), (dh, 0, d), (dxn, 0, d)], [sc_a[0]], [(d, F32)], [vec_d, vec_d])
    small[0]["sc_a"], small[0]["sh_a"] = d_sca0, d_sha0

    order = ("sh_a", "sc_a", "gt_a", "sh_f", "sc_f", "gt_f", "conv_ln_g", "conv_ln_b", "ln1_g", "ln1_b", "ln2_g", "ln2_b")
    rows = []
    for l in range(n_layers):
        rows += [small[l][k] for k in order]
        rows.append(jnp.pad(small[l]["sink"], ((0, 0), (0, d - hq))))
        rows.append(small[l]["w_dw"])
    rows.append(jnp.pad(loss_part, ((0, 0), (0, d - LANE))))
    n_small = sum(r.shape[0] for r in rows)
    pad_rows = (-n_small) % 8
    packed = jnp.concatenate(rows + [jnp.zeros((pad_rows, d), F32)], axis=0)
    everyone = _gather_small("gather_small_grads", packed, ALL_DEVICES)
    total = _sum_slots("sum_small_grads", everyone)
    per_layer = len(order) + 1 + CONV_WIDTH
    tot = total[:n_layers * per_layer].reshape(n_layers, per_layer, d)
    g_mod = tot[:, :N_MOD].reshape(n_layers, N_MOD * d)
    g_small = {k: tot[:, N_MOD + j] for j, k in enumerate(order[N_MOD:])}
    g_sink = tot[:, len(order), :hq]
    g_dw_full = tot[:, len(order) + 1:]
    cols_dw = w_dw.shape[2]
    g_dw = lax.dynamic_slice_in_dim(g_dw_full, chip * cols_dw, cols_dw, axis=2)
    loss = total[n_layers * per_layer, 0]

    d_mod_all = everyone[:, :n_layers * per_layer].reshape(8, n_layers, per_layer, d)[:, :, :N_MOD]
    d_mod_all = d_mod_all.reshape(8, n_layers, N_MOD * d)
    cols_ada = w_ada.shape[2]
    d_mod_mine = lax.dynamic_slice_in_dim(d_mod_all, chip * cols_ada, cols_ada, axis=2)
    dmod16 = jnp.concatenate([d_mod_mine, jnp.zeros_like(d_mod_mine)], axis=0)
    dmod16 = jnp.transpose(dmod16, (1, 0, 2)).astype(BF16)
    ada, ((parts_0[0],),) = _adamw_ada("adamw_ada", c16, dmod16, w_ada, m_w_ada, v_w_ada,
                                       jobs=[_job_chips([half_in], kinds[:1])])
    reduced = [_sum_slots(f"sum_chips_0_{j}", parts_0[j], into_slot=c_idx) for j in range(6)]
    (full,) = _run_jobs("rs_join_0", [_job_join(reduced)])
    big = adamw_all(0, full, big)

    def small_step(name, g, w, m, v):
        shp = w.shape
        g2, w2, m2, v2 = (a.reshape(-1, shp[-1]) for a in (g, w, m, v))
        return (g,) + tuple(a.reshape(shp) for a in _adamw_small(name, g2, w2, m2, v2))

    res = {
        "w_ada": ada,
        "b_ada": small_step("adamw_b_ada", g_mod, b_ada, m_b_ada, v_b_ada),
        "sink": small_step("adamw_sink", g_sink, sink, m_sink, v_sink),
        "w_dw": small_step("adamw_w_dw", g_dw, w_dw, m_w_dw, v_w_dw),
        "conv_ln_g": small_step("adamw_conv_ln_g", g_small["conv_ln_g"], conv_ln_g, m_conv_ln_g, v_conv_ln_g),
        "conv_ln_b": small_step("adamw_conv_ln_b", g_small["conv_ln_b"], conv_ln_b, m_conv_ln_b, v_conv_ln_b),
        "ln1_g": small_step("adamw_ln1_g", g_small["ln1_g"], ln1_g, m_ln1_g, v_ln1_g),
        "ln1_b": small_step("adamw_ln1_b", g_small["ln1_b"], ln1_b, m_ln1_b, v_ln1_b),
        "ln2_g": small_step("adamw_ln2_g", g_small["ln2_g"], ln2_g, m_ln2_g, v_ln2_g),
        "ln2_b": small_step("adamw_ln2_b", g_small["ln2_b"], ln2_b, m_ln2_b, v_ln2_b),
        "w_in": big[0], "w_gu": big[1], "w_oa": big[2], "w_ob": big[3], "w_out": big[4], "w_down": big[5],
    }
    names = ("w_ada", "b_ada", "w_in", "sink", "w_dw", "conv_ln_g", "conv_ln_b", "w_oa", "w_ob", "w_out", "ln1_g", "ln1_b",
             "w_gu", "w_down", "ln2_g", "ln2_b")
    outs = [loss, grad_x[None]]
    for field in range(4):
        outs += [res[k][field] for k in names]
    return tuple(outs)
```
